```python
import math
import jax, jax.numpy as jnp
from jax import lax
import numpy as np

D_MODEL = 2048
BATCH = 2
SEQ = 8192
DEPTH = 4

N_EVEN = (DEPTH + 1) // 2
N_ODD = DEPTH // 2

RWKV_WIDTH = D_MODEL // 2
RWKV_HEAD = 64
RWKV_HEADS = RWKV_WIDTH // RWKV_HEAD
DECAY_LORA = max(32, int(round(1.8 * RWKV_WIDTH ** 0.5 / 32)) * 32)
AAA_LORA = max(32, int(round(1.8 * RWKV_WIDTH ** 0.5 / 32)) * 32)
GATE_LORA = max(32, int(round(0.6 * RWKV_WIDTH ** 0.8 / 32)) * 32)
RWKV_COLS = 3 * RWKV_WIDTH + DECAY_LORA + AAA_LORA + GATE_LORA
RWKV_LNX_EPS = 64e-5

DIFF_WIDTH = D_MODEL - RWKV_WIDTH
DIFF_HEAD = 128
DIFF_HALF = DIFF_HEAD // 2
DIFF_HEADS = DIFF_WIDTH // DIFF_HEAD
DIFF_COLS = 3 * DIFF_WIDTH
Q_BLOCK = 128
ROPE_THETA = 10000.0
NEG_INF = -1e30

EVEN_IN = RWKV_COLS + DIFF_COLS

GMLP_WIDTH = D_MODEL
GMLP_CHUNK = 128
GMLP_GROUPS = 16
GMLP_GROUP_DIM = GMLP_WIDTH // GMLP_GROUPS

FFN_HIDDEN = -(-8 * D_MODEL // (3 * 256)) * 256

NORM_EPS = 1e-6
SUBLN_EPS = 1e-5
LN_EPS = 1e-5

kernel_name = "hybrid_rwkv7_diffattn_gmlp_trunk"


def rms_norm(x, g, eps=NORM_EPS):
    xf = x.astype(jnp.float32)
    y = xf * lax.rsqrt(jnp.mean(xf * xf, axis=-1, keepdims=True) + eps)
    return (y * g.astype(jnp.float32)).astype(x.dtype)


def layer_norm(x, g, b, eps=LN_EPS):
    xf = x.astype(jnp.float32)
    mu = jnp.mean(xf, axis=-1, keepdims=True)
    xc = xf - mu
    y = xc * lax.rsqrt(jnp.mean(xc * xc, axis=-1, keepdims=True) + eps)
    return (y * g.astype(jnp.float32) + b.astype(jnp.float32)).astype(x.dtype)


def rope_tables(T):
    inv = ROPE_THETA ** (-jnp.arange(0, DIFF_HALF, 2, dtype=jnp.float32) / DIFF_HALF)
    ang = jnp.arange(T, dtype=jnp.float32)[:, None] * inv[None, :]
    return jnp.cos(ang), jnp.sin(ang)


def apply_rope(x, cos, sin):
    x1, x2 = jnp.split(x, 2, axis=-1)
    c = cos[:, None, :].astype(x.dtype)
    s = sin[:, None, :].astype(x.dtype)
    return jnp.concatenate([x1 * c - x2 * s, x2 * c + x1 * s], axis=-1)


def rwkv7_scan(r, w, k, v, a, b):
    B, T, H, N = r.shape
    xs = tuple(jnp.moveaxis(t, 1, 0) for t in (r, w, k, v, a, b))

    def step(S, inp):
        r_t, w_t, k_t, v_t, a_t, b_t = inp
        sa = jnp.einsum('bhvk,bhk->bhv', S, a_t)
        S = (S * w_t[:, :, None, :] + sa[..., None] * b_t[:, :, None, :]
             + v_t[..., None] * k_t[:, :, None, :])
        return S, jnp.einsum('bhvk,bhk->bhv', S, r_t)

    S0 = jnp.zeros((B, H, N, N), jnp.float32)
    _, y = lax.scan(step, S0, xs)
    return jnp.moveaxis(y, 0, 1)


def rwkv7_time_mix(z, mu, w0, w_dec_up, a0, w_a_up, w_g_up, k_k, k_a, r_k, lnx_w, lnx_b):
    B, T, _ = z.shape
    H, N, C = RWKV_HEADS, RWKV_HEAD, RWKV_WIDTH
    f32 = jnp.float32
    z_prev = jnp.pad(z, ((0, 0), (1, 0), (0, 0)))[:, :-1]
    z = z + (z_prev - z) * mu
    r, k, v, xw, xa, xg = jnp.split(
        z, [C, 2 * C, 3 * C, 3 * C + DECAY_LORA, 3 * C + DECAY_LORA + AAA_LORA], axis=-1)
    w = -jax.nn.softplus(-(w0 + jnp.tanh(xw) @ w_dec_up)) - 0.5
    decay = jnp.exp(-jnp.exp(w.astype(f32)))
    a = jax.nn.sigmoid(a0 + xa @ w_a_up)
    g = jax.nn.sigmoid(xg) @ w_g_up
    kk = (k * k_k).reshape(B, T, H, N).astype(f32)
    kk = kk * lax.rsqrt(jnp.maximum(jnp.sum(kk * kk, axis=-1, keepdims=True), 1e-24))
    k = k * (1.0 + (a - 1.0) * k_a)
    rh = r.reshape(B, T, H, N).astype(f32)
    kh = k.reshape(B, T, H, N).astype(f32)
    vh = v.reshape(B, T, H, N).astype(f32)
    ah = a.reshape(B, T, H, N).astype(f32)
    y = rwkv7_scan(rh, decay.reshape(B, T, H, N), kh, vh, -kk, kk * ah)
    mean = jnp.mean(y, axis=-1, keepdims=True)
    yc = y - mean
    y = yc * lax.rsqrt(jnp.mean(yc * yc, axis=-1, keepdims=True) + RWKV_LNX_EPS)
    y = y * lnx_w.reshape(H, N).astype(f32) + lnx_b.reshape(H, N).astype(f32)
    bonus = jnp.sum(rh * kh * r_k.astype(f32), axis=-1, keepdims=True) * vh
    out = (y + bonus).reshape(B, T, C) * g.astype(f32)
    return out.astype(z.dtype)


def diff_attention(z, cos, sin, lam_q1, lam_k1, lam_q2, lam_k2, subln_g, lambda_init):
    B, T, _ = z.shape
    H = DIFF_HEADS
    f32 = jnp.float32
    q, k, v = jnp.split(z, 3, axis=-1)
    q = apply_rope(q.reshape(B, T, 2 * H, DIFF_HALF), cos, sin).reshape(B, T, H, 2, DIFF_HALF)
    k = apply_rope(k.reshape(B, T, 2 * H, DIFF_HALF), cos, sin).reshape(B, T, H, 2, DIFF_HALF)
    v = v.reshape(B, T, H, DIFF_HEAD)
    lam = (jnp.exp(jnp.sum(lam_q1.astype(f32) * lam_k1.astype(f32)))
           - jnp.exp(jnp.sum(lam_q2.astype(f32) * lam_k2.astype(f32))) + lambda_init)
    nb = T // Q_BLOCK
    q_blocks = jnp.moveaxis(q.reshape(B, nb, Q_BLOCK, H, 2, DIFF_HALF), 1, 0)
    q_pos = jnp.arange(T).reshape(nb, Q_BLOCK)
    k_pos = jnp.arange(T)
    scale = DIFF_HALF ** -0.5

    def attend(blk):
        qb, qp = blk
        s = jnp.einsum('bqhmd,bkhmd->bhmqk', qb, k).astype(f32) * scale
        s = jnp.where(qp[:, None] >= k_pos[None, :], s, NEG_INF)
        p = jax.nn.softmax(s, axis=-1)
        attn = p[:, :, 0] - lam * p[:, :, 1]
        return jnp.einsum('bhqk,bkhd->bqhd', attn.astype(v.dtype), v)

    o = lax.map(attend, (q_blocks, q_pos))
    o = jnp.moveaxis(o, 0, 1).reshape(B, T, H, DIFF_HEAD)
    o = rms_norm(o, subln_g, SUBLN_EPS) * (1.0 - lambda_init)
    return o.reshape(B, T, DIFF_WIDTH).astype(z.dtype)


def chunked_sgu(z, ln_g, ln_b, w_s, b_s):
    B, T, _ = z.shape
    z = jax.nn.gelu(z, approximate=False)
    u, v = jnp.split(z, 2, axis=-1)
    v = layer_norm(v, ln_g, ln_b)
    v = v.reshape(B, T // GMLP_CHUNK, GMLP_CHUNK, GMLP_GROUPS, GMLP_GROUP_DIM)
    causal = jnp.tril(jnp.ones((GMLP_CHUNK, GMLP_CHUNK), w_s.dtype))
    sv = jnp.einsum('gts,bcsgd->bctgd', w_s * causal, v) + b_s.T[:, :, None]
    return u * sv.reshape(B, T, GMLP_WIDTH)


def swiglu(h, w_gate, w_up, w_down):
    return (jax.nn.silu(h @ w_gate) * (h @ w_up)) @ w_down


def setup_inputs(seed: int = 0) -> dict:
    key = jax.random.key(seed)
    ks = iter(jax.random.split(key, 64))
    f32 = jnp.float32

    def nrm(shape, scale):
        return scale * jax.random.normal(next(ks), shape, f32)

    def gain(shape):
        return 1.0 + nrm(shape, 0.02)

    D, C, NE, NO = D_MODEL, RWKV_WIDTH, N_EVEN, N_ODD
    return {
        'x': nrm((BATCH, SEQ, D), 1.0),
        'mix_norm': gain((DEPTH, D)),
        'ffn_norm': gain((DEPTH, D)),
        'ffn_w_gate': nrm((DEPTH, D, FFN_HIDDEN), D ** -0.5),
        'ffn_w_up': nrm((DEPTH, D, FFN_HIDDEN), D ** -0.5),
        'ffn_w_down': nrm((DEPTH, FFN_HIDDEN, D), FFN_HIDDEN ** -0.5),
        'ev_w_in': nrm((NE, D, EVEN_IN), D ** -0.5),
        'ev_mu': jax.random.uniform(next(ks), (NE, RWKV_COLS), f32),
        'ev_w0': jax.random.uniform(next(ks), (NE, C), f32, -3.0, 1.0),
        'ev_w_dec_up': nrm((NE, DECAY_LORA, C), 0.5 * DECAY_LORA ** -0.5),
        'ev_a0': nrm((NE, C), 0.5),
        'ev_w_a_up': nrm((NE, AAA_LORA, C), 0.5 * AAA_LORA ** -0.5),
        'ev_w_g_up': nrm((NE, GATE_LORA, C), GATE_LORA ** -0.5),
        'ev_k_k': 0.85 + nrm((NE, C), 0.05),
        'ev_k_a': 1.0 + nrm((NE, C), 0.05),
        'ev_r_k': nrm((NE, RWKV_HEADS, RWKV_HEAD), 0.1),
        'ev_lnx_w': gain((NE, C)),
        'ev_lnx_b': nrm((NE, C), 0.02),
        'ev_lam_q1': nrm((NE, DIFF_HALF), 0.1),
        'ev_lam_k1': nrm((NE, DIFF_HALF), 0.1),
        'ev_lam_q2': nrm((NE, DIFF_HALF), 0.1),
        'ev_lam_k2': nrm((NE, DIFF_HALF), 0.1),
        'ev_subln_g': gain((NE, DIFF_HEAD)),
        'ev_w_out': nrm((NE, RWKV_WIDTH + DIFF_WIDTH, D), D ** -0.5),
        'od_w_in': nrm((NO, D, 2 * GMLP_WIDTH), D ** -0.5),
        'od_ln_g': gain((NO, GMLP_WIDTH)),
        'od_ln_b': nrm((NO, GMLP_WIDTH), 0.02),
        'od_w_s': nrm((NO, GMLP_GROUPS, GMLP_CHUNK, GMLP_CHUNK), GMLP_CHUNK ** -0.5),
        'od_b_s': 1.0 + nrm((NO, GMLP_GROUPS, GMLP_CHUNK), 0.02),
        'od_w_out': nrm((NO, GMLP_WIDTH, D), GMLP_WIDTH ** -0.5),
        'final_norm': gain((D,)),
    }


def reference(x, mix_norm, ffn_norm, ffn_w_gate, ffn_w_up, ffn_w_down,
              ev_w_in, ev_mu, ev_w0, ev_w_dec_up, ev_a0, ev_w_a_up, ev_w_g_up,
              ev_k_k, ev_k_a, ev_r_k, ev_lnx_w, ev_lnx_b,
              ev_lam_q1, ev_lam_k1, ev_lam_q2, ev_lam_k2, ev_subln_g, ev_w_out,
              od_w_in, od_ln_g, od_ln_b, od_w_s, od_b_s, od_w_out, final_norm):
    T = x.shape[1]
    cos, sin = rope_tables(T)
    for i in range(DEPTH):
        j = i // 2
        h = rms_norm(x, mix_norm[i])
        if i % 2 == 0:
            lambda_init = 0.8 - 0.6 * math.exp(-0.3 * i)
            z = h @ ev_w_in[j]
            y_a = rwkv7_time_mix(z[..., :RWKV_COLS], ev_mu[j], ev_w0[j], ev_w_dec_up[j],
                                 ev_a0[j], ev_w_a_up[j], ev_w_g_up[j], ev_k_k[j], ev_k_a[j],
                                 ev_r_k[j], ev_lnx_w[j], ev_lnx_b[j])
            y_b = diff_attention(z[..., RWKV_COLS:], cos, sin, ev_lam_q1[j], ev_lam_k1[j],
                                 ev_lam_q2[j], ev_lam_k2[j], ev_subln_g[j], lambda_init)
            y = jnp.concatenate([y_a, y_b], axis=-1) @ ev_w_out[j]
        else:
            z = h @ od_w_in[j]
            y = chunked_sgu(z, od_ln_g[j], od_ln_b[j], od_w_s[j], od_b_s[j]) @ od_w_out[j]
        x = x + y.astype(x.dtype)
        h = rms_norm(x, ffn_norm[i])
        x = x + swiglu(h, ffn_w_gate[i], ffn_w_up[i], ffn_w_down[i]).astype(x.dtype)
    return rms_norm(x, final_norm)
```

```python
import functools
import math

import jax
import jax.numpy as jnp
from jax import lax
from jax.experimental import pallas as pl
from jax.experimental.pallas import tpu as pltpu

F32 = jnp.float32
BF16 = jnp.bfloat16

D_MODEL = 2048
DEPTH = 4
RWKV_WIDTH = D_MODEL // 2
RWKV_HEAD = 64
DECAY_LORA = 64
AAA_LORA = 64
GATE_LORA = 160
RWKV_COLS = 3 * RWKV_WIDTH + DECAY_LORA + AAA_LORA + GATE_LORA
RWKV_LNX_EPS = 64e-5
DIFF_WIDTH = D_MODEL - RWKV_WIDTH
DIFF_HEAD = 128
DIFF_HALF = DIFF_HEAD // 2
DIFF_HEADS = DIFF_WIDTH // DIFF_HEAD
ROPE_THETA = 10000.0
NEG_INF = -1e30
GMLP_WIDTH = D_MODEL
GMLP_CHUNK = 128
GMLP_GROUPS = 16
FFN_HIDDEN = 5632
NORM_EPS = 1e-6
SUBLN_EPS = 1e-5
LN_EPS = 1e-5

LANES = 128
VMEM_LIMIT_BYTES = 56 * 1024 * 1024

RW_XW = 3 * RWKV_WIDTH
RW_XA = RW_XW + LANES
RW_XG = RW_XA + LANES
RW_PAD_COLS = RW_XG + 2 * LANES

RWKV_CHUNK = 64
PAIR = 2 * RWKV_HEAD


def _cparams(semantics):
    return pltpu.CompilerParams(dimension_semantics=semantics, vmem_limit_bytes=VMEM_LIMIT_BYTES)


def _dot(a, b):
    return jnp.dot(a, b, preferred_element_type=F32)


def _dot_nt(a, b):
    return lax.dot_general(a, b, (((1,), (1,)), ((), ())), preferred_element_type=F32)


def _rms_norm_rows(x, g, eps):
    ms = jnp.mean(x * x, axis=-1, keepdims=True)
    return x * lax.rsqrt(ms + eps) * g


def _norm_matmul_kernel(x_ref, g_ref, w_ref, o_ref, h_ref, *, gelu):
    @pl.when(pl.program_id(1) == 0)
    def _():
        h_ref[...] = _rms_norm_rows(x_ref[...], g_ref[...], NORM_EPS).astype(BF16)

    acc = _dot(h_ref[...], w_ref[...])
    if gelu:
        acc = 0.5 * acc * (1.0 + lax.erf(acc * math.sqrt(0.5)))
    o_ref[...] = acc.astype(o_ref.dtype)


def _norm_matmul(x, g, w, *, tm, tn, out_dtype, gelu=False):
    m, k = x.shape
    n = w.shape[1]
    return pl.pallas_call(
        functools.partial(_norm_matmul_kernel, gelu=gelu),
        grid=(m // tm, n // tn),
        in_specs=[
            pl.BlockSpec((tm, k), lambda i, j: (i, 0)),
            pl.BlockSpec((1, k), lambda i, j: (0, 0)),
            pl.BlockSpec((k, tn), lambda i, j: (0, j)),
        ],
        out_specs=pl.BlockSpec((tm, tn), lambda i, j: (i, j)),
        out_shape=jax.ShapeDtypeStruct((m, n), out_dtype),
        scratch_shapes=[pltpu.VMEM((tm, k), BF16)],
        compiler_params=_cparams(("arbitrary", "arbitrary")),
        name="norm_matmul_gelu" if gelu else "norm_matmul",
    )(x, g, w)


def _norm_matmul_rope_kernel(x_ref, g_ref, w_ref, cos_ref, sin_ref, o_ref, h_ref, *, tn, q_tiles, rope_tiles):
    j = pl.program_id(1)

    @pl.when(j == 0)
    def _():
        h_ref[...] = _rms_norm_rows(x_ref[...], g_ref[...], NORM_EPS).astype(BF16)

    acc = _dot(h_ref[...], w_ref[...])

    @pl.when(j < rope_tiles)
    def _():
        reps = tn // LANES
        c = jnp.concatenate([cos_ref[...]] * reps, axis=1)
        s = jnp.concatenate([sin_ref[...]] * reps, axis=1)
        lane = lax.broadcasted_iota(jnp.int32, acc.shape, 1)
        partner = jnp.where((lane & (DIFF_HALF // 2)) == 0,
                            pltpu.roll(acc, tn - DIFF_HALF // 2, 1),
                            pltpu.roll(acc, DIFF_HALF // 2, 1))
        scale = jnp.where(j < q_tiles, DIFF_HALF ** -0.5, 1.0).astype(F32)
        o_ref[...] = ((acc * c + partner * s) * scale).astype(o_ref.dtype)

    @pl.when(j >= rope_tiles)
    def _():
        o_ref[...] = acc.astype(o_ref.dtype)


def _norm_matmul_rope(x, g, w, cos, sin, *, tm, tn, seq):
    m, k = x.shape
    n = w.shape[1]
    q_tiles = DIFF_WIDTH // tn
    t_tiles = seq // tm
    return pl.pallas_call(
        functools.partial(_norm_matmul_rope_kernel, tn=tn, q_tiles=q_tiles, rope_tiles=2 * q_tiles),
        grid=(m // tm, n // tn),
        in_specs=[
            pl.BlockSpec((tm, k), lambda i, j: (i, 0)),
            pl.BlockSpec((1, k), lambda i, j: (0, 0)),
            pl.BlockSpec((k, tn), lambda i, j: (0, j)),
            pl.BlockSpec((tm, LANES), lambda i, j: (i % t_tiles, 0)),
            pl.BlockSpec((tm, LANES), lambda i, j: (i % t_tiles, 0)),
        ],
        out_specs=pl.BlockSpec((tm, tn), lambda i, j: (i, j)),
        out_shape=jax.ShapeDtypeStruct((m, n), BF16),
        scratch_shapes=[pltpu.VMEM((tm, k), BF16)],
        compiler_params=_cparams(("arbitrary", "arbitrary")),
        name="norm_matmul_rope",
    )(x, g, w, cos, sin)


def _softplus(y):
    return jnp.maximum(y, 0.0) + jnp.log1p(jnp.exp(-jnp.abs(y)))


def _sigmoid(y):
    return 1.0 / (1.0 + jnp.exp(-y))


def _split_bf16(x, parts):
    out = []
    rem = x
    for _ in range(parts):
        p = rem.astype(BF16)
        out.append(p)
        rem = rem - p.astype(F32)
    return out


def _rwkv_kernel(z_ref, mu_ref, w0_ref, wdec_ref, a0_ref, wa_ref, wg_ref, kk_ref, ka_ref, rk_ref,
                 lw_ref, lb_ref, o_ref, s_ref, prev_ref):
    L = RWKV_CHUNK
    C = RWKV_WIDTH
    n_pairs = C // PAIR

    @pl.when(pl.program_id(1) == 0)
    def _():
        s_ref[...] = jnp.zeros_like(s_ref)
        prev_ref[...] = jnp.zeros_like(prev_ref)

    z = z_ref[...]
    row = lax.broadcasted_iota(jnp.int32, z.shape, 0)
    z_prev = jnp.where(row == 0, prev_ref[0:1, :], pltpu.roll(z, 1, 0))
    prev_ref[0:1, :] = z[L - 1:L, :]
    zs = z + (z_prev - z) * mu_ref[...]

    r = zs[:, 0:C]
    k = zs[:, C:2 * C]
    v = zs[:, 2 * C:3 * C]
    xw = zs[:, RW_XW:RW_XA]
    xa = zs[:, RW_XA:RW_XG]
    xg = zs[:, RW_XG:RW_PAD_COLS]

    w = -_softplus(-(w0_ref[...] + _dot(jnp.tanh(xw).astype(BF16), wdec_ref[...]))) - 0.5
    logd = -jnp.exp(w)
    a_lr = _sigmoid(a0_ref[...] + _dot(xa.astype(BF16), wa_ref[...]))
    gate = _dot(_sigmoid(xg).astype(BF16), wg_ref[...])
    kk_raw = k * kk_ref[...]
    k2 = k * (1.0 + (a_lr - 1.0) * ka_ref[...])
    rkk = r * k2 * rk_ref[...]

    ti = lax.broadcasted_iota(jnp.int32, (L, L), 0)
    tj = lax.broadcasted_iota(jnp.int32, (L, L), 1)
    tril_ones = jnp.where(ti >= tj, 1.0, 0.0).astype(BF16)
    cum = sum(_dot(tril_ones, p) for p in _split_bf16(logd, 3))

    rr = lax.broadcasted_iota(jnp.int32, (PAIR, PAIR), 0)
    cc = lax.broadcasted_iota(jnp.int32, (PAIR, PAIR), 1)
    same_head = (rr // RWKV_HEAD) == (cc // RWKV_HEAD)
    strict = jnp.logical_and(same_head, rr > cc)
    incl = jnp.logical_and(same_head, rr >= cc)
    head_ones = jnp.where(same_head, 1.0, 0.0).astype(BF16)
    eye = jnp.where(rr == cc, 1.0, 0.0).astype(F32)
    first_head = lax.broadcasted_iota(jnp.int32, (L, PAIR), 1) < RWKV_HEAD

    def head_sum(x):
        hi, lo = _split_bf16(x, 2)
        both = _dot(jnp.concatenate([hi, lo], axis=0), head_ones)
        return both[:L] + both[L:]

    def stack(x):
        return jnp.concatenate([jnp.where(first_head, x, 0.0), jnp.where(first_head, 0.0, x)], axis=0)

    for p in range(n_pairs):
        sl = slice(p * PAIR, (p + 1) * PAIR)
        kk_p = kk_raw[:, sl]
        kk_n = kk_p * lax.rsqrt(jnp.maximum(head_sum(kk_p * kk_p), 1e-24))
        a_p = -kk_n
        b_p = kk_n * a_lr[:, sl]
        r_p, k_p, v_p = r[:, sl], k2[:, sl], v[:, sl]
        cu = cum[:, sl]
        c_end = cu[L - 1:L, :]
        e_pos = jnp.exp(cu)
        e_neg = jnp.exp(-cu)
        e_prev = jnp.exp(cu - logd[:, sl])
        e_end = jnp.exp(c_end - cu)

        lhs = jnp.concatenate([stack(a_p * e_prev), stack(r_p * e_pos)], axis=0).astype(BF16)
        rhs = jnp.concatenate([stack(b_p * e_neg), stack(k_p * e_neg)], axis=0).astype(BF16)
        pm = _dot_nt(lhs, rhs)
        p_ab = jnp.where(strict, pm[:PAIR, :PAIR], 0.0)
        p_ak = jnp.where(strict, pm[:PAIR, PAIR:], 0.0)
        p_rb = jnp.where(incl, pm[PAIR:, :PAIR], 0.0)
        p_rk = jnp.where(incl, pm[PAIR:, PAIR:], 0.0)

        x_pow = p_ab.astype(BF16)
        t_inv = eye + p_ab
        x_pow = _dot(x_pow, x_pow)
        n_steps = int(math.log2(L)) - 1
        for it in range(n_steps):
            xb = x_pow.astype(BF16)
            if it + 1 < n_steps:
                both = _dot(xb, jnp.concatenate([xb, t_inv.astype(BF16)], axis=1))
                x_pow = both[:, :PAIR]
                t_inv = t_inv + both[:, PAIR:]
            else:
                t_inv = t_inv + _dot(xb, t_inv.astype(BF16))

        s_old = s_ref[p]
        sh = _dot_nt(lhs, s_old.astype(BF16))
        v_st = stack(v_p)
        rhs_u = sh[:PAIR] + _dot(p_ak.astype(BF16), v_st.astype(BF16))
        u_st = _dot(t_inv.astype(BF16), rhs_u.astype(BF16))
        uv = jnp.concatenate([u_st, v_st], axis=0)
        y_st = sh[PAIR:] + _dot(jnp.concatenate([p_rb, p_rk], axis=1).astype(BF16), uv.astype(BF16))
        y = y_st[:L] + y_st[L:]
        bk_end = jnp.concatenate([stack(b_p * e_end), stack(k_p * e_end)], axis=0).astype(BF16)
        s_ref[p] = s_old * jnp.exp(c_end) + _dot(uv.T.astype(BF16), bk_end)

        mean = head_sum(y) * (1.0 / RWKV_HEAD)
        yc = y - mean
        var = head_sum(yc * yc) * (1.0 / RWKV_HEAD)
        yn = yc * lax.rsqrt(var + RWKV_LNX_EPS) * lw_ref[:, sl] + lb_ref[:, sl]
        bonus = head_sum(rkk[:, sl]) * v_p
        o_ref[:, sl] = ((yn + bonus) * gate[:, sl]).astype(o_ref.dtype)


def _rwkv_time_mix(z, mu, w0, wdec, a0, wa, wg, k_k, k_a, r_k, lnx_w, lnx_b, *, batch, seq):
    L = RWKV_CHUNK
    n_chunks = seq // L
    n_pairs = RWKV_WIDTH // PAIR
    full = lambda shape: pl.BlockSpec(shape, lambda b, c: (0,) * len(shape))
    return pl.pallas_call(
        _rwkv_kernel,
        grid=(batch, n_chunks),
        in_specs=[
            pl.BlockSpec((L, RW_PAD_COLS), lambda b, c: (b * n_chunks + c, 0)),
            full((1, RW_PAD_COLS)),
            full((1, RWKV_WIDTH)), full((LANES, RWKV_WIDTH)),
            full((1, RWKV_WIDTH)), full((LANES, RWKV_WIDTH)),
            full((2 * LANES, RWKV_WIDTH)),
            full((1, RWKV_WIDTH)), full((1, RWKV_WIDTH)), full((1, RWKV_WIDTH)),
            full((1, RWKV_WIDTH)), full((1, RWKV_WIDTH)),
        ],
        out_specs=pl.BlockSpec((L, RWKV_WIDTH), lambda b, c: (b * n_chunks + c, 0)),
        out_shape=jax.ShapeDtypeStruct((batch * seq, RWKV_WIDTH), BF16),
        scratch_shapes=[pltpu.VMEM((n_pairs, PAIR, PAIR), F32), pltpu.VMEM((8, RW_PAD_COLS), F32)],
        compiler_params=_cparams(("arbitrary", "arbitrary")),
        name="rwkv7_time_mix",
    )(z, mu, w0, wdec, a0, wa, wg, k_k, k_a, r_k, lnx_w, lnx_b)


def _attn_kernel(lam_ref, g_ref, q_ref, k_ref, v_ref, o_ref, *, blk, lambda_init):
    i = pl.program_id(2)
    q = q_ref[...]
    lane = lax.broadcasted_iota(jnp.int32, q.shape, 1)
    zero = jnp.zeros_like(q)
    qq = jnp.concatenate([jnp.where(lane < DIFF_HALF, q, zero), jnp.where(lane < DIFF_HALF, zero, q)], axis=0)

    def step(j, carry, masked):
        m, l, acc = carry
        start = pl.multiple_of(j * blk, blk)
        kb = k_ref[pl.ds(start, blk), :]
        vb = v_ref[pl.ds(start, blk), :]
        s = _dot_nt(qq, kb)
        if masked:
            qi = lax.broadcasted_iota(jnp.int32, s.shape, 0) % blk
            kj = lax.broadcasted_iota(jnp.int32, s.shape, 1)
            s = jnp.where(qi >= kj, s, NEG_INF)
        m_new = jnp.maximum(m, jnp.max(s, axis=-1, keepdims=True))
        alpha = jnp.exp(m - m_new)
        pr = jnp.exp(s - m_new)
        l_new = alpha * l + jnp.sum(pr, axis=-1, keepdims=True)
        acc_new = alpha * acc + _dot(pr.astype(BF16), vb)
        return m_new, l_new, acc_new

    init = (jnp.full((2 * blk, 1), NEG_INF, F32), jnp.zeros((2 * blk, 1), F32), jnp.zeros((2 * blk, DIFF_HEAD), F32))
    carry = lax.fori_loop(0, i, functools.partial(step, masked=False), init)
    _, l, acc = step(i, carry, True)
    o = acc / l
    lam_p = lam_ref[...]
    lam = (jnp.exp(jnp.sum(lam_p[0:1] * lam_p[1:2], axis=-1, keepdims=True))
           - jnp.exp(jnp.sum(lam_p[2:3] * lam_p[3:4], axis=-1, keepdims=True)) + lambda_init)
    d = o[:blk] - lam * o[blk:]
    d = _rms_norm_rows(d, g_ref[...], SUBLN_EPS) * (1.0 - lambda_init)
    o_ref[...] = d.astype(o_ref.dtype)


def _diff_attention(qkv, lam_params, subln_g, *, batch, seq, blk, lambda_init):
    nq = seq // blk
    h = DIFF_HEADS
    return pl.pallas_call(
        functools.partial(_attn_kernel, blk=blk, lambda_init=lambda_init),
        grid=(batch, h, nq),
        in_specs=[
            pl.BlockSpec((8, LANES), lambda b, hh, i: (0, 0)),
            pl.BlockSpec((1, DIFF_HEAD), lambda b, hh, i: (0, 0)),
            pl.BlockSpec((blk, DIFF_HEAD), lambda b, hh, i: (b * nq + i, hh)),
            pl.BlockSpec((seq, DIFF_HEAD), lambda b, hh, i: (b, h + hh)),
            pl.BlockSpec((seq, DIFF_HEAD), lambda b, hh, i: (b, 2 * h + hh)),
        ],
        out_specs=pl.BlockSpec((blk, DIFF_HEAD), lambda b, hh, i: (b * nq + i, hh)),
        out_shape=jax.ShapeDtypeStruct((batch * seq, DIFF_WIDTH), BF16),
        compiler_params=_cparams(("arbitrary", "arbitrary", "arbitrary")),
        name="diff_attention",
    )(lam_params, subln_g, qkv, qkv, qkv)


def _outproj2_kernel(ya_ref, yb_ref, x_ref, wa_ref, wb_ref, o_ref):
    o_ref[...] = x_ref[...] + _dot(ya_ref[...], wa_ref[...]) + _dot(yb_ref[...], wb_ref[...])


def _outproj2(ya, yb, x, wa, wb, *, tm):
    m, d = x.shape
    return pl.pallas_call(
        _outproj2_kernel,
        grid=(m // tm,),
        in_specs=[
            pl.BlockSpec((tm, ya.shape[1]), lambda i: (i, 0)),
            pl.BlockSpec((tm, yb.shape[1]), lambda i: (i, 0)),
            pl.BlockSpec((tm, d), lambda i: (i, 0)),
            pl.BlockSpec(wa.shape, lambda i: (0, 0)),
            pl.BlockSpec(wb.shape, lambda i: (0, 0)),
        ],
        out_specs=pl.BlockSpec((tm, d), lambda i: (i, 0)),
        out_shape=jax.ShapeDtypeStruct((m, d), F32),
        compiler_params=_cparams(("arbitrary",)),
        name="even_out_proj",
    )(ya, yb, x, wa, wb)


def _sgu_kernel(u_ref, v_ref, x_ref, lng_ref, lnb_ref, ws_ref, bs_ref, wo_ref, o_ref, vn_ref, gated_ref, *, tm):
    ch = GMLP_CHUNK
    v = v_ref[...].astype(F32)
    mu = jnp.mean(v, axis=-1, keepdims=True)
    vc = v - mu
    var = jnp.mean(vc * vc, axis=-1, keepdims=True)
    vn_ref[...] = (vc * lax.rsqrt(var + LN_EPS) * lng_ref[...] + lnb_ref[...]).astype(BF16)
    ti = lax.broadcasted_iota(jnp.int32, (ch, ch), 0)
    tj = lax.broadcasted_iota(jnp.int32, (ch, ch), 1)
    causal = ti >= tj
    for g in range(GMLP_GROUPS):
        cols = slice(g * LANES, (g + 1) * LANES)
        wg = jnp.where(causal, ws_ref[g], 0.0).astype(BF16)
        bias = bs_ref[:, g:g + 1]
        for c in range(tm // ch):
            rows = slice(c * ch, (c + 1) * ch)
            sv = _dot(wg, vn_ref[rows, cols]) + bias
            gated_ref[rows, cols] = (u_ref[rows, cols].astype(F32) * sv).astype(BF16)
    o_ref[...] = x_ref[...] + _dot(gated_ref[...], wo_ref[...])


def _sgu_outproj(u_v, x, ln_g, ln_b, w_s, b_s_t, w_out, *, tm):
    m, d = x.shape
    n_half = GMLP_WIDTH // d
    return pl.pallas_call(
        functools.partial(_sgu_kernel, tm=tm),
        grid=(m // tm,),
        in_specs=[
            pl.BlockSpec((tm, GMLP_WIDTH), lambda i: (i, 0)),
            pl.BlockSpec((tm, GMLP_WIDTH), lambda i: (i, n_half)),
            pl.BlockSpec((tm, d), lambda i: (i, 0)),
            pl.BlockSpec((1, GMLP_WIDTH), lambda i: (0, 0)),
            pl.BlockSpec((1, GMLP_WIDTH), lambda i: (0, 0)),
            pl.BlockSpec(w_s.shape, lambda i: (0, 0, 0)),
            pl.BlockSpec(b_s_t.shape, lambda i: (0, 0)),
            pl.BlockSpec(w_out.shape, lambda i: (0, 0)),
        ],
        out_specs=pl.BlockSpec((tm, d), lambda i: (i, 0)),
        out_shape=jax.ShapeDtypeStruct((m, d), F32),
        scratch_shapes=[pltpu.VMEM((tm, GMLP_WIDTH), BF16), pltpu.VMEM((tm, GMLP_WIDTH), BF16)],
        compiler_params=_cparams(("arbitrary",)),
        name="sgu_out_proj",
    )(u_v, u_v, x, ln_g, ln_b, w_s, b_s_t, w_out)


def _ffn_kernel(x_ref, g_ref, wg_ref, wu_ref, wd_ref, fg_ref, o_ref, h_ref, acc_ref, *, final):
    j = pl.program_id(1)

    @pl.when(j == 0)
    def _():
        h_ref[...] = _rms_norm_rows(x_ref[...], g_ref[...], NORM_EPS).astype(BF16)
        acc_ref[...] = jnp.zeros_like(acc_ref)

    h = h_ref[...]
    gate = _dot(h, wg_ref[...])
    up = _dot(h, wu_ref[...])
    act = gate * _sigmoid(gate) * up
    acc_ref[...] += _dot(act.astype(BF16), wd_ref[...])

    @pl.when(j == pl.num_programs(1) - 1)
    def _():
        y = x_ref[...] + acc_ref[...]
        if final:
            y = _rms_norm_rows(y, fg_ref[...], NORM_EPS)
        o_ref[...] = y


def _ffn(x, g, w_gate, w_up, w_down, final_g, *, tm, th, final):
    m, d = x.shape
    hid = w_gate.shape[1]
    return pl.pallas_call(
        functools.partial(_ffn_kernel, final=final),
        grid=(m // tm, hid // th),
        in_specs=[
            pl.BlockSpec((tm, d), lambda i, j: (i, 0)),
            pl.BlockSpec((1, d), lambda i, j: (0, 0)),
            pl.BlockSpec((d, th), lambda i, j: (0, j)),
            pl.BlockSpec((d, th), lambda i, j: (0, j)),
            pl.BlockSpec((th, d), lambda i, j: (j, 0)),
            pl.BlockSpec((1, d), lambda i, j: (0, 0)),
        ],
        out_specs=pl.BlockSpec((tm, d), lambda i, j: (i, 0)),
        out_shape=jax.ShapeDtypeStruct((m, d), F32),
        scratch_shapes=[pltpu.VMEM((tm, d), BF16), pltpu.VMEM((tm, d), F32)],
        compiler_params=_cparams(("arbitrary", "arbitrary")),
        name="swiglu_ffn",
    )(x, g, w_gate, w_up, w_down, final_g)


def _pad_cols(a, width):
    return jnp.pad(a, ((0, 0), (0, width - a.shape[1])))


def _pad_rows(a, height):
    return jnp.pad(a, ((0, height - a.shape[0]), (0, 0)))


def _rwkv_col_layout(a):
    c = RWKV_WIDTH
    xw = a[:, 3 * c:3 * c + DECAY_LORA]
    xa = a[:, 3 * c + DECAY_LORA:3 * c + DECAY_LORA + AAA_LORA]
    xg = a[:, 3 * c + DECAY_LORA + AAA_LORA:]
    return jnp.concatenate(
        [a[:, :3 * c], _pad_cols(xw, LANES), _pad_cols(xa, LANES), _pad_cols(xg, 2 * LANES)], axis=1)


def _rope_tables(seq):
    inv = ROPE_THETA ** (-jnp.arange(0, DIFF_HALF, 2, dtype=F32) / DIFF_HALF)
    ang = jnp.arange(seq, dtype=F32)[:, None] * inv[None, :]
    cos, sin = jnp.cos(ang), jnp.sin(ang)
    reps = LANES // DIFF_HALF
    return (jnp.tile(jnp.concatenate([cos, cos], axis=1), (1, reps)),
            jnp.tile(jnp.concatenate([-sin, sin], axis=1), (1, reps)))


def _tile_sizes(m, seq):
    return dict(
        proj_tm=min(512, seq), proj_tn=512,
        ffn_tm=min(512, m), ffn_th=512,
        out_tm=min(512, m), sgu_tm=min(256, m),
        attn_blk=min(256, seq),
    )


def kernel(x, mix_norm, ffn_norm, ffn_w_gate, ffn_w_up, ffn_w_down, ev_w_in, ev_mu, ev_w0, ev_w_dec_up, ev_a0,
           ev_w_a_up, ev_w_g_up, ev_k_k, ev_k_a, ev_r_k, ev_lnx_w, ev_lnx_b, ev_lam_q1, ev_lam_k1, ev_lam_q2,
           ev_lam_k2, ev_subln_g, ev_w_out, od_w_in, od_ln_g, od_ln_b, od_w_s, od_b_s, od_w_out, final_norm):
    batch, seq, d = x.shape
    m = batch * seq
    ts = _tile_sizes(m, seq)
    cos, sin = _rope_tables(seq)
    row = lambda a: a.reshape(1, -1).astype(F32)
    xf = x.reshape(m, d)
    for i in range(DEPTH):
        j = i // 2
        g_mix = row(mix_norm[i])
        if i % 2 == 0:
            lambda_init = 0.8 - 0.6 * math.exp(-0.3 * i)
            w_in = ev_w_in[j]
            w_rwkv = _rwkv_col_layout(w_in[:, :RWKV_COLS]).astype(BF16)
            w_diff = w_in[:, RWKV_COLS:].astype(BF16)
            z = _norm_matmul(xf, g_mix, w_rwkv, tm=ts["proj_tm"], tn=ts["proj_tn"], out_dtype=F32)
            qkv = _norm_matmul_rope(xf, g_mix, w_diff, cos, sin, tm=ts["proj_tm"], tn=ts["proj_tn"], seq=seq)
            y_a = _rwkv_time_mix(
                z, _rwkv_col_layout(row(ev_mu[j])), row(ev_w0[j]),
                _pad_rows(ev_w_dec_up[j], LANES).astype(BF16), row(ev_a0[j]),
                _pad_rows(ev_w_a_up[j], LANES).astype(BF16), _pad_rows(ev_w_g_up[j], 2 * LANES).astype(BF16),
                row(ev_k_k[j]), row(ev_k_a[j]), row(ev_r_k[j]), row(ev_lnx_w[j]), row(ev_lnx_b[j]),
                batch=batch, seq=seq)
            lam_params = _pad_rows(_pad_cols(
                jnp.stack([ev_lam_q1[j], ev_lam_k1[j], ev_lam_q2[j], ev_lam_k2[j]]).astype(F32), LANES), 8)
            y_b = _diff_attention(qkv, lam_params, row(ev_subln_g[j]), batch=batch, seq=seq,
                                  blk=ts["attn_blk"], lambda_init=lambda_init)
            w_out = ev_w_out[j].astype(BF16)
            xf = _outproj2(y_a, y_b, xf, w_out[:RWKV_WIDTH], w_out[RWKV_WIDTH:], tm=ts["out_tm"])
        else:
            u_v = _norm_matmul(xf, g_mix, od_w_in[j].astype(BF16), tm=ts["proj_tm"], tn=ts["proj_tn"],
                               out_dtype=BF16, gelu=True)
            xf = _sgu_outproj(u_v, xf, row(od_ln_g[j]), row(od_ln_b[j]), od_w_s[j].astype(F32),
                              od_b_s[j].T.astype(F32), od_w_out[j].astype(BF16), tm=ts["sgu_tm"])
        xf = _ffn(xf, row(ffn_norm[i]), ffn_w_gate[i].astype(BF16), ffn_w_up[i].astype(BF16),
                  ffn_w_down[i].astype(BF16), row(final_norm), tm=ts["ffn_tm"], th=ts["ffn_th"],
                  final=(i == DEPTH - 1))
    return xf.reshape(batch, seq, d)
```

```python
import functools
import math

import jax
import jax.numpy as jnp
from jax import lax
from jax.experimental import pallas as pl
from jax.experimental.pallas import tpu as pltpu

F32 = jnp.float32
BF16 = jnp.bfloat16

D_MODEL = 2048
DEPTH = 4
RWKV_WIDTH = D_MODEL // 2
RWKV_HEAD = 64
DECAY_LORA = 64
AAA_LORA = 64
GATE_LORA = 160
RWKV_COLS = 3 * RWKV_WIDTH + DECAY_LORA + AAA_LORA + GATE_LORA
RWKV_LNX_EPS = 64e-5
DIFF_WIDTH = D_MODEL - RWKV_WIDTH
DIFF_HEAD = 128
DIFF_HALF = DIFF_HEAD // 2
DIFF_HEADS = DIFF_WIDTH // DIFF_HEAD
ROPE_THETA = 10000.0
NEG_INF = -1e30
GMLP_WIDTH = D_MODEL
GMLP_CHUNK = 128
GMLP_GROUPS = 16
FFN_HIDDEN = 5632
NORM_EPS = 1e-6
SUBLN_EPS = 1e-5
LN_EPS = 1e-5

LANES = 128
VMEM_LIMIT_BYTES = 56 * 1024 * 1024

RW_XW = 3 * RWKV_WIDTH
RW_XA = RW_XW + LANES
RW_XG = RW_XA + LANES
RW_PAD_COLS = RW_XG + 2 * LANES

RWKV_CHUNK = 64
ATTN_SUB = 128
PAIR = 2 * RWKV_HEAD


def _cparams(semantics):
    return pltpu.CompilerParams(dimension_semantics=semantics, vmem_limit_bytes=VMEM_LIMIT_BYTES)


def _dot(a, b):
    return jnp.dot(a, b, preferred_element_type=F32)


def _dot_nt(a, b):
    return lax.dot_general(a, b, (((1,), (1,)), ((), ())), preferred_element_type=F32)


def _rms_norm_rows(x, g, eps):
    ms = jnp.mean(x * x, axis=-1, keepdims=True)
    return x * lax.rsqrt(ms + eps) * g


def _norm_matmul_kernel(x_ref, g_ref, w_ref, o_ref, h_ref, *, gelu):
    @pl.when(pl.program_id(1) == 0)
    def _():
        h_ref[...] = _rms_norm_rows(x_ref[...], g_ref[...], NORM_EPS).astype(BF16)

    acc = _dot(h_ref[...], w_ref[...])
    if gelu:
        acc = 0.5 * acc * (1.0 + lax.erf(acc * math.sqrt(0.5)))
    o_ref[...] = acc.astype(o_ref.dtype)


def _norm_matmul(x, g, w, *, tm, tn, out_dtype, gelu=False):
    m, k = x.shape
    n = w.shape[1]
    return pl.pallas_call(
        functools.partial(_norm_matmul_kernel, gelu=gelu),
        grid=(m // tm, n // tn),
        in_specs=[
            pl.BlockSpec((tm, k), lambda i, j: (i, 0)),
            pl.BlockSpec((1, k), lambda i, j: (0, 0)),
            pl.BlockSpec((k, tn), lambda i, j: (0, j)),
        ],
        out_specs=pl.BlockSpec((tm, tn), lambda i, j: (i, j)),
        out_shape=jax.ShapeDtypeStruct((m, n), out_dtype),
        scratch_shapes=[pltpu.VMEM((tm, k), BF16)],
        compiler_params=_cparams(("arbitrary", "arbitrary")),
        name="norm_matmul_gelu" if gelu else "norm_matmul",
    )(x, g, w)


def _norm_matmul_rope_kernel(x_ref, g_ref, w_ref, cos_ref, sin_ref, o_ref, h_ref, *, tn, q_tiles, rope_tiles):
    j = pl.program_id(1)

    @pl.when(j == 0)
    def _():
        h_ref[...] = _rms_norm_rows(x_ref[...], g_ref[...], NORM_EPS).astype(BF16)

    acc = _dot(h_ref[...], w_ref[...])

    @pl.when(j < rope_tiles)
    def _():
        reps = tn // LANES
        c = jnp.concatenate([cos_ref[...]] * reps, axis=1)
        s = jnp.concatenate([sin_ref[...]] * reps, axis=1)
        lane = lax.broadcasted_iota(jnp.int32, acc.shape, 1)
        partner = jnp.where((lane & (DIFF_HALF // 2)) == 0,
                            pltpu.roll(acc, tn - DIFF_HALF // 2, 1),
                            pltpu.roll(acc, DIFF_HALF // 2, 1))
        scale = jnp.where(j < q_tiles, DIFF_HALF ** -0.5, 1.0).astype(F32)
        o_ref[...] = ((acc * c + partner * s) * scale).astype(o_ref.dtype)

    @pl.when(j >= rope_tiles)
    def _():
        o_ref[...] = acc.astype(o_ref.dtype)


def _norm_matmul_rope(x, g, w, cos, sin, *, tm, tn, seq):
    m, k = x.shape
    n = w.shape[1]
    q_tiles = DIFF_WIDTH // tn
    t_tiles = seq // tm
    return pl.pallas_call(
        functools.partial(_norm_matmul_rope_kernel, tn=tn, q_tiles=q_tiles, rope_tiles=2 * q_tiles),
        grid=(m // tm, n // tn),
        in_specs=[
            pl.BlockSpec((tm, k), lambda i, j: (i, 0)),
            pl.BlockSpec((1, k), lambda i, j: (0, 0)),
            pl.BlockSpec((k, tn), lambda i, j: (0, j)),
            pl.BlockSpec((tm, LANES), lambda i, j: (i % t_tiles, 0)),
            pl.BlockSpec((tm, LANES), lambda i, j: (i % t_tiles, 0)),
        ],
        out_specs=pl.BlockSpec((tm, tn), lambda i, j: (i, j)),
        out_shape=jax.ShapeDtypeStruct((m, n), BF16),
        scratch_shapes=[pltpu.VMEM((tm, k), BF16)],
        compiler_params=_cparams(("arbitrary", "arbitrary")),
        name="norm_matmul_rope",
    )(x, g, w, cos, sin)


def _softplus(y):
    return jnp.maximum(y, 0.0) + jnp.log1p(jnp.exp(-jnp.abs(y)))


def _sigmoid(y):
    return 1.0 / (1.0 + jnp.exp(-y))


def _split_bf16(x, parts):
    out = []
    rem = x
    for _ in range(parts):
        p = rem.astype(BF16)
        out.append(p)
        rem = rem - p.astype(F32)
    return out


def _rwkv_kernel(z_ref, mu_ref, w0_ref, wdec_ref, a0_ref, wa_ref, wg_ref, kk_ref, ka_ref, rk_ref,
                 lw_ref, lb_ref, o_ref, s_ref, prev_ref):
    L = RWKV_CHUNK
    C = RWKV_WIDTH
    n_pairs = C // PAIR

    @pl.when(pl.program_id(1) == 0)
    def _():
        s_ref[...] = jnp.zeros_like(s_ref)
        prev_ref[...] = jnp.zeros_like(prev_ref)

    z = z_ref[...]
    row = lax.broadcasted_iota(jnp.int32, z.shape, 0)
    z_prev = jnp.where(row == 0, prev_ref[0:1, :], pltpu.roll(z, 1, 0))
    prev_ref[0:1, :] = z[L - 1:L, :]
    zs = z + (z_prev - z) * mu_ref[...]

    r = zs[:, 0:C]
    k = zs[:, C:2 * C]
    v = zs[:, 2 * C:3 * C]
    xw = zs[:, RW_XW:RW_XA]
    xa = zs[:, RW_XA:RW_XG]
    xg = zs[:, RW_XG:RW_PAD_COLS]

    w = -_softplus(-(w0_ref[...] + _dot(jnp.tanh(xw).astype(BF16), wdec_ref[...]))) - 0.5
    logd = -jnp.exp(w)
    a_lr = _sigmoid(a0_ref[...] + _dot(xa.astype(BF16), wa_ref[...]))
    gate = _dot(_sigmoid(xg).astype(BF16), wg_ref[...])
    kk_raw = k * kk_ref[...]
    k2 = k * (1.0 + (a_lr - 1.0) * ka_ref[...])
    rkk = r * k2 * rk_ref[...]

    ti = lax.broadcasted_iota(jnp.int32, (L, L), 0)
    tj = lax.broadcasted_iota(jnp.int32, (L, L), 1)
    tril_ones = jnp.where(ti >= tj, 1.0, 0.0).astype(BF16)
    cum = sum(_dot(tril_ones, p) for p in _split_bf16(logd, 3))

    rr = lax.broadcasted_iota(jnp.int32, (PAIR, PAIR), 0)
    cc = lax.broadcasted_iota(jnp.int32, (PAIR, PAIR), 1)
    same_head = (rr // RWKV_HEAD) == (cc // RWKV_HEAD)
    strict = jnp.logical_and(same_head, rr > cc)
    incl = jnp.logical_and(same_head, rr >= cc)
    head_ones = jnp.where(same_head, 1.0, 0.0).astype(BF16)
    eye = jnp.where(rr == cc, 1.0, 0.0).astype(F32)
    first_head = lax.broadcasted_iota(jnp.int32, (L, PAIR), 1) < RWKV_HEAD

    def head_sum(x):
        hi, lo = _split_bf16(x, 2)
        both = _dot(jnp.concatenate([hi, lo], axis=0), head_ones)
        return both[:L] + both[L:]

    def stack(x):
        return jnp.concatenate([jnp.where(first_head, x, 0.0), jnp.where(first_head, 0.0, x)], axis=0)

    for p in range(n_pairs):
        sl = slice(p * PAIR, (p + 1) * PAIR)
        kk_p = kk_raw[:, sl]
        kk_n = kk_p * lax.rsqrt(jnp.maximum(head_sum(kk_p * kk_p), 1e-24))
        a_p = -kk_n
        b_p = kk_n * a_lr[:, sl]
        r_p, k_p, v_p = r[:, sl], k2[:, sl], v[:, sl]
        cu = cum[:, sl]
        c_end = cu[L - 1:L, :]
        e_pos = jnp.exp(cu)
        e_neg = jnp.exp(-cu)
        e_prev = jnp.exp(cu - logd[:, sl])
        e_end = jnp.exp(c_end - cu)

        lhs = jnp.concatenate([stack(a_p * e_prev), stack(r_p * e_pos)], axis=0).astype(BF16)
        rhs = jnp.concatenate([stack(b_p * e_neg), stack(k_p * e_neg)], axis=0).astype(BF16)
        pm = _dot_nt(lhs, rhs)
        p_ab = jnp.where(strict, pm[:PAIR, :PAIR], 0.0)
        p_ak = jnp.where(strict, pm[:PAIR, PAIR:], 0.0)
        p_rb = jnp.where(incl, pm[PAIR:, :PAIR], 0.0)
        p_rk = jnp.where(incl, pm[PAIR:, PAIR:], 0.0)

        x_pow = p_ab.astype(BF16)
        t_inv = eye + p_ab
        x_pow = _dot(x_pow, x_pow)
        n_steps = int(math.log2(L)) - 1
        for it in range(n_steps):
            xb = x_pow.astype(BF16)
            if it + 1 < n_steps:
                both = _dot(xb, jnp.concatenate([xb, t_inv.astype(BF16)], axis=1))
                x_pow = both[:, :PAIR]
                t_inv = t_inv + both[:, PAIR:]
            else:
                t_inv = t_inv + _dot(xb, t_inv.astype(BF16))

        s_old = s_ref[p]
        sh = _dot_nt(lhs, s_old.astype(BF16))
        v_st = stack(v_p)
        rhs_u = sh[:PAIR] + _dot(p_ak.astype(BF16), v_st.astype(BF16))
        u_st = _dot(t_inv.astype(BF16), rhs_u.astype(BF16))
        uv = jnp.concatenate([u_st, v_st], axis=0)
        y_st = sh[PAIR:] + _dot(jnp.concatenate([p_rb, p_rk], axis=1).astype(BF16), uv.astype(BF16))
        y = y_st[:L] + y_st[L:]
        bk_end = jnp.concatenate([stack(b_p * e_end), stack(k_p * e_end)], axis=0).astype(BF16)
        s_ref[p] = s_old * jnp.exp(c_end) + _dot(uv.T.astype(BF16), bk_end)

        mean = head_sum(y) * (1.0 / RWKV_HEAD)
        yc = y - mean
        var = head_sum(yc * yc) * (1.0 / RWKV_HEAD)
        yn = yc * lax.rsqrt(var + RWKV_LNX_EPS) * lw_ref[:, sl] + lb_ref[:, sl]
        bonus = head_sum(rkk[:, sl]) * v_p
        o_ref[:, sl] = ((yn + bonus) * gate[:, sl]).astype(o_ref.dtype)


def _rwkv_time_mix(z, mu, w0, wdec, a0, wa, wg, k_k, k_a, r_k, lnx_w, lnx_b, *, batch, seq):
    L = RWKV_CHUNK
    n_chunks = seq // L
    n_pairs = RWKV_WIDTH // PAIR
    full = lambda shape: pl.BlockSpec(shape, lambda b, c: (0,) * len(shape))
    return pl.pallas_call(
        _rwkv_kernel,
        grid=(batch, n_chunks),
        in_specs=[
            pl.BlockSpec((L, RW_PAD_COLS), lambda b, c: (b * n_chunks + c, 0)),
            full((1, RW_PAD_COLS)),
            full((1, RWKV_WIDTH)), full((LANES, RWKV_WIDTH)),
            full((1, RWKV_WIDTH)), full((LANES, RWKV_WIDTH)),
            full((2 * LANES, RWKV_WIDTH)),
            full((1, RWKV_WIDTH)), full((1, RWKV_WIDTH)), full((1, RWKV_WIDTH)),
            full((1, RWKV_WIDTH)), full((1, RWKV_WIDTH)),
        ],
        out_specs=pl.BlockSpec((L, RWKV_WIDTH), lambda b, c: (b * n_chunks + c, 0)),
        out_shape=jax.ShapeDtypeStruct((batch * seq, RWKV_WIDTH), BF16),
        scratch_shapes=[pltpu.VMEM((n_pairs, PAIR, PAIR), F32), pltpu.VMEM((8, RW_PAD_COLS), F32)],
        compiler_params=_cparams(("arbitrary", "arbitrary")),
        name="rwkv7_time_mix",
    )(z, mu, w0, wdec, a0, wa, wg, k_k, k_a, r_k, lnx_w, lnx_b)


def _attn_kernel(lam_ref, g_ref, q_ref, k_ref, v_ref, o_ref, vt_ref, acc_ref, qq_ref, s_ref, *, seq, n_sub,
                 lambda_init):
    i = pl.program_id(2)
    sb = ATTN_SUB
    two = 2 * sb

    @pl.when(i == 0)
    def _():
        def transpose_block(t, carry):
            start = pl.multiple_of(t * sb, sb)
            vt_ref[t] = v_ref[pl.ds(start, sb), :].astype(F32).T.astype(BF16)
            return carry
        lax.fori_loop(0, seq // sb, transpose_block, 0)

    lane = lax.broadcasted_iota(jnp.int32, (sb, DIFF_HEAD), 1)
    for c in range(n_sub):
        qc = q_ref[c * sb:(c + 1) * sb, :]
        zero = jnp.zeros_like(qc)
        qq_ref[c] = jnp.concatenate([jnp.where(lane < DIFF_HALF, qc, zero),
                                     jnp.where(lane < DIFF_HALF, zero, qc)], axis=0)
        acc_ref[c] = jnp.zeros((DIFF_HEAD, two), F32)

    key_idx = lax.broadcasted_iota(jnp.int32, (sb, two), 0)
    qry_idx = lax.broadcasted_iota(jnp.int32, (sb, two), 1) & (sb - 1)
    causal = key_idx <= qry_idx

    def scores(c, j):
        start = pl.multiple_of(j * sb, sb)
        return _dot_nt(k_ref[pl.ds(start, sb), :], qq_ref[c])

    def softmax_pv(c, j, st, m, l, masked):
        if masked:
            st = jnp.where(causal, st, NEG_INF)
        m_new = jnp.maximum(m, jnp.max(st, axis=0, keepdims=True))
        alpha = jnp.exp(m - m_new)
        pr = jnp.exp(st - m_new)
        l_new = alpha * l + jnp.sum(pr, axis=0, keepdims=True)
        acc_ref[c] = alpha * acc_ref[c] + _dot(vt_ref[j], pr.astype(BF16))
        return m_new, l_new

    for c in range(n_sub):
        s_ref[0, c] = scores(c, 0)

    def two_blocks(t, carry):
        ms, ls = list(carry[0]), list(carry[1])
        for half in range(2):
            j = 2 * t + half
            for c in range(n_sub):
                s_ref[1 - half, c] = scores(c, j + 1)
            for c in range(n_sub):
                ms[c], ls[c] = softmax_pv(c, j, s_ref[half, c], ms[c], ls[c], False)
        return tuple(ms), tuple(ls)

    init = (tuple(jnp.full((1, two), NEG_INF, F32) for _ in range(n_sub)),
            tuple(jnp.zeros((1, two), F32) for _ in range(n_sub)))
    ms, ls = lax.fori_loop(0, (i * n_sub) // 2, two_blocks, init)
    ms, ls = list(ms), list(ls)
    first_diag = i * n_sub
    tail = [[scores(c, first_diag + jj) for c in range(jj, n_sub)] for jj in range(1, n_sub)]
    for c in range(n_sub):
        ms[c], ls[c] = softmax_pv(c, first_diag, s_ref[0, c], ms[c], ls[c], c == 0)
    for jj in range(1, n_sub):
        for c in range(jj, n_sub):
            ms[c], ls[c] = softmax_pv(c, first_diag + jj, tail[jj - 1][c - jj], ms[c], ls[c], c == jj)

    lam_p = lam_ref[...]
    lam = (jnp.exp(jnp.sum(lam_p[0:1] * lam_p[1:2], axis=-1, keepdims=True))
           - jnp.exp(jnp.sum(lam_p[2:3] * lam_p[3:4], axis=-1, keepdims=True)) + lambda_init)
    for c in range(n_sub):
        o = acc_ref[c] / ls[c]
        d = o[:, :sb] - lam * o[:, sb:]
        ms_d = jnp.mean(d * d, axis=0, keepdims=True)
        d = d * lax.rsqrt(ms_d + SUBLN_EPS) * g_ref[...] * (1.0 - lambda_init)
        o_ref[c * sb:(c + 1) * sb, :] = d.T.astype(o_ref.dtype)


def _diff_attention(qkv, lam_params, subln_g_col, *, batch, seq, n_sub, lambda_init):
    qb = n_sub * ATTN_SUB
    nq = seq // qb
    h = DIFF_HEADS
    return pl.pallas_call(
        functools.partial(_attn_kernel, seq=seq, n_sub=n_sub, lambda_init=lambda_init),
        grid=(batch, h, nq),
        in_specs=[
            pl.BlockSpec((8, LANES), lambda b, hh, i: (0, 0)),
            pl.BlockSpec((DIFF_HEAD, 1), lambda b, hh, i: (0, 0)),
            pl.BlockSpec((qb, DIFF_HEAD), lambda b, hh, i: (b * nq + i, hh)),
            pl.BlockSpec((seq, DIFF_HEAD), lambda b, hh, i: (b, h + hh)),
            pl.BlockSpec((seq, DIFF_HEAD), lambda b, hh, i: (b, 2 * h + hh)),
        ],
        out_specs=pl.BlockSpec((qb, DIFF_HEAD), lambda b, hh, i: (b * nq + i, hh)),
        out_shape=jax.ShapeDtypeStruct((batch * seq, DIFF_WIDTH), BF16),
        scratch_shapes=[pltpu.VMEM((seq // ATTN_SUB, DIFF_HEAD, ATTN_SUB), BF16),
                        pltpu.VMEM((n_sub, DIFF_HEAD, 2 * ATTN_SUB), F32),
                        pltpu.VMEM((n_sub, 2 * ATTN_SUB, DIFF_HEAD), BF16),
                        pltpu.VMEM((2, n_sub, ATTN_SUB, 2 * ATTN_SUB), F32)],
        compiler_params=_cparams(("arbitrary", "arbitrary", "arbitrary")),
        name="diff_attention",
    )(lam_params, subln_g_col, qkv, qkv, qkv)


def _outproj2_kernel(ya_ref, yb_ref, x_ref, wa_ref, wb_ref, o_ref):
    o_ref[...] = x_ref[...] + _dot(ya_ref[...], wa_ref[...]) + _dot(yb_ref[...], wb_ref[...])


def _outproj2(ya, yb, x, wa, wb, *, tm):
    m, d = x.shape
    return pl.pallas_call(
        _outproj2_kernel,
        grid=(m // tm,),
        in_specs=[
            pl.BlockSpec((tm, ya.shape[1]), lambda i: (i, 0)),
            pl.BlockSpec((tm, yb.shape[1]), lambda i: (i, 0)),
            pl.BlockSpec((tm, d), lambda i: (i, 0)),
            pl.BlockSpec(wa.shape, lambda i: (0, 0)),
            pl.BlockSpec(wb.shape, lambda i: (0, 0)),
        ],
        out_specs=pl.BlockSpec((tm, d), lambda i: (i, 0)),
        out_shape=jax.ShapeDtypeStruct((m, d), F32),
        compiler_params=_cparams(("arbitrary",)),
        name="even_out_proj",
    )(ya, yb, x, wa, wb)


def _sgu_kernel(u_ref, v_ref, x_ref, lng_ref, lnb_ref, ws_ref, bs_ref, wo_ref, o_ref, vn_ref, gated_ref, *, tm):
    ch = GMLP_CHUNK
    v = v_ref[...].astype(F32)
    mu = jnp.mean(v, axis=-1, keepdims=True)
    vc = v - mu
    var = jnp.mean(vc * vc, axis=-1, keepdims=True)
    vn_ref[...] = (vc * lax.rsqrt(var + LN_EPS) * lng_ref[...] + lnb_ref[...]).astype(BF16)
    ti = lax.broadcasted_iota(jnp.int32, (ch, ch), 0)
    tj = lax.broadcasted_iota(jnp.int32, (ch, ch), 1)
    causal = ti >= tj
    for g in range(GMLP_GROUPS):
        cols = slice(g * LANES, (g + 1) * LANES)
        wg = jnp.where(causal, ws_ref[g], 0.0).astype(BF16)
        bias = bs_ref[:, g:g + 1]
        for c in range(tm // ch):
            rows = slice(c * ch, (c + 1) * ch)
            sv = _dot(wg, vn_ref[rows, cols]) + bias
            gated_ref[rows, cols] = (u_ref[rows, cols].astype(F32) * sv).astype(BF16)
    o_ref[...] = x_ref[...] + _dot(gated_ref[...], wo_ref[...])


def _sgu_outproj(u_v, x, ln_g, ln_b, w_s, b_s_t, w_out, *, tm):
    m, d = x.shape
    n_half = GMLP_WIDTH // d
    return pl.pallas_call(
        functools.partial(_sgu_kernel, tm=tm),
        grid=(m // tm,),
        in_specs=[
            pl.BlockSpec((tm, GMLP_WIDTH), lambda i: (i, 0)),
            pl.BlockSpec((tm, GMLP_WIDTH), lambda i: (i, n_half)),
            pl.BlockSpec((tm, d), lambda i: (i, 0)),
            pl.BlockSpec((1, GMLP_WIDTH), lambda i: (0, 0)),
            pl.BlockSpec((1, GMLP_WIDTH), lambda i: (0, 0)),
            pl.BlockSpec(w_s.shape, lambda i: (0, 0, 0)),
            pl.BlockSpec(b_s_t.shape, lambda i: (0, 0)),
            pl.BlockSpec(w_out.shape, lambda i: (0, 0)),
        ],
        out_specs=pl.BlockSpec((tm, d), lambda i: (i, 0)),
        out_shape=jax.ShapeDtypeStruct((m, d), F32),
        scratch_shapes=[pltpu.VMEM((tm, GMLP_WIDTH), BF16), pltpu.VMEM((tm, GMLP_WIDTH), BF16)],
        compiler_params=_cparams(("arbitrary",)),
        name="sgu_out_proj",
    )(u_v, u_v, x, ln_g, ln_b, w_s, b_s_t, w_out)


def _ffn_kernel(x_ref, g_ref, wg_ref, wu_ref, wd_ref, fg_ref, o_ref, h_ref, acc_ref, *, final):
    j = pl.program_id(1)

    @pl.when(j == 0)
    def _():
        h_ref[...] = _rms_norm_rows(x_ref[...], g_ref[...], NORM_EPS).astype(BF16)
        acc_ref[...] = jnp.zeros_like(acc_ref)

    h = h_ref[...]
    gate = _dot(h, wg_ref[...])
    up = _dot(h, wu_ref[...])
    act = gate * _sigmoid(gate) * up
    acc_ref[...] += _dot(act.astype(BF16), wd_ref[...])

    @pl.when(j == pl.num_programs(1) - 1)
    def _():
        y = x_ref[...] + acc_ref[...]
        if final:
            y = _rms_norm_rows(y, fg_ref[...], NORM_EPS)
        o_ref[...] = y


def _ffn(x, g, w_gate, w_up, w_down, final_g, *, tm, th, final):
    m, d = x.shape
    hid = w_gate.shape[1]
    return pl.pallas_call(
        functools.partial(_ffn_kernel, final=final),
        grid=(m // tm, hid // th),
        in_specs=[
            pl.BlockSpec((tm, d), lambda i, j: (i, 0)),
            pl.BlockSpec((1, d), lambda i, j: (0, 0)),
            pl.BlockSpec((d, th), lambda i, j: (0, j)),
            pl.BlockSpec((d, th), lambda i, j: (0, j)),
            pl.BlockSpec((th, d), lambda i, j: (j, 0)),
            pl.BlockSpec((1, d), lambda i, j: (0, 0)),
        ],
        out_specs=pl.BlockSpec((tm, d), lambda i, j: (i, 0)),
        out_shape=jax.ShapeDtypeStruct((m, d), F32),
        scratch_shapes=[pltpu.VMEM((tm, d), BF16), pltpu.VMEM((tm, d), F32)],
        compiler_params=_cparams(("arbitrary", "arbitrary")),
        name="swiglu_ffn",
    )(x, g, w_gate, w_up, w_down, final_g)


def _pad_cols(a, width):
    return jnp.pad(a, ((0, 0), (0, width - a.shape[1])))


def _pad_rows(a, height):
    return jnp.pad(a, ((0, height - a.shape[0]), (0, 0)))


def _rwkv_col_layout(a):
    c = RWKV_WIDTH
    xw = a[:, 3 * c:3 * c + DECAY_LORA]
    xa = a[:, 3 * c + DECAY_LORA:3 * c + DECAY_LORA + AAA_LORA]
    xg = a[:, 3 * c + DECAY_LORA + AAA_LORA:]
    return jnp.concatenate(
        [a[:, :3 * c], _pad_cols(xw, LANES), _pad_cols(xa, LANES), _pad_cols(xg, 2 * LANES)], axis=1)


def _rope_tables(seq):
    inv = ROPE_THETA ** (-jnp.arange(0, DIFF_HALF, 2, dtype=F32) / DIFF_HALF)
    ang = jnp.arange(seq, dtype=F32)[:, None] * inv[None, :]
    cos, sin = jnp.cos(ang), jnp.sin(ang)
    reps = LANES // DIFF_HALF
    return (jnp.tile(jnp.concatenate([cos, cos], axis=1), (1, reps)),
            jnp.tile(jnp.concatenate([-sin, sin], axis=1), (1, reps)))


def _tile_sizes(m, seq):
    return dict(
        proj_tm=min(512, seq), proj_tn=512,
        ffn_tm=min(512, m), ffn_th=512,
        out_tm=min(512, m), sgu_tm=min(256, m),
        attn_sub=min(4, seq // ATTN_SUB),
    )


def kernel(x, mix_norm, ffn_norm, ffn_w_gate, ffn_w_up, ffn_w_down, ev_w_in, ev_mu, ev_w0, ev_w_dec_up, ev_a0,
           ev_w_a_up, ev_w_g_up, ev_k_k, ev_k_a, ev_r_k, ev_lnx_w, ev_lnx_b, ev_lam_q1, ev_lam_k1, ev_lam_q2,
           ev_lam_k2, ev_subln_g, ev_w_out, od_w_in, od_ln_g, od_ln_b, od_w_s, od_b_s, od_w_out, final_norm):
    batch, seq, d = x.shape
    m = batch * seq
    ts = _tile_sizes(m, seq)
    cos, sin = _rope_tables(seq)
    row = lambda a: a.reshape(1, -1).astype(F32)
    xf = x.reshape(m, d)
    for i in range(DEPTH):
        j = i // 2
        g_mix = row(mix_norm[i])
        if i % 2 == 0:
            lambda_init = 0.8 - 0.6 * math.exp(-0.3 * i)
            w_in = ev_w_in[j]
            w_rwkv = _rwkv_col_layout(w_in[:, :RWKV_COLS]).astype(BF16)
            w_diff = w_in[:, RWKV_COLS:].astype(BF16)
            z = _norm_matmul(xf, g_mix, w_rwkv, tm=ts["proj_tm"], tn=ts["proj_tn"], out_dtype=F32)
            qkv = _norm_matmul_rope(xf, g_mix, w_diff, cos, sin, tm=ts["proj_tm"], tn=ts["proj_tn"], seq=seq)
            y_a = _rwkv_time_mix(
                z, _rwkv_col_layout(row(ev_mu[j])), row(ev_w0[j]),
                _pad_rows(ev_w_dec_up[j], LANES).astype(BF16), row(ev_a0[j]),
                _pad_rows(ev_w_a_up[j], LANES).astype(BF16), _pad_rows(ev_w_g_up[j], 2 * LANES).astype(BF16),
                row(ev_k_k[j]), row(ev_k_a[j]), row(ev_r_k[j]), row(ev_lnx_w[j]), row(ev_lnx_b[j]),
                batch=batch, seq=seq)
            lam_params = _pad_rows(_pad_cols(
                jnp.stack([ev_lam_q1[j], ev_lam_k1[j], ev_lam_q2[j], ev_lam_k2[j]]).astype(F32), LANES), 8)
            y_b = _diff_attention(qkv, lam_params, ev_subln_g[j].reshape(-1, 1).astype(F32), batch=batch, seq=seq,
                                  n_sub=ts["attn_sub"], lambda_init=lambda_init)
            w_out = ev_w_out[j].astype(BF16)
            xf = _outproj2(y_a, y_b, xf, w_out[:RWKV_WIDTH], w_out[RWKV_WIDTH:], tm=ts["out_tm"])
        else:
            u_v = _norm_matmul(xf, g_mix, od_w_in[j].astype(BF16), tm=ts["proj_tm"], tn=ts["proj_tn"],
                               out_dtype=BF16, gelu=True)
            xf = _sgu_outproj(u_v, xf, row(od_ln_g[j]), row(od_ln_b[j]), od_w_s[j].astype(F32),
                              od_b_s[j].T.astype(F32), od_w_out[j].astype(BF16), tm=ts["sgu_tm"])
        xf = _ffn(xf, row(ffn_norm[i]), ffn_w_gate[i].astype(BF16), ffn_w_up[i].astype(BF16),
                  ffn_w_down[i].astype(BF16), row(final_norm), tm=ts["ffn_tm"], th=ts["ffn_th"],
                  final=(i == DEPTH - 1))
    return xf.reshape(batch, seq, d)
```

```python
import functools
import math

import jax
import jax.numpy as jnp
from jax import lax
from jax.experimental import pallas as pl
from jax.experimental.pallas import tpu as pltpu

F32 = jnp.float32
BF16 = jnp.bfloat16

D_MODEL = 2048
DEPTH = 4
RWKV_WIDTH = D_MODEL // 2
RWKV_HEAD = 64
DECAY_LORA = 64
AAA_LORA = 64
GATE_LORA = 160
RWKV_COLS = 3 * RWKV_WIDTH + DECAY_LORA + AAA_LORA + GATE_LORA
RWKV_LNX_EPS = 64e-5
DIFF_WIDTH = D_MODEL - RWKV_WIDTH
DIFF_HEAD = 128
DIFF_HALF = DIFF_HEAD // 2
DIFF_HEADS = DIFF_WIDTH // DIFF_HEAD
ROPE_THETA = 10000.0
NEG_INF = -1e30
GMLP_WIDTH = D_MODEL
GMLP_CHUNK = 128
GMLP_GROUPS = 16
FFN_HIDDEN = 5632
NORM_EPS = 1e-6
SUBLN_EPS = 1e-5
LN_EPS = 1e-5

LANES = 128
VMEM_LIMIT_BYTES = 56 * 1024 * 1024

RW_XW = 3 * RWKV_WIDTH
RW_XA = RW_XW + LANES
RW_XG = RW_XA + LANES
RW_PAD_COLS = RW_XG + 2 * LANES

RWKV_CHUNK = 64
RWKV_GROUP = 8
ATTN_SUB = 128
ATTN_KEYS = 256
LOG2_E = 1.4426950408889634
PAIR = 2 * RWKV_HEAD


def _cparams(semantics):
    return pltpu.CompilerParams(dimension_semantics=semantics, vmem_limit_bytes=VMEM_LIMIT_BYTES)


def _dot(a, b):
    return jnp.dot(a, b, preferred_element_type=F32)


def _dot_nt(a, b):
    return lax.dot_general(a, b, (((1,), (1,)), ((), ())), preferred_element_type=F32)


def _rms_norm_rows(x, g, eps):
    ms = jnp.mean(x * x, axis=-1, keepdims=True)
    return x * lax.rsqrt(ms + eps) * g


def _norm_matmul_kernel(x_ref, g_ref, w_ref, o_ref, h_ref, *, gelu):
    @pl.when(pl.program_id(1) == 0)
    def _():
        h_ref[...] = _rms_norm_rows(x_ref[...], g_ref[...], NORM_EPS).astype(BF16)

    acc = _dot(h_ref[...], w_ref[...])
    if gelu:
        acc = 0.5 * acc * (1.0 + lax.erf(acc * math.sqrt(0.5)))
    o_ref[...] = acc.astype(o_ref.dtype)


def _norm_matmul(x, g, w, *, tm, tn, out_dtype, gelu=False):
    m, k = x.shape
    n = w.shape[1]
    return pl.pallas_call(
        functools.partial(_norm_matmul_kernel, gelu=gelu),
        grid=(m // tm, n // tn),
        in_specs=[
            pl.BlockSpec((tm, k), lambda i, j: (i, 0)),
            pl.BlockSpec((1, k), lambda i, j: (0, 0)),
            pl.BlockSpec((k, tn), lambda i, j: (0, j)),
        ],
        out_specs=pl.BlockSpec((tm, tn), lambda i, j: (i, j)),
        out_shape=jax.ShapeDtypeStruct((m, n), out_dtype),
        scratch_shapes=[pltpu.VMEM((tm, k), BF16)],
        compiler_params=_cparams(("arbitrary", "arbitrary")),
        name="norm_matmul_gelu" if gelu else "norm_matmul",
    )(x, g, w)


def _norm_matmul_rope_kernel(x_ref, g_ref, w_ref, cos_ref, sin_ref, o_ref, h_ref, *, tn, q_tiles, rope_tiles):
    j = pl.program_id(1)

    @pl.when(j == 0)
    def _():
        h_ref[...] = _rms_norm_rows(x_ref[...], g_ref[...], NORM_EPS).astype(BF16)

    acc = _dot(h_ref[...], w_ref[...])

    @pl.when(j < rope_tiles)
    def _():
        reps = tn // LANES
        c = jnp.concatenate([cos_ref[...]] * reps, axis=1)
        s = jnp.concatenate([sin_ref[...]] * reps, axis=1)
        lane = lax.broadcasted_iota(jnp.int32, acc.shape, 1)
        partner = jnp.where((lane & (DIFF_HALF // 2)) == 0,
                            pltpu.roll(acc, tn - DIFF_HALF // 2, 1),
                            pltpu.roll(acc, DIFF_HALF // 2, 1))
        scale = jnp.where(j < q_tiles, DIFF_HALF ** -0.5 * LOG2_E, 1.0).astype(F32)
        o_ref[...] = ((acc * c + partner * s) * scale).astype(o_ref.dtype)

    @pl.when(j >= rope_tiles)
    def _():
        o_ref[...] = acc.astype(o_ref.dtype)


def _norm_matmul_rope(x, g, w, cos, sin, *, tm, tn, seq):
    m, k = x.shape
    n = w.shape[1]
    q_tiles = DIFF_WIDTH // tn
    t_tiles = seq // tm
    return pl.pallas_call(
        functools.partial(_norm_matmul_rope_kernel, tn=tn, q_tiles=q_tiles, rope_tiles=2 * q_tiles),
        grid=(m // tm, n // tn),
        in_specs=[
            pl.BlockSpec((tm, k), lambda i, j: (i, 0)),
            pl.BlockSpec((1, k), lambda i, j: (0, 0)),
            pl.BlockSpec((k, tn), lambda i, j: (0, j)),
            pl.BlockSpec((tm, LANES), lambda i, j: (i % t_tiles, 0)),
            pl.BlockSpec((tm, LANES), lambda i, j: (i % t_tiles, 0)),
        ],
        out_specs=pl.BlockSpec((tm, tn), lambda i, j: (i, j)),
        out_shape=jax.ShapeDtypeStruct((m, n), BF16),
        scratch_shapes=[pltpu.VMEM((tm, k), BF16)],
        compiler_params=_cparams(("arbitrary", "arbitrary")),
        name="norm_matmul_rope",
    )(x, g, w, cos, sin)


def _softplus(y):
    return jnp.maximum(y, 0.0) + jnp.log1p(jnp.exp(-jnp.abs(y)))


def _sigmoid(y):
    return 1.0 / (1.0 + jnp.exp(-y))


def _split_bf16(x, parts):
    out = []
    rem = x
    for _ in range(parts):
        p = rem.astype(BF16)
        out.append(p)
        rem = rem - p.astype(F32)
    return out


def _rwkv_kernel(z_ref, mu_ref, w0_ref, wdec_ref, a0_ref, wa_ref, wg_ref, kk_ref, ka_ref, rk_ref,
                 lw_ref, lb_ref, o_ref, s_ref, prev_ref):
    L = RWKV_CHUNK
    C = RWKV_WIDTH
    n_pairs = C // PAIR

    @pl.when(pl.program_id(1) == 0)
    def _():
        s_ref[...] = jnp.zeros_like(s_ref)
        prev_ref[...] = jnp.zeros_like(prev_ref)

    z = z_ref[...].astype(F32)
    row = lax.broadcasted_iota(jnp.int32, z.shape, 0)
    z_prev = jnp.where(row == 0, prev_ref[0:1, :], pltpu.roll(z, 1, 0))
    prev_ref[0:1, :] = z[L - 1:L, :]
    zs = z + (z_prev - z) * mu_ref[...]

    r = zs[:, 0:C]
    k = zs[:, C:2 * C]
    v = zs[:, 2 * C:3 * C]
    xw = zs[:, RW_XW:RW_XA]
    xa = zs[:, RW_XA:RW_XG]
    xg = zs[:, RW_XG:RW_PAD_COLS]

    w = -_softplus(-(w0_ref[...] + _dot(jnp.tanh(xw).astype(BF16), wdec_ref[...]))) - 0.5
    logd = -jnp.exp(w)
    a_lr = _sigmoid(a0_ref[...] + _dot(xa.astype(BF16), wa_ref[...]))
    gate = _dot(_sigmoid(xg).astype(BF16), wg_ref[...])
    kk_raw = k * kk_ref[...]
    k2 = k * (1.0 + (a_lr - 1.0) * ka_ref[...])
    rkk = r * k2 * rk_ref[...]

    ti = lax.broadcasted_iota(jnp.int32, (L, L), 0)
    tj = lax.broadcasted_iota(jnp.int32, (L, L), 1)
    tril_ones = jnp.where(ti >= tj, 1.0, 0.0).astype(BF16)
    cum = sum(_dot(tril_ones, p) for p in _split_bf16(logd, 3))

    rr = lax.broadcasted_iota(jnp.int32, (PAIR, PAIR), 0)
    cc = lax.broadcasted_iota(jnp.int32, (PAIR, PAIR), 1)
    same_head = (rr // RWKV_HEAD) == (cc // RWKV_HEAD)
    strict = jnp.logical_and(same_head, rr > cc)
    incl = jnp.logical_and(same_head, rr >= cc)
    head_ones = jnp.where(same_head, 1.0, 0.0).astype(BF16)
    eye = jnp.where(rr == cc, 1.0, 0.0).astype(F32)
    first_head = lax.broadcasted_iota(jnp.int32, (L, PAIR), 1) < RWKV_HEAD

    def head_sum(x):
        hi, lo = _split_bf16(x, 2)
        both = _dot(jnp.concatenate([hi, lo], axis=0), head_ones)
        return both[:L] + both[L:]

    def stack(x):
        return jnp.concatenate([jnp.where(first_head, x, 0.0), jnp.where(first_head, 0.0, x)], axis=0)

    n_steps = int(math.log2(L)) - 1
    for g0 in range(0, n_pairs, RWKV_GROUP):
        pairs = range(g0, g0 + RWKV_GROUP)
        sls = [slice(p * PAIR, (p + 1) * PAIR) for p in pairs]
        idx = range(len(sls))

        kk_p = [kk_raw[:, sl] for sl in sls]
        kk_ss = [head_sum(kk_p[i] * kk_p[i]) for i in idx]
        bonus_s = [head_sum(rkk[:, sl]) for sl in sls]
        lhs, rhs, v_st, bk_end, d_end = [], [], [], [], []
        for i in idx:
            sl = sls[i]
            kk_n = kk_p[i] * lax.rsqrt(jnp.maximum(kk_ss[i], 1e-24))
            a_p = -kk_n
            b_p = kk_n * a_lr[:, sl]
            r_p, k_p = r[:, sl], k2[:, sl]
            cu = cum[:, sl]
            c_end = cu[L - 1:L, :]
            e_pos = jnp.exp(cu)
            e_neg = jnp.exp(-cu)
            e_prev = jnp.exp(cu - logd[:, sl])
            e_end = jnp.exp(c_end - cu)
            lhs.append(jnp.concatenate([stack(a_p * e_prev), stack(r_p * e_pos)], axis=0).astype(BF16))
            rhs.append(jnp.concatenate([stack(b_p * e_neg), stack(k_p * e_neg)], axis=0).astype(BF16))
            bk_end.append(jnp.concatenate([stack(b_p * e_end), stack(k_p * e_end)], axis=0).astype(BF16))
            v_st.append(stack(v[:, sl]))
            d_end.append(jnp.exp(c_end))

        pm = [_dot_nt(lhs[i], rhs[i]) for i in idx]
        s_old = [s_ref[p] for p in pairs]
        sh = [_dot_nt(lhs[i], s_old[i].astype(BF16)) for i in idx]
        p_ab = [jnp.where(strict, pm[i][:PAIR, :PAIR], 0.0) for i in idx]
        p_ak = [jnp.where(strict, pm[i][:PAIR, PAIR:], 0.0).astype(BF16) for i in idx]
        p_rbk = [jnp.concatenate([jnp.where(incl, pm[i][PAIR:, :PAIR], 0.0),
                                  jnp.where(incl, pm[i][PAIR:, PAIR:], 0.0)], axis=1).astype(BF16) for i in idx]
        akv = [_dot(p_ak[i], v_st[i].astype(BF16)) for i in idx]

        t_inv = [eye + p_ab[i] for i in idx]
        xb = [p_ab[i].astype(BF16) for i in idx]
        xb = [_dot(xb[i], xb[i]).astype(BF16) for i in idx]
        for it in range(n_steps):
            if it + 1 < n_steps:
                both = [_dot(xb[i], jnp.concatenate([xb[i], t_inv[i].astype(BF16)], axis=1)) for i in idx]
                xb = [both[i][:, :PAIR].astype(BF16) for i in idx]
                t_inv = [t_inv[i] + both[i][:, PAIR:] for i in idx]
            else:
                t_inv = [t_inv[i] + _dot(xb[i], t_inv[i].astype(BF16)) for i in idx]

        u_st = [_dot(t_inv[i].astype(BF16), (sh[i][:PAIR] + akv[i]).astype(BF16)) for i in idx]
        uv = [jnp.concatenate([u_st[i], v_st[i]], axis=0) for i in idx]
        y_st = [sh[i][PAIR:] + _dot(p_rbk[i], uv[i].astype(BF16)) for i in idx]
        s_upd = [_dot(uv[i].T.astype(BF16), bk_end[i]) for i in idx]
        for i, p in enumerate(pairs):
            s_ref[p] = s_old[i] * d_end[i] + s_upd[i]

        y = [y_st[i][:L] + y_st[i][L:] for i in idx]
        mean = [head_sum(y[i]) * (1.0 / RWKV_HEAD) for i in idx]
        yc = [y[i] - mean[i] for i in idx]
        var = [head_sum(yc[i] * yc[i]) * (1.0 / RWKV_HEAD) for i in idx]
        for i in idx:
            sl = sls[i]
            yn = yc[i] * lax.rsqrt(var[i] + RWKV_LNX_EPS) * lw_ref[:, sl] + lb_ref[:, sl]
            o_ref[:, sl] = ((yn + bonus_s[i] * v[:, sl]) * gate[:, sl]).astype(o_ref.dtype)


def _rwkv_time_mix(z, mu, w0, wdec, a0, wa, wg, k_k, k_a, r_k, lnx_w, lnx_b, *, batch, seq):
    L = RWKV_CHUNK
    n_chunks = seq // L
    n_pairs = RWKV_WIDTH // PAIR
    full = lambda shape: pl.BlockSpec(shape, lambda b, c: (0,) * len(shape))
    return pl.pallas_call(
        _rwkv_kernel,
        grid=(batch, n_chunks),
        in_specs=[
            pl.BlockSpec((L, RW_PAD_COLS), lambda b, c: (b * n_chunks + c, 0)),
            full((1, RW_PAD_COLS)),
            full((1, RWKV_WIDTH)), full((LANES, RWKV_WIDTH)),
            full((1, RWKV_WIDTH)), full((LANES, RWKV_WIDTH)),
            full((2 * LANES, RWKV_WIDTH)),
            full((1, RWKV_WIDTH)), full((1, RWKV_WIDTH)), full((1, RWKV_WIDTH)),
            full((1, RWKV_WIDTH)), full((1, RWKV_WIDTH)),
        ],
        out_specs=pl.BlockSpec((L, RWKV_WIDTH), lambda b, c: (b * n_chunks + c, 0)),
        out_shape=jax.ShapeDtypeStruct((batch * seq, RWKV_WIDTH), BF16),
        scratch_shapes=[pltpu.VMEM((n_pairs, PAIR, PAIR), F32), pltpu.VMEM((8, RW_PAD_COLS), F32)],
        compiler_params=_cparams(("arbitrary", "arbitrary")),
        name="rwkv7_time_mix",
    )(z, mu, w0, wdec, a0, wa, wg, k_k, k_a, r_k, lnx_w, lnx_b)


def _attn_kernel(lam_ref, g_ref, q_ref, k_ref, v_ref, o_ref, vt_ref, acc_ref, qq_ref, s_ref, *, seq, n_sub,
                 lambda_init):
    i = pl.program_id(2)
    sb = ATTN_SUB
    two = 2 * sb
    kb = ATTN_KEYS
    steps_per_q = (n_sub * sb) // kb

    @pl.when(i == 0)
    def _():
        def transpose_block(t, carry):
            start = pl.multiple_of(t * kb, kb)
            vt_ref[t] = v_ref[pl.ds(start, kb), :].astype(F32).T.astype(BF16)
            return carry
        lax.fori_loop(0, seq // kb, transpose_block, 0)

    lane = lax.broadcasted_iota(jnp.int32, (sb, DIFF_HEAD), 1)
    for c in range(n_sub):
        qc = q_ref[c * sb:(c + 1) * sb, :]
        zero = jnp.zeros_like(qc)
        qq_ref[c] = jnp.concatenate([jnp.where(lane < DIFF_HALF, qc, zero),
                                     jnp.where(lane < DIFF_HALF, zero, qc)], axis=0)
        acc_ref[c] = jnp.zeros((DIFF_HEAD, two), F32)

    key_idx = lax.broadcasted_iota(jnp.int32, (kb, two), 0)
    qry_idx = lax.broadcasted_iota(jnp.int32, (kb, two), 1) & (sb - 1)

    def scores(c, j):
        start = pl.multiple_of(j * kb, kb)
        return _dot_nt(k_ref[pl.ds(start, kb), :], qq_ref[c])

    def softmax_pv(c, j, st, m, l, mask):
        if mask is not None:
            st = jnp.where(mask, st, NEG_INF)
        m_new = jnp.maximum(m, jnp.max(st, axis=0, keepdims=True))
        alpha = jnp.exp2(m - m_new)
        pr = jnp.exp2(st - m_new)
        l_new = alpha * l + jnp.sum(pr, axis=0, keepdims=True)
        acc_ref[c] = alpha * acc_ref[c] + _dot(vt_ref[j], pr.astype(BF16))
        return m_new, l_new

    for c in range(n_sub):
        s_ref[0, c] = scores(c, 0)

    def two_steps(t, carry):
        ms, ls = list(carry[0]), list(carry[1])
        for half in range(2):
            j = 2 * t + half
            for c in range(n_sub):
                s_ref[1 - half, c] = scores(c, j + 1)
            for c in range(n_sub):
                ms[c], ls[c] = softmax_pv(c, j, s_ref[half, c], ms[c], ls[c], None)
        return tuple(ms), tuple(ls)

    init = (tuple(jnp.full((1, two), NEG_INF, F32) for _ in range(n_sub)),
            tuple(jnp.zeros((1, two), F32) for _ in range(n_sub)))
    ms, ls = lax.fori_loop(0, (i * steps_per_q) // 2, two_steps, init)
    ms, ls = list(ms), list(ls)

    first_diag = i * steps_per_q
    needed = lambda s, c: s * kb < (c + 1) * sb
    tail = {(s, c): scores(c, first_diag + s)
            for s in range(1, steps_per_q) for c in range(n_sub) if needed(s, c)}
    for s in range(steps_per_q):
        for c in range(n_sub):
            if not needed(s, c):
                continue
            st = s_ref[0, c] if s == 0 else tail[(s, c)]
            partial = (s + 1) * kb - 1 > c * sb
            mask = (key_idx + s * kb <= qry_idx + c * sb) if partial else None
            ms[c], ls[c] = softmax_pv(c, first_diag + s, st, ms[c], ls[c], mask)

    lam_p = lam_ref[...]
    lam = (jnp.exp(jnp.sum(lam_p[0:1] * lam_p[1:2], axis=-1, keepdims=True))
           - jnp.exp(jnp.sum(lam_p[2:3] * lam_p[3:4], axis=-1, keepdims=True)) + lambda_init)
    for c in range(n_sub):
        o = acc_ref[c] / ls[c]
        d = o[:, :sb] - lam * o[:, sb:]
        ms_d = jnp.mean(d * d, axis=0, keepdims=True)
        d = d * lax.rsqrt(ms_d + SUBLN_EPS) * g_ref[...] * (1.0 - lambda_init)
        o_ref[c * sb:(c + 1) * sb, :] = d.T.astype(o_ref.dtype)


def _diff_attention(qkv, lam_params, subln_g_col, *, batch, seq, n_sub, lambda_init):
    qb = n_sub * ATTN_SUB
    assert qb % (2 * ATTN_KEYS) == 0 and seq % qb == 0
    nq = seq // qb
    h = DIFF_HEADS
    return pl.pallas_call(
        functools.partial(_attn_kernel, seq=seq, n_sub=n_sub, lambda_init=lambda_init),
        grid=(batch, h, nq),
        in_specs=[
            pl.BlockSpec((8, LANES), lambda b, hh, i: (0, 0)),
            pl.BlockSpec((DIFF_HEAD, 1), lambda b, hh, i: (0, 0)),
            pl.BlockSpec((qb, DIFF_HEAD), lambda b, hh, i: (b * nq + i, hh)),
            pl.BlockSpec((seq, DIFF_HEAD), lambda b, hh, i: (b, h + hh)),
            pl.BlockSpec((seq, DIFF_HEAD), lambda b, hh, i: (b, 2 * h + hh)),
        ],
        out_specs=pl.BlockSpec((qb, DIFF_HEAD), lambda b, hh, i: (b * nq + i, hh)),
        out_shape=jax.ShapeDtypeStruct((batch * seq, DIFF_WIDTH), BF16),
        scratch_shapes=[pltpu.VMEM((seq // ATTN_KEYS, DIFF_HEAD, ATTN_KEYS), BF16),
                        pltpu.VMEM((n_sub, DIFF_HEAD, 2 * ATTN_SUB), F32),
                        pltpu.VMEM((n_sub, 2 * ATTN_SUB, DIFF_HEAD), BF16),
                        pltpu.VMEM((2, n_sub, ATTN_KEYS, 2 * ATTN_SUB), F32)],
        compiler_params=_cparams(("arbitrary", "arbitrary", "arbitrary")),
        name="diff_attention",
    )(lam_params, subln_g_col, qkv, qkv, qkv)


def _outproj2_kernel(ya_ref, yb_ref, x_ref, wa_ref, wb_ref, o_ref):
    o_ref[...] = x_ref[...] + _dot(ya_ref[...], wa_ref[...]) + _dot(yb_ref[...], wb_ref[...])


def _outproj2(ya, yb, x, wa, wb, *, tm):
    m, d = x.shape
    return pl.pallas_call(
        _outproj2_kernel,
        grid=(m // tm,),
        in_specs=[
            pl.BlockSpec((tm, ya.shape[1]), lambda i: (i, 0)),
            pl.BlockSpec((tm, yb.shape[1]), lambda i: (i, 0)),
            pl.BlockSpec((tm, d), lambda i: (i, 0)),
            pl.BlockSpec(wa.shape, lambda i: (0, 0)),
            pl.BlockSpec(wb.shape, lambda i: (0, 0)),
        ],
        out_specs=pl.BlockSpec((tm, d), lambda i: (i, 0)),
        out_shape=jax.ShapeDtypeStruct((m, d), F32),
        compiler_params=_cparams(("arbitrary",)),
        name="even_out_proj",
    )(ya, yb, x, wa, wb)


def _sgu_kernel(u_ref, v_ref, x_ref, lng_ref, lnb_ref, ws_ref, bs_ref, wo_ref, o_ref, vn_ref, gated_ref, *, tm):
    ch = GMLP_CHUNK
    v = v_ref[...].astype(F32)
    mu = jnp.mean(v, axis=-1, keepdims=True)
    vc = v - mu
    var = jnp.mean(vc * vc, axis=-1, keepdims=True)
    vn_ref[...] = (vc * lax.rsqrt(var + LN_EPS) * lng_ref[...] + lnb_ref[...]).astype(BF16)
    ti = lax.broadcasted_iota(jnp.int32, (ch, ch), 0)
    tj = lax.broadcasted_iota(jnp.int32, (ch, ch), 1)
    causal = ti >= tj
    for g in range(GMLP_GROUPS):
        cols = slice(g * LANES, (g + 1) * LANES)
        wg = jnp.where(causal, ws_ref[g], 0.0).astype(BF16)
        bias = bs_ref[:, g:g + 1]
        for c in range(tm // ch):
            rows = slice(c * ch, (c + 1) * ch)
            sv = _dot(wg, vn_ref[rows, cols]) + bias
            gated_ref[rows, cols] = (u_ref[rows, cols].astype(F32) * sv).astype(BF16)
    o_ref[...] = x_ref[...] + _dot(gated_ref[...], wo_ref[...])


def _sgu_outproj(u_v, x, ln_g, ln_b, w_s, b_s_t, w_out, *, tm):
    m, d = x.shape
    n_half = GMLP_WIDTH // d
    return pl.pallas_call(
        functools.partial(_sgu_kernel, tm=tm),
        grid=(m // tm,),
        in_specs=[
            pl.BlockSpec((tm, GMLP_WIDTH), lambda i: (i, 0)),
            pl.BlockSpec((tm, GMLP_WIDTH), lambda i: (i, n_half)),
            pl.BlockSpec((tm, d), lambda i: (i, 0)),
            pl.BlockSpec((1, GMLP_WIDTH), lambda i: (0, 0)),
            pl.BlockSpec((1, GMLP_WIDTH), lambda i: (0, 0)),
            pl.BlockSpec(w_s.shape, lambda i: (0, 0, 0)),
            pl.BlockSpec(b_s_t.shape, lambda i: (0, 0)),
            pl.BlockSpec(w_out.shape, lambda i: (0, 0)),
        ],
        out_specs=pl.BlockSpec((tm, d), lambda i: (i, 0)),
        out_shape=jax.ShapeDtypeStruct((m, d), F32),
        scratch_shapes=[pltpu.VMEM((tm, GMLP_WIDTH), BF16), pltpu.VMEM((tm, GMLP_WIDTH), BF16)],
        compiler_params=_cparams(("arbitrary",)),
        name="sgu_out_proj",
    )(u_v, u_v, x, ln_g, ln_b, w_s, b_s_t, w_out)


def _ffn_kernel(x_ref, g_ref, wg_ref, wu_ref, wd_ref, fg_ref, o_ref, h_ref, acc_ref, *, final):
    j = pl.program_id(1)

    @pl.when(j == 0)
    def _():
        h_ref[...] = _rms_norm_rows(x_ref[...], g_ref[...], NORM_EPS).astype(BF16)
        acc_ref[...] = jnp.zeros_like(acc_ref)

    h = h_ref[...]
    gate = _dot(h, wg_ref[...])
    up = _dot(h, wu_ref[...])
    act = gate * _sigmoid(gate) * up
    acc_ref[...] += _dot(act.astype(BF16), wd_ref[...])

    @pl.when(j == pl.num_programs(1) - 1)
    def _():
        y = x_ref[...] + acc_ref[...]
        if final:
            y = _rms_norm_rows(y, fg_ref[...], NORM_EPS)
        o_ref[...] = y


def _ffn(x, g, w_gate, w_up, w_down, final_g, *, tm, th, final):
    m, d = x.shape
    hid = w_gate.shape[1]
    return pl.pallas_call(
        functools.partial(_ffn_kernel, final=final),
        grid=(m // tm, hid // th),
        in_specs=[
            pl.BlockSpec((tm, d), lambda i, j: (i, 0)),
            pl.BlockSpec((1, d), lambda i, j: (0, 0)),
            pl.BlockSpec((d, th), lambda i, j: (0, j)),
            pl.BlockSpec((d, th), lambda i, j: (0, j)),
            pl.BlockSpec((th, d), lambda i, j: (j, 0)),
            pl.BlockSpec((1, d), lambda i, j: (0, 0)),
        ],
        out_specs=pl.BlockSpec((tm, d), lambda i, j: (i, 0)),
        out_shape=jax.ShapeDtypeStruct((m, d), F32),
        scratch_shapes=[pltpu.VMEM((tm, d), BF16), pltpu.VMEM((tm, d), F32)],
        compiler_params=_cparams(("arbitrary", "arbitrary")),
        name="swiglu_ffn",
    )(x, g, w_gate, w_up, w_down, final_g)


def _pad_cols(a, width):
    return jnp.pad(a, ((0, 0), (0, width - a.shape[1])))


def _pad_rows(a, height):
    return jnp.pad(a, ((0, height - a.shape[0]), (0, 0)))


def _rwkv_col_layout(a):
    c = RWKV_WIDTH
    xw = a[:, 3 * c:3 * c + DECAY_LORA]
    xa = a[:, 3 * c + DECAY_LORA:3 * c + DECAY_LORA + AAA_LORA]
    xg = a[:, 3 * c + DECAY_LORA + AAA_LORA:]
    return jnp.concatenate(
        [a[:, :3 * c], _pad_cols(xw, LANES), _pad_cols(xa, LANES), _pad_cols(xg, 2 * LANES)], axis=1)


def _rope_tables(seq):
    inv = ROPE_THETA ** (-jnp.arange(0, DIFF_HALF, 2, dtype=F32) / DIFF_HALF)
    ang = jnp.arange(seq, dtype=F32)[:, None] * inv[None, :]
    cos, sin = jnp.cos(ang), jnp.sin(ang)
    reps = LANES // DIFF_HALF
    return (jnp.tile(jnp.concatenate([cos, cos], axis=1), (1, reps)),
            jnp.tile(jnp.concatenate([-sin, sin], axis=1), (1, reps)))


def _tile_sizes(m, seq):
    return dict(
        proj_tm=min(1024, seq), proj_tn=512,
        ffn_tm=min(512, m), ffn_th=512,
        out_tm=min(512, m), sgu_tm=min(256, m),
        attn_sub=min(4, seq // ATTN_SUB),
    )


def kernel(x, mix_norm, ffn_norm, ffn_w_gate, ffn_w_up, ffn_w_down, ev_w_in, ev_mu, ev_w0, ev_w_dec_up, ev_a0,
           ev_w_a_up, ev_w_g_up, ev_k_k, ev_k_a, ev_r_k, ev_lnx_w, ev_lnx_b, ev_lam_q1, ev_lam_k1, ev_lam_q2,
           ev_lam_k2, ev_subln_g, ev_w_out, od_w_in, od_ln_g, od_ln_b, od_w_s, od_b_s, od_w_out, final_norm):
    batch, seq, d = x.shape
    m = batch * seq
    ts = _tile_sizes(m, seq)
    cos, sin = _rope_tables(seq)
    row = lambda a: a.reshape(1, -1).astype(F32)
    xf = x.reshape(m, d)
    for i in range(DEPTH):
        j = i // 2
        g_mix = row(mix_norm[i])
        if i % 2 == 0:
            lambda_init = 0.8 - 0.6 * math.exp(-0.3 * i)
            w_in = ev_w_in[j]
            w_rwkv = _rwkv_col_layout(w_in[:, :RWKV_COLS]).astype(BF16)
            w_diff = w_in[:, RWKV_COLS:].astype(BF16)
            z = _norm_matmul(xf, g_mix, w_rwkv, tm=ts["proj_tm"], tn=ts["proj_tn"], out_dtype=BF16)
            qkv = _norm_matmul_rope(xf, g_mix, w_diff, cos, sin, tm=ts["proj_tm"], tn=ts["proj_tn"], seq=seq)
            y_a = _rwkv_time_mix(
                z, _rwkv_col_layout(row(ev_mu[j])), row(ev_w0[j]),
                _pad_rows(ev_w_dec_up[j], LANES).astype(BF16), row(ev_a0[j]),
                _pad_rows(ev_w_a_up[j], LANES).astype(BF16), _pad_rows(ev_w_g_up[j], 2 * LANES).astype(BF16),
                row(ev_k_k[j]), row(ev_k_a[j]), row(ev_r_k[j]), row(ev_lnx_w[j]), row(ev_lnx_b[j]),
                batch=batch, seq=seq)
            lam_params = _pad_rows(_pad_cols(
                jnp.stack([ev_lam_q1[j], ev_lam_k1[j], ev_lam_q2[j], ev_lam_k2[j]]).astype(F32), LANES), 8)
            y_b = _diff_attention(qkv, lam_params, ev_subln_g[j].reshape(-1, 1).astype(F32), batch=batch, seq=seq,
                                  n_sub=ts["attn_sub"], lambda_init=lambda_init)
            w_out = ev_w_out[j].astype(BF16)
            xf = _outproj2(y_a, y_b, xf, w_out[:RWKV_WIDTH], w_out[RWKV_WIDTH:], tm=ts["out_tm"])
        else:
            u_v = _norm_matmul(xf, g_mix, od_w_in[j].astype(BF16), tm=ts["proj_tm"], tn=ts["proj_tn"],
                               out_dtype=BF16, gelu=True)
            xf = _sgu_outproj(u_v, xf, row(od_ln_g[j]), row(od_ln_b[j]), od_w_s[j].astype(F32),
                              od_b_s[j].T.astype(F32), od_w_out[j].astype(BF16), tm=ts["sgu_tm"])
        xf = _ffn(xf, row(ffn_norm[i]), ffn_w_gate[i].astype(BF16), ffn_w_up[i].astype(BF16),
                  ffn_w_down[i].astype(BF16), row(final_norm), tm=ts["ffn_tm"], th=ts["ffn_th"],
                  final=(i == DEPTH - 1))
    return xf.reshape(batch, seq, d)
```

```python
import functools
import math

import jax
import jax.numpy as jnp
from jax import lax
from jax.experimental import pallas as pl
from jax.experimental.pallas import tpu as pltpu

F32 = jnp.float32
BF16 = jnp.bfloat16

D_MODEL = 2048
DEPTH = 4
RWKV_WIDTH = D_MODEL // 2
RWKV_HEAD = 64
DECAY_LORA = 64
AAA_LORA = 64
GATE_LORA = 160
RWKV_COLS = 3 * RWKV_WIDTH + DECAY_LORA + AAA_LORA + GATE_LORA
RWKV_LNX_EPS = 64e-5
DIFF_WIDTH = D_MODEL - RWKV_WIDTH
DIFF_HEAD = 128
DIFF_HALF = DIFF_HEAD // 2
DIFF_HEADS = DIFF_WIDTH // DIFF_HEAD
ROPE_THETA = 10000.0
NEG_INF = -1e30
GMLP_WIDTH = D_MODEL
GMLP_CHUNK = 128
GMLP_GROUPS = 16
FFN_HIDDEN = 5632
NORM_EPS = 1e-6
SUBLN_EPS = 1e-5
LN_EPS = 1e-5

LANES = 128
MXU_COLS = 256
VMEM_LIMIT_BYTES = 56 * 1024 * 1024

RW_XW = 3 * RWKV_WIDTH
RW_XA = RW_XW + LANES
RW_XG = RW_XA + LANES
RW_PAD_COLS = RW_XG + 2 * LANES

RWKV_CHUNK = 64
ATTN_SUB = 128
ATTN_KEYS = 256
ATTN_TRIP = 4
LOG2_E = 1.4426950408889634
PAIR = 2 * RWKV_HEAD


def _cparams(semantics):
    return pltpu.CompilerParams(dimension_semantics=semantics, vmem_limit_bytes=VMEM_LIMIT_BYTES)


def _dot(a, b):
    return jnp.dot(a, b, preferred_element_type=F32)


def _dot_nt(a, b):
    return lax.dot_general(a, b, (((1,), (1,)), ((), ())), preferred_element_type=F32)


def _rms_norm_rows(x, g, eps):
    ms = jnp.mean(x * x, axis=-1, keepdims=True)
    return x * lax.rsqrt(ms + eps) * g


def _norm_matmul_kernel(x_ref, g_ref, w_ref, o_ref, h_ref, *, tn, gelu):
    @pl.when(pl.program_id(1) == 0)
    def _():
        h_ref[...] = _rms_norm_rows(x_ref[...], g_ref[...], NORM_EPS).astype(BF16)

    for n0 in range(0, tn, MXU_COLS):
        acc = _dot(h_ref[...], w_ref[:, n0:n0 + MXU_COLS])
        if gelu:
            acc = 0.5 * acc * (1.0 + lax.erf(acc * math.sqrt(0.5)))
        o_ref[:, n0:n0 + MXU_COLS] = acc.astype(o_ref.dtype)


def _norm_matmul(x, g, w, *, tm, tn, out_dtype, gelu=False):
    m, k = x.shape
    n = w.shape[1]
    return pl.pallas_call(
        functools.partial(_norm_matmul_kernel, tn=tn, gelu=gelu),
        grid=(m // tm, n // tn),
        in_specs=[
            pl.BlockSpec((tm, k), lambda i, j: (i, 0)),
            pl.BlockSpec((1, k), lambda i, j: (0, 0)),
            pl.BlockSpec((k, tn), lambda i, j: (0, j)),
        ],
        out_specs=pl.BlockSpec((tm, tn), lambda i, j: (i, j)),
        out_shape=jax.ShapeDtypeStruct((m, n), out_dtype),
        scratch_shapes=[pltpu.VMEM((tm, k), BF16)],
        compiler_params=_cparams(("arbitrary", "arbitrary")),
        name="norm_matmul_gelu" if gelu else "norm_matmul",
    )(x, g, w)


def _norm_matmul_rope_kernel(x_ref, g_ref, w_ref, cos_ref, sin_ref, o_ref, h_ref, *, tn, q_tiles, rope_tiles):
    j = pl.program_id(1)

    @pl.when(j == 0)
    def _():
        h_ref[...] = _rms_norm_rows(x_ref[...], g_ref[...], NORM_EPS).astype(BF16)

    def chunks(epilogue):
        for n0 in range(0, tn, MXU_COLS):
            acc = _dot(h_ref[...], w_ref[:, n0:n0 + MXU_COLS])
            o_ref[:, n0:n0 + MXU_COLS] = epilogue(acc).astype(o_ref.dtype)

    def rope(acc):
        reps = MXU_COLS // LANES
        c = jnp.concatenate([cos_ref[...]] * reps, axis=1)
        s = jnp.concatenate([sin_ref[...]] * reps, axis=1)
        lane = lax.broadcasted_iota(jnp.int32, acc.shape, 1)
        partner = jnp.where((lane & (DIFF_HALF // 2)) == 0,
                            pltpu.roll(acc, MXU_COLS - DIFF_HALF // 2, 1),
                            pltpu.roll(acc, DIFF_HALF // 2, 1))
        scale = jnp.where(j < q_tiles, DIFF_HALF ** -0.5 * LOG2_E, 1.0).astype(F32)
        return (acc * c + partner * s) * scale

    @pl.when(j < rope_tiles)
    def _():
        chunks(rope)

    @pl.when(j >= rope_tiles)
    def _():
        chunks(lambda acc: acc)


def _norm_matmul_rope(x, g, w, cos, sin, *, tm, tn, seq):
    m, k = x.shape
    n = w.shape[1]
    q_tiles = DIFF_WIDTH // tn
    t_tiles = seq // tm
    return pl.pallas_call(
        functools.partial(_norm_matmul_rope_kernel, tn=tn, q_tiles=q_tiles, rope_tiles=2 * q_tiles),
        grid=(m // tm, n // tn),
        in_specs=[
            pl.BlockSpec((tm, k), lambda i, j: (i, 0)),
            pl.BlockSpec((1, k), lambda i, j: (0, 0)),
            pl.BlockSpec((k, tn), lambda i, j: (0, j)),
            pl.BlockSpec((tm, LANES), lambda i, j: (i % t_tiles, 0)),
            pl.BlockSpec((tm, LANES), lambda i, j: (i % t_tiles, 0)),
        ],
        out_specs=pl.BlockSpec((tm, tn), lambda i, j: (i, j)),
        out_shape=jax.ShapeDtypeStruct((m, n), BF16),
        scratch_shapes=[pltpu.VMEM((tm, k), BF16)],
        compiler_params=_cparams(("arbitrary", "arbitrary")),
        name="norm_matmul_rope",
    )(x, g, w, cos, sin)


def _sigmoid(y):
    return 1.0 / (1.0 + jnp.exp(-y))


def _split_bf16(x, parts):
    out = []
    rem = x
    for _ in range(parts):
        p = rem.astype(BF16)
        out.append(p)
        rem = rem - p.astype(F32)
    return out


def _rwkv_kernel(z_ref, mu_ref, w0_ref, wdec_ref, a0_ref, wa_ref, wg_ref, kk_ref, ka_ref, rk_ref,
                 lw_ref, lb_ref, o_ref, s_ref, prev_ref):
    L = RWKV_CHUNK
    C = RWKV_WIDTH
    n_pairs = C // PAIR
    n_batch = z_ref.shape[0]

    @pl.when(pl.program_id(0) == 0)
    def _():
        s_ref[...] = jnp.zeros_like(s_ref)
        prev_ref[...] = jnp.zeros_like(prev_ref)

    ti = lax.broadcasted_iota(jnp.int32, (L, L), 0)
    tj = lax.broadcasted_iota(jnp.int32, (L, L), 1)
    tril_ones = jnp.where(ti >= tj, 1.0, 0.0).astype(BF16)

    def prepare(bi):
        z = z_ref[bi].astype(F32)
        row = lax.broadcasted_iota(jnp.int32, z.shape, 0)
        z_prev = jnp.where(row == 0, prev_ref[bi, 0:1, :], pltpu.roll(z, 1, 0))
        prev_ref[bi, 0:1, :] = z[L - 1:L, :]
        zs = z + (z_prev - z) * mu_ref[...]

        r = zs[:, 0:C]
        k = zs[:, C:2 * C]
        v = zs[:, 2 * C:3 * C]
        xw = zs[:, RW_XW:RW_XA]
        xa = zs[:, RW_XA:RW_XG]
        xg = zs[:, RW_XG:RW_PAD_COLS]

        y_dec = w0_ref[...] + _dot(jnp.tanh(xw).astype(BF16), wdec_ref[...])
        logd = -math.exp(-0.5) * _sigmoid(y_dec)
        a_lr = _sigmoid(a0_ref[...] + _dot(xa.astype(BF16), wa_ref[...]))
        gate = _dot(_sigmoid(xg).astype(BF16), wg_ref[...])
        kk_raw = k * kk_ref[...]
        k2 = k * (1.0 + (a_lr - 1.0) * ka_ref[...])
        rkk = r * k2 * rk_ref[...]
        cum = sum(_dot(tril_ones, p) for p in _split_bf16(logd, 3))
        return r, k2, v, logd, a_lr, gate, kk_raw, rkk, cum

    rr = lax.broadcasted_iota(jnp.int32, (PAIR, PAIR), 0)
    cc = lax.broadcasted_iota(jnp.int32, (PAIR, PAIR), 1)
    same_head = (rr // RWKV_HEAD) == (cc // RWKV_HEAD)
    strict = jnp.logical_and(same_head, rr > cc)
    incl = jnp.logical_and(same_head, rr >= cc)
    head_ones = jnp.where(same_head, 1.0, 0.0).astype(BF16)
    eye = jnp.where(rr == cc, 1.0, 0.0).astype(F32)
    first_head = lax.broadcasted_iota(jnp.int32, (L, PAIR), 1) < RWKV_HEAD

    def head_sum(x):
        return _dot(x.astype(BF16), head_ones)

    def stack(x):
        return jnp.concatenate([jnp.where(first_head, x, 0.0), jnp.where(first_head, 0.0, x)], axis=0)

    n_steps = int(math.log2(L)) - 1
    idx = range(n_pairs)
    sls = [slice(p * PAIR, (p + 1) * PAIR) for p in idx]

    def elementwise(prep, p):
        r, k2, v, logd, a_lr, _, kk_raw, rkk, cum = prep
        sl = sls[p]
        kk_p = kk_raw[:, sl]
        kk_n = kk_p * lax.rsqrt(jnp.maximum(head_sum(kk_p * kk_p), 1e-24))
        bonus = head_sum(rkk[:, sl])
        a_p = -kk_n
        b_p = kk_n * a_lr[:, sl]
        r_p, k_p = r[:, sl], k2[:, sl]
        cu = cum[:, sl]
        c_end = cu[L - 1:L, :]
        e_pos = jnp.exp(cu)
        e_neg = jnp.exp(-cu)
        e_prev = jnp.exp(cu - logd[:, sl])
        e_end = jnp.exp(c_end - cu)
        lhs = jnp.concatenate([stack(a_p * e_prev), stack(r_p * e_pos)], axis=0).astype(BF16)
        rhs = jnp.concatenate([stack(b_p * e_neg), stack(k_p * e_neg)], axis=0).astype(BF16)
        bk_end = jnp.concatenate([stack(b_p * e_end), stack(k_p * e_end)], axis=0).astype(BF16)
        return lhs, rhs, bk_end, stack(v[:, sl]), jnp.exp(c_end), bonus

    def stages(bi, prep, elems, filler):
        v, gate = prep[2], prep[5]
        lhs, rhs, bk_end, v_st, d_end, bonus_s = zip(*elems)
        pm = [_dot_nt(lhs[i], rhs[i]) for i in idx]
        filler()
        s_old = [s_ref[bi, p] for p in idx]
        sh = [_dot_nt(lhs[i], s_old[i].astype(BF16)) for i in idx]
        filler()
        p_ab = [jnp.where(strict, pm[i][:PAIR, :PAIR], 0.0) for i in idx]
        p_ak = [jnp.where(strict, pm[i][:PAIR, PAIR:], 0.0).astype(BF16) for i in idx]
        p_rbk = [jnp.concatenate([jnp.where(incl, pm[i][PAIR:, :PAIR], 0.0),
                                  jnp.where(incl, pm[i][PAIR:, PAIR:], 0.0)], axis=1).astype(BF16) for i in idx]
        akv = [_dot(p_ak[i], v_st[i].astype(BF16)) for i in idx]
        filler()

        t_inv = [eye + p_ab[i] for i in idx]
        xb = [p_ab[i].astype(BF16) for i in idx]
        xb = [_dot(xb[i], xb[i]).astype(BF16) for i in idx]
        filler()
        for it in range(n_steps):
            if it + 1 < n_steps:
                both = [_dot(xb[i], jnp.concatenate([xb[i], t_inv[i].astype(BF16)], axis=1)) for i in idx]
                xb = [both[i][:, :PAIR].astype(BF16) for i in idx]
                t_inv = [t_inv[i] + both[i][:, PAIR:] for i in idx]
            else:
                t_inv = [t_inv[i] + _dot(xb[i], t_inv[i].astype(BF16)) for i in idx]
            filler()

        u_st = [_dot(t_inv[i].astype(BF16), (sh[i][:PAIR] + akv[i]).astype(BF16)) for i in idx]
        filler()
        uv = [jnp.concatenate([u_st[i], v_st[i]], axis=0) for i in idx]
        y_st = [sh[i][PAIR:] + _dot(p_rbk[i], uv[i].astype(BF16)) for i in idx]
        filler()
        s_upd = [_dot(uv[i].T.astype(BF16), bk_end[i]) for i in idx]
        for p in idx:
            s_ref[bi, p] = s_old[p] * d_end[p] + s_upd[p]

        y = [y_st[i][:L] + y_st[i][L:] for i in idx]
        mean = [head_sum(y[i]) * (1.0 / RWKV_HEAD) for i in idx]
        yc = [y[i] - mean[i] for i in idx]
        var = [head_sum(yc[i] * yc[i]) * (1.0 / RWKV_HEAD) for i in idx]
        for i in idx:
            sl = sls[i]
            yn = yc[i] * lax.rsqrt(var[i] + RWKV_LNX_EPS) * lw_ref[:, sl] + lb_ref[:, sl]
            o_ref[bi, :, sl] = ((yn + bonus_s[i] * v[:, sl]) * gate[:, sl]).astype(o_ref.dtype)

    prepared = [prepare(bi) for bi in range(n_batch)]
    elems = [elementwise(prepared[0], p) for p in idx]
    for bi in range(n_batch):
        nxt_elems, queue = [], []
        if bi + 1 < n_batch:
            queue = [functools.partial(lambda row, p: nxt_elems.append(elementwise(prepared[row], p)), bi + 1, p)
                     for p in idx]

        def filler():
            if queue:
                queue.pop(0)()

        stages(bi, prepared[bi], elems, filler)
        while queue:
            queue.pop(0)()
        elems = nxt_elems


def _rwkv_time_mix(z, mu, w0, wdec, a0, wa, wg, k_k, k_a, r_k, lnx_w, lnx_b, *, batch, seq):
    L = RWKV_CHUNK
    n_chunks = seq // L
    n_pairs = RWKV_WIDTH // PAIR
    full = lambda shape: pl.BlockSpec(shape, lambda c: (0,) * len(shape))
    out = pl.pallas_call(
        _rwkv_kernel,
        grid=(n_chunks,),
        in_specs=[
            pl.BlockSpec((batch, L, RW_PAD_COLS), lambda c: (0, c, 0)),
            full((1, RW_PAD_COLS)),
            full((1, RWKV_WIDTH)), full((LANES, RWKV_WIDTH)),
            full((1, RWKV_WIDTH)), full((LANES, RWKV_WIDTH)),
            full((2 * LANES, RWKV_WIDTH)),
            full((1, RWKV_WIDTH)), full((1, RWKV_WIDTH)), full((1, RWKV_WIDTH)),
            full((1, RWKV_WIDTH)), full((1, RWKV_WIDTH)),
        ],
        out_specs=pl.BlockSpec((batch, L, RWKV_WIDTH), lambda c: (0, c, 0)),
        out_shape=jax.ShapeDtypeStruct((batch, seq, RWKV_WIDTH), BF16),
        scratch_shapes=[pltpu.VMEM((batch, n_pairs, PAIR, PAIR), F32), pltpu.VMEM((batch, 8, RW_PAD_COLS), F32)],
        compiler_params=_cparams(("arbitrary",)),
        name="rwkv7_time_mix",
    )(z.reshape(batch, seq, RW_PAD_COLS), mu, w0, wdec, a0, wa, wg, k_k, k_a, r_k, lnx_w, lnx_b)
    return out.reshape(batch * seq, RWKV_WIDTH)


def _attn_kernel(lam_ref, g_ref, q_ref, k_ref, v_ref, o_ref, vt_ref, acc_ref, qq_ref, s_ref, *, seq, n_sub,
                 lambda_init):
    i = pl.program_id(2)
    sb = ATTN_SUB
    two = 2 * sb
    kb = ATTN_KEYS
    steps_per_q = (n_sub * sb) // kb

    @pl.when(i == 0)
    def _():
        def transpose_block(t, carry):
            start = pl.multiple_of(t * kb, kb)
            vt_ref[t] = v_ref[pl.ds(start, kb), :].astype(F32).T.astype(BF16)
            return carry
        lax.fori_loop(0, seq // kb, transpose_block, 0)

    lane = lax.broadcasted_iota(jnp.int32, (sb, DIFF_HEAD), 1)
    for c in range(n_sub):
        qc = q_ref[c * sb:(c + 1) * sb, :]
        zero = jnp.zeros_like(qc)
        qq_ref[c] = jnp.concatenate([jnp.where(lane < DIFF_HALF, qc, zero),
                                     jnp.where(lane < DIFF_HALF, zero, qc)], axis=0)
        acc_ref[c] = jnp.zeros((DIFF_HEAD, two), F32)

    key_idx = lax.broadcasted_iota(jnp.int32, (kb, two), 0)
    qry_idx = lax.broadcasted_iota(jnp.int32, (kb, two), 1) & (sb - 1)

    def scores(c, j):
        start = pl.multiple_of(j * kb, kb)
        return _dot_nt(k_ref[pl.ds(start, kb), :], qq_ref[c])

    def softmax_pv(c, j, st, m, l, mask):
        if mask is not None:
            st = jnp.where(mask, st, NEG_INF)
        m_new = jnp.maximum(m, jnp.max(st, axis=0, keepdims=True))
        alpha = jnp.exp2(m - m_new)
        pr = jnp.exp2(st - m_new)
        l_new = alpha * l + jnp.sum(pr, axis=0, keepdims=True)
        acc_ref[c] = alpha * acc_ref[c] + _dot(vt_ref[j], pr.astype(BF16))
        return m_new, l_new

    for c in range(n_sub):
        s_ref[0, c] = scores(c, 0)

    def full_steps(j0, n_steps, carry):
        ms, ls = list(carry[0]), list(carry[1])
        for h in range(n_steps):
            for c in range(n_sub):
                s_ref[1 - h % 2, c] = scores(c, j0 + h + 1)
            for c in range(n_sub):
                ms[c], ls[c] = softmax_pv(c, j0 + h, s_ref[h % 2, c], ms[c], ls[c], None)
        return tuple(ms), tuple(ls)

    init = (tuple(jnp.full((1, two), NEG_INF, F32) for _ in range(n_sub)),
            tuple(jnp.zeros((1, two), F32) for _ in range(n_sub)))
    n_full = i * steps_per_q
    n_short = (n_full % ATTN_TRIP) // 2
    carry = lax.fori_loop(0, n_short, lambda t, cr: full_steps(2 * t, 2, cr), init)
    ms, ls = lax.fori_loop(0, n_full // ATTN_TRIP,
                           lambda t, cr: full_steps(2 * n_short + ATTN_TRIP * t, ATTN_TRIP, cr), carry)
    ms, ls = list(ms), list(ls)

    first_diag = i * steps_per_q
    needed = lambda s, c: s * kb < (c + 1) * sb
    tail = {(s, c): scores(c, first_diag + s)
            for s in range(1, steps_per_q) for c in range(n_sub) if needed(s, c)}
    for s in range(steps_per_q):
        for c in range(n_sub):
            if not needed(s, c):
                continue
            st = s_ref[0, c] if s == 0 else tail[(s, c)]
            partial = (s + 1) * kb - 1 > c * sb
            mask = (key_idx + s * kb <= qry_idx + c * sb) if partial else None
            ms[c], ls[c] = softmax_pv(c, first_diag + s, st, ms[c], ls[c], mask)

    lam_p = lam_ref[...]
    lam = (jnp.exp(jnp.sum(lam_p[0:1] * lam_p[1:2], axis=-1, keepdims=True))
           - jnp.exp(jnp.sum(lam_p[2:3] * lam_p[3:4], axis=-1, keepdims=True)) + lambda_init)
    for c in range(n_sub):
        o = acc_ref[c] / ls[c]
        d = o[:, :sb] - lam * o[:, sb:]
        ms_d = jnp.mean(d * d, axis=0, keepdims=True)
        d = d * lax.rsqrt(ms_d + SUBLN_EPS) * g_ref[...] * (1.0 - lambda_init)
        o_ref[c * sb:(c + 1) * sb, :] = d.T.astype(o_ref.dtype)


def _diff_attention(qkv, lam_params, subln_g_col, *, batch, seq, n_sub, lambda_init):
    qb = n_sub * ATTN_SUB
    assert qb % (2 * ATTN_KEYS) == 0 and seq % qb == 0
    nq = seq // qb
    h = DIFF_HEADS
    return pl.pallas_call(
        functools.partial(_attn_kernel, seq=seq, n_sub=n_sub, lambda_init=lambda_init),
        grid=(batch, h, nq),
        in_specs=[
            pl.BlockSpec((8, LANES), lambda b, hh, i: (0, 0)),
            pl.BlockSpec((DIFF_HEAD, 1), lambda b, hh, i: (0, 0)),
            pl.BlockSpec((qb, DIFF_HEAD), lambda b, hh, i: (b * nq + i, hh)),
            pl.BlockSpec((seq, DIFF_HEAD), lambda b, hh, i: (b, h + hh)),
            pl.BlockSpec((seq, DIFF_HEAD), lambda b, hh, i: (b, 2 * h + hh)),
        ],
        out_specs=pl.BlockSpec((qb, DIFF_HEAD), lambda b, hh, i: (b * nq + i, hh)),
        out_shape=jax.ShapeDtypeStruct((batch * seq, DIFF_WIDTH), BF16),
        scratch_shapes=[pltpu.VMEM((seq // ATTN_KEYS, DIFF_HEAD, ATTN_KEYS), BF16),
                        pltpu.VMEM((n_sub, DIFF_HEAD, 2 * ATTN_SUB), F32),
                        pltpu.VMEM((n_sub, 2 * ATTN_SUB, DIFF_HEAD), BF16),
                        pltpu.VMEM((2, n_sub, ATTN_KEYS, 2 * ATTN_SUB), F32)],
        compiler_params=_cparams(("arbitrary", "arbitrary", "arbitrary")),
        name="diff_attention",
    )(lam_params, subln_g_col, qkv, qkv, qkv)


def _outproj2_kernel(ya_ref, yb_ref, x_ref, wa_ref, wb_ref, o_ref):
    o_ref[...] = x_ref[...] + _dot(ya_ref[...], wa_ref[...]) + _dot(yb_ref[...], wb_ref[...])


def _outproj2(ya, yb, x, wa, wb, *, tm):
    m, d = x.shape
    return pl.pallas_call(
        _outproj2_kernel,
        grid=(m // tm,),
        in_specs=[
            pl.BlockSpec((tm, ya.shape[1]), lambda i: (i, 0)),
            pl.BlockSpec((tm, yb.shape[1]), lambda i: (i, 0)),
            pl.BlockSpec((tm, d), lambda i: (i, 0)),
            pl.BlockSpec(wa.shape, lambda i: (0, 0)),
            pl.BlockSpec(wb.shape, lambda i: (0, 0)),
        ],
        out_specs=pl.BlockSpec((tm, d), lambda i: (i, 0)),
        out_shape=jax.ShapeDtypeStruct((m, d), F32),
        compiler_params=_cparams(("arbitrary",)),
        name="even_out_proj",
    )(ya, yb, x, wa, wb)


def _sgu_kernel(u_ref, v_ref, x_ref, lng_ref, lnb_ref, ws_ref, bs_ref, wo_ref, o_ref, vn_ref, gated_ref, *, tm):
    ch = GMLP_CHUNK
    v = v_ref[...].astype(F32)
    mu = jnp.mean(v, axis=-1, keepdims=True)
    vc = v - mu
    var = jnp.mean(vc * vc, axis=-1, keepdims=True)
    vn_ref[...] = (vc * lax.rsqrt(var + LN_EPS) * lng_ref[...] + lnb_ref[...]).astype(BF16)
    ti = lax.broadcasted_iota(jnp.int32, (ch, ch), 0)
    tj = lax.broadcasted_iota(jnp.int32, (ch, ch), 1)
    causal = ti >= tj
    for g in range(GMLP_GROUPS):
        cols = slice(g * LANES, (g + 1) * LANES)
        wg = jnp.where(causal, ws_ref[g], 0.0).astype(BF16)
        bias = bs_ref[:, g:g + 1]
        for c in range(tm // ch):
            rows = slice(c * ch, (c + 1) * ch)
            sv = _dot(wg, vn_ref[rows, cols]) + bias
            gated_ref[rows, cols] = (u_ref[rows, cols].astype(F32) * sv).astype(BF16)
    o_ref[...] = x_ref[...] + _dot(gated_ref[...], wo_ref[...])


def _sgu_outproj(u_v, x, ln_g, ln_b, w_s, b_s_t, w_out, *, tm):
    m, d = x.shape
    n_half = GMLP_WIDTH // d
    return pl.pallas_call(
        functools.partial(_sgu_kernel, tm=tm),
        grid=(m // tm,),
        in_specs=[
            pl.BlockSpec((tm, GMLP_WIDTH), lambda i: (i, 0)),
            pl.BlockSpec((tm, GMLP_WIDTH), lambda i: (i, n_half)),
            pl.BlockSpec((tm, d), lambda i: (i, 0)),
            pl.BlockSpec((1, GMLP_WIDTH), lambda i: (0, 0)),
            pl.BlockSpec((1, GMLP_WIDTH), lambda i: (0, 0)),
            pl.BlockSpec(w_s.shape, lambda i: (0, 0, 0)),
            pl.BlockSpec(b_s_t.shape, lambda i: (0, 0)),
            pl.BlockSpec(w_out.shape, lambda i: (0, 0)),
        ],
        out_specs=pl.BlockSpec((tm, d), lambda i: (i, 0)),
        out_shape=jax.ShapeDtypeStruct((m, d), F32),
        scratch_shapes=[pltpu.VMEM((tm, GMLP_WIDTH), BF16), pltpu.VMEM((tm, GMLP_WIDTH), BF16)],
        compiler_params=_cparams(("arbitrary",)),
        name="sgu_out_proj",
    )(u_v, u_v, x, ln_g, ln_b, w_s, b_s_t, w_out)


def _ffn_kernel(x_ref, g_ref, wg_ref, wu_ref, wd_ref, fg_ref, o_ref, h_ref, acc_ref, *, final):
    j = pl.program_id(1)

    @pl.when(j == 0)
    def _():
        h_ref[...] = _rms_norm_rows(x_ref[...], g_ref[...], NORM_EPS).astype(BF16)
        acc_ref[...] = jnp.zeros_like(acc_ref)

    h = h_ref[...]
    gate = _dot(h, wg_ref[...])
    up = _dot(h, wu_ref[...])
    act = gate * _sigmoid(gate) * up
    acc_ref[...] += _dot(act.astype(BF16), wd_ref[...])

    @pl.when(j == pl.num_programs(1) - 1)
    def _():
        y = x_ref[...] + acc_ref[...]
        if final:
            y = _rms_norm_rows(y, fg_ref[...], NORM_EPS)
        o_ref[...] = y


def _ffn(x, g, w_gate, w_up, w_down, final_g, *, tm, th, final):
    m, d = x.shape
    hid = w_gate.shape[1]
    return pl.pallas_call(
        functools.partial(_ffn_kernel, final=final),
        grid=(m // tm, hid // th),
        in_specs=[
            pl.BlockSpec((tm, d), lambda i, j: (i, 0)),
            pl.BlockSpec((1, d), lambda i, j: (0, 0)),
            pl.BlockSpec((d, th), lambda i, j: (0, j)),
            pl.BlockSpec((d, th), lambda i, j: (0, j)),
            pl.BlockSpec((th, d), lambda i, j: (j, 0)),
            pl.BlockSpec((1, d), lambda i, j: (0, 0)),
        ],
        out_specs=pl.BlockSpec((tm, d), lambda i, j: (i, 0)),
        out_shape=jax.ShapeDtypeStruct((m, d), F32),
        scratch_shapes=[pltpu.VMEM((tm, d), BF16), pltpu.VMEM((tm, d), F32)],
        compiler_params=_cparams(("arbitrary", "arbitrary")),
        name="swiglu_ffn",
    )(x, g, w_gate, w_up, w_down, final_g)


def _pad_cols(a, width):
    return jnp.pad(a, ((0, 0), (0, width - a.shape[1])))


def _pad_rows(a, height):
    return jnp.pad(a, ((0, height - a.shape[0]), (0, 0)))


def _rwkv_col_layout(a):
    c = RWKV_WIDTH
    xw = a[:, 3 * c:3 * c + DECAY_LORA]
    xa = a[:, 3 * c + DECAY_LORA:3 * c + DECAY_LORA + AAA_LORA]
    xg = a[:, 3 * c + DECAY_LORA + AAA_LORA:]
    return jnp.concatenate(
        [a[:, :3 * c], _pad_cols(xw, LANES), _pad_cols(xa, LANES), _pad_cols(xg, 2 * LANES)], axis=1)


def _rope_tables(seq):
    inv = ROPE_THETA ** (-jnp.arange(0, DIFF_HALF, 2, dtype=F32) / DIFF_HALF)
    ang = jnp.arange(seq, dtype=F32)[:, None] * inv[None, :]
    cos, sin = jnp.cos(ang), jnp.sin(ang)
    reps = LANES // DIFF_HALF
    return (jnp.tile(jnp.concatenate([cos, cos], axis=1), (1, reps)),
            jnp.tile(jnp.concatenate([-sin, sin], axis=1), (1, reps)))


def _tile_sizes(m, seq):
    return dict(
        proj_tm=min(1024, seq), proj_tn=1024, rwkv_proj_tn=512,
        ffn_tm=min(512, m), ffn_th=512,
        out_tm=min(512, m), sgu_tm=min(256, m),
        attn_sub=min(4, seq // ATTN_SUB),
    )


def kernel(x, mix_norm, ffn_norm, ffn_w_gate, ffn_w_up, ffn_w_down, ev_w_in, ev_mu, ev_w0, ev_w_dec_up, ev_a0,
           ev_w_a_up, ev_w_g_up, ev_k_k, ev_k_a, ev_r_k, ev_lnx_w, ev_lnx_b, ev_lam_q1, ev_lam_k1, ev_lam_q2,
           ev_lam_k2, ev_subln_g, ev_w_out, od_w_in, od_ln_g, od_ln_b, od_w_s, od_b_s, od_w_out, final_norm):
    batch, seq, d = x.shape
    m = batch * seq
    ts = _tile_sizes(m, seq)
    cos, sin = _rope_tables(seq)
    row = lambda a: a.reshape(1, -1).astype(F32)
    xf = x.reshape(m, d)
    for i in range(DEPTH):
        j = i // 2
        g_mix = row(mix_norm[i])
        if i % 2 == 0:
            lambda_init = 0.8 - 0.6 * math.exp(-0.3 * i)
            w_in = ev_w_in[j]
            w_rwkv = _rwkv_col_layout(w_in[:, :RWKV_COLS]).astype(BF16)
            w_diff = w_in[:, RWKV_COLS:].astype(BF16)
            z = _norm_matmul(xf, g_mix, w_rwkv, tm=ts["proj_tm"], tn=ts["rwkv_proj_tn"], out_dtype=BF16)
            qkv = _norm_matmul_rope(xf, g_mix, w_diff, cos, sin, tm=ts["proj_tm"], tn=ts["proj_tn"], seq=seq)
            y_a = _rwkv_time_mix(
                z, _rwkv_col_layout(row(ev_mu[j])), row(ev_w0[j]),
                _pad_rows(ev_w_dec_up[j], LANES).astype(BF16), row(ev_a0[j]),
                _pad_rows(ev_w_a_up[j], LANES).astype(BF16), _pad_rows(ev_w_g_up[j], 2 * LANES).astype(BF16),
                row(ev_k_k[j]), row(ev_k_a[j]), row(ev_r_k[j]), row(ev_lnx_w[j]), row(ev_lnx_b[j]),
                batch=batch, seq=seq)
            lam_params = _pad_rows(_pad_cols(
                jnp.stack([ev_lam_q1[j], ev_lam_k1[j], ev_lam_q2[j], ev_lam_k2[j]]).astype(F32), LANES), 8)
            y_b = _diff_attention(qkv, lam_params, ev_subln_g[j].reshape(-1, 1).astype(F32), batch=batch, seq=seq,
                                  n_sub=ts["attn_sub"], lambda_init=lambda_init)
            w_out = ev_w_out[j].astype(BF16)
            xf = _outproj2(y_a, y_b, xf, w_out[:RWKV_WIDTH], w_out[RWKV_WIDTH:], tm=ts["out_tm"])
        else:
            u_v = _norm_matmul(xf, g_mix, od_w_in[j].astype(BF16), tm=ts["proj_tm"], tn=ts["proj_tn"],
                               out_dtype=BF16, gelu=True)
            xf = _sgu_outproj(u_v, xf, row(od_ln_g[j]), row(od_ln_b[j]), od_w_s[j].astype(F32),
                              od_b_s[j].T.astype(F32), od_w_out[j].astype(BF16), tm=ts["sgu_tm"])
        xf = _ffn(xf, row(ffn_norm[i]), ffn_w_gate[i].astype(BF16), ffn_w_up[i].astype(BF16),
                  ffn_w_down[i].astype(BF16), row(final_norm), tm=ts["ffn_tm"], th=ts["ffn_th"],
                  final=(i == DEPTH - 1))
    return xf.reshape(batch, seq, d)
```

```python
import functools
import math

import jax
import jax.numpy as jnp
from jax import lax
from jax.experimental import pallas as pl
from jax.experimental.pallas import tpu as pltpu

F32 = jnp.float32
BF16 = jnp.bfloat16

D_MODEL = 2048
DEPTH = 4
RWKV_WIDTH = D_MODEL // 2
RWKV_HEAD = 64
DECAY_LORA = 64
AAA_LORA = 64
GATE_LORA = 160
RWKV_COLS = 3 * RWKV_WIDTH + DECAY_LORA + AAA_LORA + GATE_LORA
RWKV_LNX_EPS = 64e-5
DIFF_WIDTH = D_MODEL - RWKV_WIDTH
DIFF_HEAD = 128
DIFF_HALF = DIFF_HEAD // 2
DIFF_HEADS = DIFF_WIDTH // DIFF_HEAD
ROPE_THETA = 10000.0
NEG_INF = -1e30
GMLP_WIDTH = D_MODEL
GMLP_CHUNK = 128
GMLP_GROUPS = 16
FFN_HIDDEN = 5632
NORM_EPS = 1e-6
SUBLN_EPS = 1e-5
LN_EPS = 1e-5

LANES = 128
MXU_COLS = 256
VMEM_LIMIT_BYTES = 56 * 1024 * 1024

RW_XW = 3 * RWKV_WIDTH
RW_XA = RW_XW + LANES
RW_XG = RW_XA + LANES
RW_PAD_COLS = RW_XG + 2 * LANES

RWKV_CHUNK = 64
ATTN_SUB = 128
ATTN_KEYS = 256
ATTN_TRIP = 8
LOG2_E = 1.4426950408889634
PAIR = 2 * RWKV_HEAD


def _cparams(semantics):
    return pltpu.CompilerParams(dimension_semantics=semantics, vmem_limit_bytes=VMEM_LIMIT_BYTES)


def _dot(a, b):
    return jnp.dot(a, b, preferred_element_type=F32)


def _dot_nt(a, b):
    return lax.dot_general(a, b, (((1,), (1,)), ((), ())), preferred_element_type=F32)


def _rms_norm_rows(x, g, eps):
    ms = jnp.mean(x * x, axis=-1, keepdims=True)
    return x * lax.rsqrt(ms + eps) * g


def _norm_matmul_kernel(x_ref, g_ref, w_ref, o_ref, h_ref, *, tn, gelu):
    @pl.when(pl.program_id(1) == 0)
    def _():
        h_ref[...] = _rms_norm_rows(x_ref[...], g_ref[...], NORM_EPS).astype(BF16)

    for n0 in range(0, tn, MXU_COLS):
        acc = _dot(h_ref[...], w_ref[:, n0:n0 + MXU_COLS])
        if gelu:
            acc = 0.5 * acc * (1.0 + lax.erf(acc * math.sqrt(0.5)))
        o_ref[:, n0:n0 + MXU_COLS] = acc.astype(o_ref.dtype)


def _norm_matmul(x, g, w, *, tm, tn, out_dtype, gelu=False):
    m, k = x.shape
    n = w.shape[1]
    return pl.pallas_call(
        functools.partial(_norm_matmul_kernel, tn=tn, gelu=gelu),
        grid=(m // tm, n // tn),
        in_specs=[
            pl.BlockSpec((tm, k), lambda i, j: (i, 0)),
            pl.BlockSpec((1, k), lambda i, j: (0, 0)),
            pl.BlockSpec((k, tn), lambda i, j: (0, j)),
        ],
        out_specs=pl.BlockSpec((tm, tn), lambda i, j: (i, j)),
        out_shape=jax.ShapeDtypeStruct((m, n), out_dtype),
        scratch_shapes=[pltpu.VMEM((tm, k), BF16)],
        compiler_params=_cparams(("arbitrary", "arbitrary")),
        name="norm_matmul_gelu" if gelu else "norm_matmul",
    )(x, g, w)


def _norm_matmul_rope_kernel(x_ref, g_ref, w_ref, cos_ref, sin_ref, o_ref, h_ref, *, tn, q_tiles, rope_tiles):
    j = pl.program_id(1)

    @pl.when(j == 0)
    def _():
        h_ref[...] = _rms_norm_rows(x_ref[...], g_ref[...], NORM_EPS).astype(BF16)

    def chunks(epilogue):
        for n0 in range(0, tn, MXU_COLS):
            acc = _dot(h_ref[...], w_ref[:, n0:n0 + MXU_COLS])
            o_ref[:, n0:n0 + MXU_COLS] = epilogue(acc).astype(o_ref.dtype)

    def rope(acc):
        reps = MXU_COLS // LANES
        c = jnp.concatenate([cos_ref[...]] * reps, axis=1)
        s = jnp.concatenate([sin_ref[...]] * reps, axis=1)
        lane = lax.broadcasted_iota(jnp.int32, acc.shape, 1)
        partner = jnp.where((lane & (DIFF_HALF // 2)) == 0,
                            pltpu.roll(acc, MXU_COLS - DIFF_HALF // 2, 1),
                            pltpu.roll(acc, DIFF_HALF // 2, 1))
        scale = jnp.where(j < q_tiles, DIFF_HALF ** -0.5 * LOG2_E, 1.0).astype(F32)
        return (acc * c + partner * s) * scale

    @pl.when(j < rope_tiles)
    def _():
        chunks(rope)

    @pl.when(j >= rope_tiles)
    def _():
        chunks(lambda acc: acc)


def _norm_matmul_rope(x, g, w, cos, sin, *, tm, tn, seq):
    m, k = x.shape
    n = w.shape[1]
    q_tiles = DIFF_WIDTH // tn
    t_tiles = seq // tm
    return pl.pallas_call(
        functools.partial(_norm_matmul_rope_kernel, tn=tn, q_tiles=q_tiles, rope_tiles=2 * q_tiles),
        grid=(m // tm, n // tn),
        in_specs=[
            pl.BlockSpec((tm, k), lambda i, j: (i, 0)),
            pl.BlockSpec((1, k), lambda i, j: (0, 0)),
            pl.BlockSpec((k, tn), lambda i, j: (0, j)),
            pl.BlockSpec((tm, LANES), lambda i, j: (i % t_tiles, 0)),
            pl.BlockSpec((tm, LANES), lambda i, j: (i % t_tiles, 0)),
        ],
        out_specs=pl.BlockSpec((tm, tn), lambda i, j: (i, j)),
        out_shape=jax.ShapeDtypeStruct((m, n), BF16),
        scratch_shapes=[pltpu.VMEM((tm, k), BF16)],
        compiler_params=_cparams(("arbitrary", "arbitrary")),
        name="norm_matmul_rope",
    )(x, g, w, cos, sin)


def _sigmoid(y):
    return 1.0 / (1.0 + jnp.exp(-y))


def _split_bf16(x, parts):
    out = []
    rem = x
    for _ in range(parts):
        p = rem.astype(BF16)
        out.append(p)
        rem = rem - p.astype(F32)
    return out


def _rwkv_kernel(z_ref, mu_ref, w0_ref, wdec_ref, a0_ref, wa_ref, wg_ref, kk_ref, ka_ref, rk_ref,
                 lw_ref, lb_ref, o_ref, s_ref, prev_ref):
    L = RWKV_CHUNK
    C = RWKV_WIDTH
    n_pairs = C // PAIR
    n_batch = z_ref.shape[0]

    @pl.when(pl.program_id(0) == 0)
    def _():
        s_ref[...] = jnp.zeros_like(s_ref)
        prev_ref[...] = jnp.zeros_like(prev_ref)

    ti = lax.broadcasted_iota(jnp.int32, (L, L), 0)
    tj = lax.broadcasted_iota(jnp.int32, (L, L), 1)
    tril_ones = jnp.where(ti >= tj, 1.0, 0.0).astype(BF16)

    def prepare(bi):
        z = z_ref[bi].astype(F32)
        row = lax.broadcasted_iota(jnp.int32, z.shape, 0)
        z_prev = jnp.where(row == 0, prev_ref[bi, 0:1, :], pltpu.roll(z, 1, 0))
        prev_ref[bi, 0:1, :] = z[L - 1:L, :]
        zs = z + (z_prev - z) * mu_ref[...]

        r = zs[:, 0:C]
        k = zs[:, C:2 * C]
        v = zs[:, 2 * C:3 * C]
        xw = zs[:, RW_XW:RW_XA]
        xa = zs[:, RW_XA:RW_XG]
        xg = zs[:, RW_XG:RW_PAD_COLS]

        y_dec = w0_ref[...] + _dot(jnp.tanh(xw).astype(BF16), wdec_ref[...])
        logd = -math.exp(-0.5) * _sigmoid(y_dec)
        a_lr = _sigmoid(a0_ref[...] + _dot(xa.astype(BF16), wa_ref[...]))
        gate = _dot(_sigmoid(xg).astype(BF16), wg_ref[...])
        kk_raw = k * kk_ref[...]
        k2 = k * (1.0 + (a_lr - 1.0) * ka_ref[...])
        rkk = r * k2 * rk_ref[...]
        cum = sum(_dot(tril_ones, p) for p in _split_bf16(logd, 3))
        return r, k2, v, logd, a_lr, gate, kk_raw, rkk, cum

    rr = lax.broadcasted_iota(jnp.int32, (PAIR, PAIR), 0)
    cc = lax.broadcasted_iota(jnp.int32, (PAIR, PAIR), 1)
    same_head = (rr // RWKV_HEAD) == (cc // RWKV_HEAD)
    strict = jnp.logical_and(same_head, rr > cc)
    incl = jnp.logical_and(same_head, rr >= cc)
    head_ones = jnp.where(same_head, 1.0, 0.0).astype(BF16)
    eye = jnp.where(rr == cc, 1.0, 0.0).astype(F32)
    first_head = lax.broadcasted_iota(jnp.int32, (L, PAIR), 1) < RWKV_HEAD

    def head_sum(x):
        return _dot(x.astype(BF16), head_ones)

    def stack(x):
        return jnp.concatenate([jnp.where(first_head, x, 0.0), jnp.where(first_head, 0.0, x)], axis=0)

    n_steps = int(math.log2(L)) - 1
    idx = range(n_pairs)
    sls = [slice(p * PAIR, (p + 1) * PAIR) for p in idx]

    def elementwise(prep, p):
        r, k2, v, logd, a_lr, _, kk_raw, rkk, cum = prep
        sl = sls[p]
        kk_p = kk_raw[:, sl]
        kk_n = kk_p * lax.rsqrt(jnp.maximum(head_sum(kk_p * kk_p), 1e-24))
        bonus = head_sum(rkk[:, sl])
        a_p = -kk_n
        b_p = kk_n * a_lr[:, sl]
        r_p, k_p = r[:, sl], k2[:, sl]
        cu = cum[:, sl]
        c_end = cu[L - 1:L, :]
        e_pos = jnp.exp(cu)
        e_neg = jnp.exp(-cu)
        e_prev = jnp.exp(cu - logd[:, sl])
        e_end = jnp.exp(c_end - cu)
        lhs = jnp.concatenate([stack(a_p * e_prev), stack(r_p * e_pos)], axis=0).astype(BF16)
        rhs = jnp.concatenate([stack(b_p * e_neg), stack(k_p * e_neg)], axis=0).astype(BF16)
        bk_end = jnp.concatenate([stack(b_p * e_end), stack(k_p * e_end)], axis=0).astype(BF16)
        return lhs, rhs, bk_end, stack(v[:, sl]), jnp.exp(c_end), bonus

    def stages(bi, prep, elems, filler):
        v, gate = prep[2], prep[5]
        lhs, rhs, bk_end, v_st, d_end, bonus_s = zip(*elems)
        pm = [_dot_nt(lhs[i], rhs[i]) for i in idx]
        filler()
        s_old = [s_ref[bi, p] for p in idx]
        sh = [_dot_nt(lhs[i], s_old[i].astype(BF16)) for i in idx]
        filler()
        p_ab = [jnp.where(strict, pm[i][:PAIR, :PAIR], 0.0) for i in idx]
        p_ak = [jnp.where(strict, pm[i][:PAIR, PAIR:], 0.0).astype(BF16) for i in idx]
        p_rbk = [jnp.concatenate([jnp.where(incl, pm[i][PAIR:, :PAIR], 0.0),
                                  jnp.where(incl, pm[i][PAIR:, PAIR:], 0.0)], axis=1).astype(BF16) for i in idx]
        akv = [_dot(p_ak[i], v_st[i].astype(BF16)) for i in idx]
        filler()

        t_inv = [eye + p_ab[i] for i in idx]
        xb = [p_ab[i].astype(BF16) for i in idx]
        xb = [_dot(xb[i], xb[i]).astype(BF16) for i in idx]
        filler()
        for it in range(n_steps):
            if it + 1 < n_steps:
                both = [_dot(xb[i], jnp.concatenate([xb[i], t_inv[i].astype(BF16)], axis=1)) for i in idx]
                xb = [both[i][:, :PAIR].astype(BF16) for i in idx]
                t_inv = [t_inv[i] + both[i][:, PAIR:] for i in idx]
            else:
                t_inv = [t_inv[i] + _dot(xb[i], t_inv[i].astype(BF16)) for i in idx]
            filler()

        u_st = [_dot(t_inv[i].astype(BF16), (sh[i][:PAIR] + akv[i]).astype(BF16)) for i in idx]
        filler()
        uv = [jnp.concatenate([u_st[i], v_st[i]], axis=0) for i in idx]
        y_st = [sh[i][PAIR:] + _dot(p_rbk[i], uv[i].astype(BF16)) for i in idx]
        filler()
        s_upd = [_dot(uv[i].T.astype(BF16), bk_end[i]) for i in idx]
        for p in idx:
            s_ref[bi, p] = s_old[p] * d_end[p] + s_upd[p]

        y = [y_st[i][:L] + y_st[i][L:] for i in idx]
        mean = [head_sum(y[i]) * (1.0 / RWKV_HEAD) for i in idx]
        yc = [y[i] - mean[i] for i in idx]
        var = [head_sum(yc[i] * yc[i]) * (1.0 / RWKV_HEAD) for i in idx]
        for i in idx:
            sl = sls[i]
            yn = yc[i] * lax.rsqrt(var[i] + RWKV_LNX_EPS) * lw_ref[:, sl] + lb_ref[:, sl]
            o_ref[bi, :, sl] = ((yn + bonus_s[i] * v[:, sl]) * gate[:, sl]).astype(o_ref.dtype)

    prepared = [prepare(bi) for bi in range(n_batch)]
    elems = [elementwise(prepared[0], p) for p in idx]
    for bi in range(n_batch):
        nxt_elems, queue = [], []
        if bi + 1 < n_batch:
            queue = [functools.partial(lambda row, p: nxt_elems.append(elementwise(prepared[row], p)), bi + 1, p)
                     for p in idx]

        def filler():
            if queue:
                queue.pop(0)()

        stages(bi, prepared[bi], elems, filler)
        while queue:
            queue.pop(0)()
        elems = nxt_elems


def _rwkv_time_mix(z, mu, w0, wdec, a0, wa, wg, k_k, k_a, r_k, lnx_w, lnx_b, *, batch, seq):
    L = RWKV_CHUNK
    n_chunks = seq // L
    n_pairs = RWKV_WIDTH // PAIR
    full = lambda shape: pl.BlockSpec(shape, lambda c: (0,) * len(shape))
    out = pl.pallas_call(
        _rwkv_kernel,
        grid=(n_chunks,),
        in_specs=[
            pl.BlockSpec((batch, L, RW_PAD_COLS), lambda c: (0, c, 0)),
            full((1, RW_PAD_COLS)),
            full((1, RWKV_WIDTH)), full((LANES, RWKV_WIDTH)),
            full((1, RWKV_WIDTH)), full((LANES, RWKV_WIDTH)),
            full((2 * LANES, RWKV_WIDTH)),
            full((1, RWKV_WIDTH)), full((1, RWKV_WIDTH)), full((1, RWKV_WIDTH)),
            full((1, RWKV_WIDTH)), full((1, RWKV_WIDTH)),
        ],
        out_specs=pl.BlockSpec((batch, L, RWKV_WIDTH), lambda c: (0, c, 0)),
        out_shape=jax.ShapeDtypeStruct((batch, seq, RWKV_WIDTH), BF16),
        scratch_shapes=[pltpu.VMEM((batch, n_pairs, PAIR, PAIR), F32), pltpu.VMEM((batch, 8, RW_PAD_COLS), F32)],
        compiler_params=_cparams(("arbitrary",)),
        name="rwkv7_time_mix",
    )(z.reshape(batch, seq, RW_PAD_COLS), mu, w0, wdec, a0, wa, wg, k_k, k_a, r_k, lnx_w, lnx_b)
    return out.reshape(batch * seq, RWKV_WIDTH)


def _attn_kernel(lam_ref, g_ref, q_ref, k_ref, v_ref, o_ref, vt_ref, acc_ref, qq_ref, s_ref, *, seq, n_sub,
                 lambda_init):
    i = pl.program_id(2)
    sb = ATTN_SUB
    two = 2 * sb
    kb = ATTN_KEYS
    steps_per_q = (n_sub * sb) // kb

    @pl.when(i == 0)
    def _():
        def transpose_block(t, carry):
            start = pl.multiple_of(t * kb, kb)
            vt_ref[t] = v_ref[pl.ds(start, kb), :].astype(F32).T.astype(BF16)
            return carry
        lax.fori_loop(0, seq // kb, transpose_block, 0)

    lane = lax.broadcasted_iota(jnp.int32, (sb, DIFF_HEAD), 1)
    for c in range(n_sub):
        qc = q_ref[c * sb:(c + 1) * sb, :]
        zero = jnp.zeros_like(qc)
        qq_ref[c] = jnp.concatenate([jnp.where(lane < DIFF_HALF, qc, zero),
                                     jnp.where(lane < DIFF_HALF, zero, qc)], axis=0)
        acc_ref[c] = jnp.zeros((DIFF_HEAD, two), F32)

    key_idx = lax.broadcasted_iota(jnp.int32, (kb, two), 0)
    qry_idx = lax.broadcasted_iota(jnp.int32, (kb, two), 1) & (sb - 1)

    def scores(c, j):
        start = pl.multiple_of(j * kb, kb)
        return _dot_nt(k_ref[pl.ds(start, kb), :], qq_ref[c])

    def softmax_pv(c, j, st, m, l, mask):
        if mask is not None:
            st = jnp.where(mask, st, NEG_INF)
        m_new = jnp.maximum(m, jnp.max(st, axis=0, keepdims=True))
        alpha = jnp.exp2(m - m_new)
        pr = jnp.exp2(st - m_new)
        l_new = alpha * l + jnp.sum(pr, axis=0, keepdims=True)
        acc_ref[c] = alpha * acc_ref[c] + _dot(vt_ref[j], pr.astype(BF16))
        return m_new, l_new

    for c in range(n_sub):
        s_ref[0, c] = scores(c, 0)

    def full_steps(j0, n_steps, carry):
        ms, ls = list(carry[0]), list(carry[1])
        for h in range(n_steps):
            for c in range(n_sub):
                s_ref[1 - h % 2, c] = scores(c, j0 + h + 1)
            for c in range(n_sub):
                ms[c], ls[c] = softmax_pv(c, j0 + h, s_ref[h % 2, c], ms[c], ls[c], None)
        return tuple(ms), tuple(ls)

    init = (tuple(jnp.full((1, two), NEG_INF, F32) for _ in range(n_sub)),
            tuple(jnp.zeros((1, two), F32) for _ in range(n_sub)))
    n_full = i * steps_per_q
    n_short = (n_full % ATTN_TRIP) // 2
    carry = lax.fori_loop(0, n_short, lambda t, cr: full_steps(2 * t, 2, cr), init)
    ms, ls = lax.fori_loop(0, n_full // ATTN_TRIP,
                           lambda t, cr: full_steps(2 * n_short + ATTN_TRIP * t, ATTN_TRIP, cr), carry)
    ms, ls = list(ms), list(ls)

    first_diag = i * steps_per_q
    needed = lambda s, c: s * kb < (c + 1) * sb
    tail = {(s, c): scores(c, first_diag + s)
            for s in range(1, steps_per_q) for c in range(n_sub) if needed(s, c)}
    for s in range(steps_per_q):
        for c in range(n_sub):
            if not needed(s, c):
                continue
            st = s_ref[0, c] if s == 0 else tail[(s, c)]
            partial = (s + 1) * kb - 1 > c * sb
            mask = (key_idx + s * kb <= qry_idx + c * sb) if partial else None
            ms[c], ls[c] = softmax_pv(c, first_diag + s, st, ms[c], ls[c], mask)

    lam_p = lam_ref[...]
    lam = (jnp.exp(jnp.sum(lam_p[0:1] * lam_p[1:2], axis=-1, keepdims=True))
           - jnp.exp(jnp.sum(lam_p[2:3] * lam_p[3:4], axis=-1, keepdims=True)) + lambda_init)
    for c in range(n_sub):
        o = acc_ref[c] / ls[c]
        d = o[:, :sb] - lam * o[:, sb:]
        ms_d = jnp.mean(d * d, axis=0, keepdims=True)
        d = d * lax.rsqrt(ms_d + SUBLN_EPS) * g_ref[...] * (1.0 - lambda_init)
        o_ref[c * sb:(c + 1) * sb, :] = d.T.astype(o_ref.dtype)


def _diff_attention(qkv, lam_params, subln_g_col, *, batch, seq, n_sub, lambda_init):
    qb = n_sub * ATTN_SUB
    assert qb % (2 * ATTN_KEYS) == 0 and seq % qb == 0
    nq = seq // qb
    h = DIFF_HEADS
    return pl.pallas_call(
        functools.partial(_attn_kernel, seq=seq, n_sub=n_sub, lambda_init=lambda_init),
        grid=(batch, h, nq),
        in_specs=[
            pl.BlockSpec((8, LANES), lambda b, hh, i: (0, 0)),
            pl.BlockSpec((DIFF_HEAD, 1), lambda b, hh, i: (0, 0)),
            pl.BlockSpec((qb, DIFF_HEAD), lambda b, hh, i: (b * nq + i, hh)),
            pl.BlockSpec((seq, DIFF_HEAD), lambda b, hh, i: (b, h + hh)),
            pl.BlockSpec((seq, DIFF_HEAD), lambda b, hh, i: (b, 2 * h + hh)),
        ],
        out_specs=pl.BlockSpec((qb, DIFF_HEAD), lambda b, hh, i: (b * nq + i, hh)),
        out_shape=jax.ShapeDtypeStruct((batch * seq, DIFF_WIDTH), BF16),
        scratch_shapes=[pltpu.VMEM((seq // ATTN_KEYS, DIFF_HEAD, ATTN_KEYS), BF16),
                        pltpu.VMEM((n_sub, DIFF_HEAD, 2 * ATTN_SUB), F32),
                        pltpu.VMEM((n_sub, 2 * ATTN_SUB, DIFF_HEAD), BF16),
                        pltpu.VMEM((2, n_sub, ATTN_KEYS, 2 * ATTN_SUB), F32)],
        compiler_params=_cparams(("arbitrary", "arbitrary", "arbitrary")),
        name="diff_attention",
    )(lam_params, subln_g_col, qkv, qkv, qkv)


def _outproj2_kernel(ya_ref, yb_ref, x_ref, wa_ref, wb_ref, o_ref):
    o_ref[...] = x_ref[...] + _dot(ya_ref[...], wa_ref[...]) + _dot(yb_ref[...], wb_ref[...])


def _outproj2(ya, yb, x, wa, wb, *, tm):
    m, d = x.shape
    return pl.pallas_call(
        _outproj2_kernel,
        grid=(m // tm,),
        in_specs=[
            pl.BlockSpec((tm, ya.shape[1]), lambda i: (i, 0)),
            pl.BlockSpec((tm, yb.shape[1]), lambda i: (i, 0)),
            pl.BlockSpec((tm, d), lambda i: (i, 0)),
            pl.BlockSpec(wa.shape, lambda i: (0, 0)),
            pl.BlockSpec(wb.shape, lambda i: (0, 0)),
        ],
        out_specs=pl.BlockSpec((tm, d), lambda i: (i, 0)),
        out_shape=jax.ShapeDtypeStruct((m, d), F32),
        compiler_params=_cparams(("arbitrary",)),
        name="even_out_proj",
    )(ya, yb, x, wa, wb)


def _sgu_kernel(u_ref, v_ref, x_ref, lng_ref, lnb_ref, ws_ref, bs_ref, wo_ref, o_ref, vn_ref, gated_ref, *, tm):
    ch = GMLP_CHUNK
    v = v_ref[...].astype(F32)
    mu = jnp.mean(v, axis=-1, keepdims=True)
    vc = v - mu
    var = jnp.mean(vc * vc, axis=-1, keepdims=True)
    vn_ref[...] = (vc * lax.rsqrt(var + LN_EPS) * lng_ref[...] + lnb_ref[...]).astype(BF16)
    ti = lax.broadcasted_iota(jnp.int32, (ch, ch), 0)
    tj = lax.broadcasted_iota(jnp.int32, (ch, ch), 1)
    causal = ti >= tj
    for g in range(GMLP_GROUPS):
        cols = slice(g * LANES, (g + 1) * LANES)
        wg = jnp.where(causal, ws_ref[g], 0.0).astype(BF16)
        bias = bs_ref[:, g:g + 1]
        for c in range(tm // ch):
            rows = slice(c * ch, (c + 1) * ch)
            sv = _dot(wg, vn_ref[rows, cols]) + bias
            gated_ref[rows, cols] = (u_ref[rows, cols].astype(F32) * sv).astype(BF16)
    o_ref[...] = x_ref[...] + _dot(gated_ref[...], wo_ref[...])


def _sgu_outproj(u_v, x, ln_g, ln_b, w_s, b_s_t, w_out, *, tm):
    m, d = x.shape
    n_half = GMLP_WIDTH // d
    return pl.pallas_call(
        functools.partial(_sgu_kernel, tm=tm),
        grid=(m // tm,),
        in_specs=[
            pl.BlockSpec((tm, GMLP_WIDTH), lambda i: (i, 0)),
            pl.BlockSpec((tm, GMLP_WIDTH), lambda i: (i, n_half)),
            pl.BlockSpec((tm, d), lambda i: (i, 0)),
            pl.BlockSpec((1, GMLP_WIDTH), lambda i: (0, 0)),
            pl.BlockSpec((1, GMLP_WIDTH), lambda i: (0, 0)),
            pl.BlockSpec(w_s.shape, lambda i: (0, 0, 0)),
            pl.BlockSpec(b_s_t.shape, lambda i: (0, 0)),
            pl.BlockSpec(w_out.shape, lambda i: (0, 0)),
        ],
        out_specs=pl.BlockSpec((tm, d), lambda i: (i, 0)),
        out_shape=jax.ShapeDtypeStruct((m, d), F32),
        scratch_shapes=[pltpu.VMEM((tm, GMLP_WIDTH), BF16), pltpu.VMEM((tm, GMLP_WIDTH), BF16)],
        compiler_params=_cparams(("arbitrary",)),
        name="sgu_out_proj",
    )(u_v, u_v, x, ln_g, ln_b, w_s, b_s_t, w_out)


def _ffn_kernel(x_ref, g_ref, wg_ref, wu_ref, wd_ref, fg_ref, o_ref, h_ref, *, final):
    j = pl.program_id(1)

    @pl.when(j == 0)
    def _():
        x = x_ref[...]
        h_ref[...] = _rms_norm_rows(x, g_ref[...], NORM_EPS).astype(BF16)
        o_ref[...] = x

    h = h_ref[...]
    gate = _dot(h, wg_ref[...])
    up = _dot(h, wu_ref[...])
    act = gate * _sigmoid(gate) * up
    o_ref[...] += _dot(act.astype(BF16), wd_ref[...])

    if final:
        @pl.when(j == pl.num_programs(1) - 1)
        def _():
            o_ref[...] = _rms_norm_rows(o_ref[...], fg_ref[...], NORM_EPS)


def _ffn(x, g, w_gate, w_up, w_down, final_g, *, tm, th, final):
    m, d = x.shape
    hid = w_gate.shape[1]
    return pl.pallas_call(
        functools.partial(_ffn_kernel, final=final),
        grid=(m // tm, hid // th),
        in_specs=[
            pl.BlockSpec((tm, d), lambda i, j: (i, 0)),
            pl.BlockSpec((1, d), lambda i, j: (0, 0)),
            pl.BlockSpec((d, th), lambda i, j: (0, j)),
            pl.BlockSpec((d, th), lambda i, j: (0, j)),
            pl.BlockSpec((th, d), lambda i, j: (j, 0)),
            pl.BlockSpec((1, d), lambda i, j: (0, 0)),
        ],
        out_specs=pl.BlockSpec((tm, d), lambda i, j: (i, 0)),
        out_shape=jax.ShapeDtypeStruct((m, d), F32),
        scratch_shapes=[pltpu.VMEM((tm, d), BF16)],
        compiler_params=_cparams(("arbitrary", "arbitrary")),
        name="swiglu_ffn",
    )(x, g, w_gate, w_up, w_down, final_g)


def _pad_cols(a, width):
    return jnp.pad(a, ((0, 0), (0, width - a.shape[1])))


def _pad_rows(a, height):
    return jnp.pad(a, ((0, height - a.shape[0]), (0, 0)))


def _rwkv_col_layout(a):
    c = RWKV_WIDTH
    xw = a[:, 3 * c:3 * c + DECAY_LORA]
    xa = a[:, 3 * c + DECAY_LORA:3 * c + DECAY_LORA + AAA_LORA]
    xg = a[:, 3 * c + DECAY_LORA + AAA_LORA:]
    return jnp.concatenate(
        [a[:, :3 * c], _pad_cols(xw, LANES), _pad_cols(xa, LANES), _pad_cols(xg, 2 * LANES)], axis=1)


def _rope_tables(seq):
    inv = ROPE_THETA ** (-jnp.arange(0, DIFF_HALF, 2, dtype=F32) / DIFF_HALF)
    ang = jnp.arange(seq, dtype=F32)[:, None] * inv[None, :]
    cos, sin = jnp.cos(ang), jnp.sin(ang)
    reps = LANES // DIFF_HALF
    return (jnp.tile(jnp.concatenate([cos, cos], axis=1), (1, reps)),
            jnp.tile(jnp.concatenate([-sin, sin], axis=1), (1, reps)))


def _tile_sizes(m, seq):
    return dict(
        proj_tm=min(1024, seq), proj_tn=1024, rwkv_proj_tn=RW_PAD_COLS // 2,
        ffn_tm=min(1024, m), ffn_th=512,
        out_tm=min(512, m), sgu_tm=min(512, m),
        attn_sub=min(4, seq // ATTN_SUB),
    )


def kernel(x, mix_norm, ffn_norm, ffn_w_gate, ffn_w_up, ffn_w_down, ev_w_in, ev_mu, ev_w0, ev_w_dec_up, ev_a0,
           ev_w_a_up, ev_w_g_up, ev_k_k, ev_k_a, ev_r_k, ev_lnx_w, ev_lnx_b, ev_lam_q1, ev_lam_k1, ev_lam_q2,
           ev_lam_k2, ev_subln_g, ev_w_out, od_w_in, od_ln_g, od_ln_b, od_w_s, od_b_s, od_w_out, final_norm):
    batch, seq, d = x.shape
    m = batch * seq
    ts = _tile_sizes(m, seq)
    cos, sin = _rope_tables(seq)
    row = lambda a: a.reshape(1, -1).astype(F32)
    xf = x.reshape(m, d)
    for i in range(DEPTH):
        j = i // 2
        g_mix = row(mix_norm[i])
        if i % 2 == 0:
            lambda_init = 0.8 - 0.6 * math.exp(-0.3 * i)
            w_in = ev_w_in[j]
            w_rwkv = _rwkv_col_layout(w_in[:, :RWKV_COLS]).astype(BF16)
            w_diff = w_in[:, RWKV_COLS:].astype(BF16)
            z = _norm_matmul(xf, g_mix, w_rwkv, tm=ts["proj_tm"], tn=ts["rwkv_proj_tn"], out_dtype=BF16)
            qkv = _norm_matmul_rope(xf, g_mix, w_diff, cos, sin, tm=ts["proj_tm"], tn=ts["proj_tn"], seq=seq)
            y_a = _rwkv_time_mix(
                z, _rwkv_col_layout(row(ev_mu[j])), row(ev_w0[j]),
                _pad_rows(ev_w_dec_up[j], LANES).astype(BF16), row(ev_a0[j]),
                _pad_rows(ev_w_a_up[j], LANES).astype(BF16), _pad_rows(ev_w_g_up[j], 2 * LANES).astype(BF16),
                row(ev_k_k[j]), row(ev_k_a[j]), row(ev_r_k[j]), row(ev_lnx_w[j]), row(ev_lnx_b[j]),
                batch=batch, seq=seq)
            lam_params = _pad_rows(_pad_cols(
                jnp.stack([ev_lam_q1[j], ev_lam_k1[j], ev_lam_q2[j], ev_lam_k2[j]]).astype(F32), LANES), 8)
            y_b = _diff_attention(qkv, lam_params, ev_subln_g[j].reshape(-1, 1).astype(F32), batch=batch, seq=seq,
                                  n_sub=ts["attn_sub"], lambda_init=lambda_init)
            w_out = ev_w_out[j].astype(BF16)
            xf = _outproj2(y_a, y_b, xf, w_out[:RWKV_WIDTH], w_out[RWKV_WIDTH:], tm=ts["out_tm"])
        else:
            u_v = _norm_matmul(xf, g_mix, od_w_in[j].astype(BF16), tm=ts["proj_tm"], tn=ts["proj_tn"],
                               out_dtype=BF16, gelu=True)
            xf = _sgu_outproj(u_v, xf, row(od_ln_g[j]), row(od_ln_b[j]), od_w_s[j].astype(F32),
                              od_b_s[j].T.astype(F32), od_w_out[j].astype(BF16), tm=ts["sgu_tm"])
        xf = _ffn(xf, row(ffn_norm[i]), ffn_w_gate[i].astype(BF16), ffn_w_up[i].astype(BF16),
                  ffn_w_down[i].astype(BF16), row(final_norm), tm=ts["ffn_tm"], th=ts["ffn_th"],
                  final=(i == DEPTH - 1))
    return xf.reshape(batch, seq, d)
```

```python
import functools
import math

import jax
import jax.numpy as jnp
from jax import lax
from jax.experimental import pallas as pl
from jax.experimental.pallas import tpu as pltpu

F32 = jnp.float32
BF16 = jnp.bfloat16

D_MODEL = 2048
DEPTH = 4
RWKV_WIDTH = D_MODEL // 2
RWKV_HEAD = 64
DECAY_LORA = 64
AAA_LORA = 64
GATE_LORA = 160
RWKV_COLS = 3 * RWKV_WIDTH + DECAY_LORA + AAA_LORA + GATE_LORA
RWKV_LNX_EPS = 64e-5
DIFF_WIDTH = D_MODEL - RWKV_WIDTH
DIFF_HEAD = 128
DIFF_HALF = DIFF_HEAD // 2
DIFF_HEADS = DIFF_WIDTH // DIFF_HEAD
ROPE_THETA = 10000.0
NEG_INF = -1e30
GMLP_WIDTH = D_MODEL
GMLP_CHUNK = 128
GMLP_GROUPS = 16
FFN_HIDDEN = 5632
NORM_EPS = 1e-6
SUBLN_EPS = 1e-5
LN_EPS = 1e-5

LANES = 128
MXU_COLS = 256
VMEM_LIMIT_BYTES = 56 * 1024 * 1024

RW_XW = 3 * RWKV_WIDTH
RW_XA = RW_XW + LANES
RW_XG = RW_XA + LANES
RW_PAD_COLS = RW_XG + 2 * LANES

RWKV_CHUNK = 64
ATTN_SUB = 128
ATTN_KEYS = 256
ATTN_ONES_ROWS = 16
ATTN_TRIP = 8
LOG2_E = 1.4426950408889634
PAIR = 2 * RWKV_HEAD


def _cparams(semantics):
    return pltpu.CompilerParams(dimension_semantics=semantics, vmem_limit_bytes=VMEM_LIMIT_BYTES)


def _dot(a, b):
    return jnp.dot(a, b, preferred_element_type=F32)


def _dot_nt(a, b):
    return lax.dot_general(a, b, (((1,), (1,)), ((), ())), preferred_element_type=F32)


def _rms_norm_rows(x, g, eps):
    ms = jnp.mean(x * x, axis=-1, keepdims=True)
    return x * lax.rsqrt(ms + eps) * g


def _norm_matmul_kernel(x_ref, g_ref, w_ref, o_ref, h_ref, *, tn, gelu):
    @pl.when(pl.program_id(1) == 0)
    def _():
        h_ref[...] = _rms_norm_rows(x_ref[...], g_ref[...], NORM_EPS).astype(BF16)

    for n0 in range(0, tn, MXU_COLS):
        acc = _dot(h_ref[...], w_ref[:, n0:n0 + MXU_COLS])
        if gelu:
            acc = 0.5 * acc * (1.0 + lax.erf(acc * math.sqrt(0.5)))
        o_ref[:, n0:n0 + MXU_COLS] = acc.astype(o_ref.dtype)


def _norm_matmul(x, g, w, *, tm, tn, out_dtype, gelu=False):
    m, k = x.shape
    n = w.shape[1]
    return pl.pallas_call(
        functools.partial(_norm_matmul_kernel, tn=tn, gelu=gelu),
        grid=(m // tm, n // tn),
        in_specs=[
            pl.BlockSpec((tm, k), lambda i, j: (i, 0)),
            pl.BlockSpec((1, k), lambda i, j: (0, 0)),
            pl.BlockSpec((k, tn), lambda i, j: (0, j)),
        ],
        out_specs=pl.BlockSpec((tm, tn), lambda i, j: (i, j)),
        out_shape=jax.ShapeDtypeStruct((m, n), out_dtype),
        scratch_shapes=[pltpu.VMEM((tm, k), BF16)],
        compiler_params=_cparams(("arbitrary", "arbitrary")),
        name="norm_matmul_gelu" if gelu else "norm_matmul",
    )(x, g, w)


def _norm_matmul_rope_kernel(x_ref, g_ref, w_ref, cos_ref, sin_ref, o_ref, h_ref, *, tn, q_tiles, rope_tiles):
    j = pl.program_id(1)

    @pl.when(j == 0)
    def _():
        h_ref[...] = _rms_norm_rows(x_ref[...], g_ref[...], NORM_EPS).astype(BF16)

    def chunks(epilogue):
        for n0 in range(0, tn, MXU_COLS):
            acc = _dot(h_ref[...], w_ref[:, n0:n0 + MXU_COLS])
            o_ref[:, n0:n0 + MXU_COLS] = epilogue(acc).astype(o_ref.dtype)

    def rope(acc):
        reps = MXU_COLS // LANES
        c = jnp.concatenate([cos_ref[...]] * reps, axis=1)
        s = jnp.concatenate([sin_ref[...]] * reps, axis=1)
        lane = lax.broadcasted_iota(jnp.int32, acc.shape, 1)
        partner = jnp.where((lane & (DIFF_HALF // 2)) == 0,
                            pltpu.roll(acc, MXU_COLS - DIFF_HALF // 2, 1),
                            pltpu.roll(acc, DIFF_HALF // 2, 1))
        scale = jnp.where(j < q_tiles, DIFF_HALF ** -0.5 * LOG2_E, 1.0).astype(F32)
        return (acc * c + partner * s) * scale

    @pl.when(j < rope_tiles)
    def _():
        chunks(rope)

    @pl.when(j >= rope_tiles)
    def _():
        chunks(lambda acc: acc)


def _norm_matmul_rope(x, g, w, cos, sin, *, tm, tn, seq):
    m, k = x.shape
    n = w.shape[1]
    q_tiles = DIFF_WIDTH // tn
    t_tiles = seq // tm
    return pl.pallas_call(
        functools.partial(_norm_matmul_rope_kernel, tn=tn, q_tiles=q_tiles, rope_tiles=2 * q_tiles),
        grid=(m // tm, n // tn),
        in_specs=[
            pl.BlockSpec((tm, k), lambda i, j: (i, 0)),
            pl.BlockSpec((1, k), lambda i, j: (0, 0)),
            pl.BlockSpec((k, tn), lambda i, j: (0, j)),
            pl.BlockSpec((tm, LANES), lambda i, j: (i % t_tiles, 0)),
            pl.BlockSpec((tm, LANES), lambda i, j: (i % t_tiles, 0)),
        ],
        out_specs=pl.BlockSpec((tm, tn), lambda i, j: (i, j)),
        out_shape=jax.ShapeDtypeStruct((m, n), BF16),
        scratch_shapes=[pltpu.VMEM((tm, k), BF16)],
        compiler_params=_cparams(("arbitrary", "arbitrary")),
        name="norm_matmul_rope",
    )(x, g, w, cos, sin)


def _sigmoid(y):
    return 1.0 / (1.0 + jnp.exp(-y))


def _split_bf16(x, parts):
    out = []
    rem = x
    for _ in range(parts):
        p = rem.astype(BF16)
        out.append(p)
        rem = rem - p.astype(F32)
    return out


def _rwkv_kernel(z_ref, mu_ref, w0_ref, wdec_ref, a0_ref, wa_ref, wg_ref, kk_ref, ka_ref, rk_ref,
                 lw_ref, lb_ref, o_ref, s_ref, prev_ref):
    L = RWKV_CHUNK
    C = RWKV_WIDTH
    n_pairs = C // PAIR
    n_batch = z_ref.shape[0]

    @pl.when(pl.program_id(0) == 0)
    def _():
        s_ref[...] = jnp.zeros_like(s_ref)
        prev_ref[...] = jnp.zeros_like(prev_ref)

    ti = lax.broadcasted_iota(jnp.int32, (L, L), 0)
    tj = lax.broadcasted_iota(jnp.int32, (L, L), 1)
    tril_ones = jnp.where(ti >= tj, 1.0, 0.0).astype(BF16)

    def prepare(bi):
        z = z_ref[bi].astype(F32)
        row = lax.broadcasted_iota(jnp.int32, z.shape, 0)
        z_prev = jnp.where(row == 0, prev_ref[bi, 0:1, :], pltpu.roll(z, 1, 0))
        prev_ref[bi, 0:1, :] = z[L - 1:L, :]
        zs = z + (z_prev - z) * mu_ref[...]

        r = zs[:, 0:C]
        k = zs[:, C:2 * C]
        v = zs[:, 2 * C:3 * C]
        xw = zs[:, RW_XW:RW_XA]
        xa = zs[:, RW_XA:RW_XG]
        xg = zs[:, RW_XG:RW_PAD_COLS]

        y_dec = w0_ref[...] + _dot(jnp.tanh(xw).astype(BF16), wdec_ref[...])
        logd = -math.exp(-0.5) * _sigmoid(y_dec)
        a_lr = _sigmoid(a0_ref[...] + _dot(xa.astype(BF16), wa_ref[...]))
        gate = _dot(_sigmoid(xg).astype(BF16), wg_ref[...])
        kk_raw = k * kk_ref[...]
        k2 = k * (1.0 + (a_lr - 1.0) * ka_ref[...])
        rkk = r * k2 * rk_ref[...]
        cum = sum(_dot(tril_ones, p) for p in _split_bf16(logd, 2))
        return r, k2, v, logd, a_lr, gate, kk_raw, rkk, cum

    rr = lax.broadcasted_iota(jnp.int32, (PAIR, PAIR), 0)
    cc = lax.broadcasted_iota(jnp.int32, (PAIR, PAIR), 1)
    same_head = (rr // RWKV_HEAD) == (cc // RWKV_HEAD)
    strict = jnp.logical_and(same_head, rr > cc)
    incl = jnp.logical_and(same_head, rr >= cc)
    head_ones = jnp.where(same_head, 1.0, 0.0).astype(BF16)
    eye = jnp.where(rr == cc, 1.0, 0.0).astype(F32)
    first_head = lax.broadcasted_iota(jnp.int32, (L, PAIR), 1) < RWKV_HEAD

    def head_sum(x):
        return _dot(x.astype(BF16), head_ones)

    def stack(x):
        return jnp.concatenate([jnp.where(first_head, x, 0.0), jnp.where(first_head, 0.0, x)], axis=0)

    n_steps = int(math.log2(L)) - 1
    idx = range(n_pairs)
    sls = [slice(p * PAIR, (p + 1) * PAIR) for p in idx]

    def elementwise(prep, p):
        r, k2, v, logd, a_lr, _, kk_raw, rkk, cum = prep
        sl = sls[p]
        kk_p = kk_raw[:, sl]
        kk_n = kk_p * lax.rsqrt(jnp.maximum(head_sum(kk_p * kk_p), 1e-24))
        bonus = head_sum(rkk[:, sl])
        a_p = -kk_n
        b_p = kk_n * a_lr[:, sl]
        r_p, k_p = r[:, sl], k2[:, sl]
        cu = cum[:, sl]
        c_end = cu[L - 1:L, :]
        e_pos = jnp.exp(cu)
        e_neg = jnp.exp(-cu)
        e_prev = jnp.exp(cu - logd[:, sl])
        e_end = jnp.exp(c_end - cu)
        lhs = jnp.concatenate([stack(a_p * e_prev), stack(r_p * e_pos)], axis=0).astype(BF16)
        rhs = jnp.concatenate([stack(b_p * e_neg), stack(k_p * e_neg)], axis=0).astype(BF16)
        bk_end = jnp.concatenate([stack(b_p * e_end), stack(k_p * e_end)], axis=0).astype(BF16)
        return lhs, rhs, bk_end, stack(v[:, sl]), jnp.exp(c_end), bonus

    def stages(bi, prep, elems, filler):
        v, gate = prep[2], prep[5]
        lhs, rhs, bk_end, v_st, d_end, bonus_s = zip(*elems)
        pm = [_dot_nt(lhs[i], rhs[i]) for i in idx]
        filler()
        s_old = [s_ref[bi, p] for p in idx]
        sh = [_dot_nt(lhs[i], s_old[i].astype(BF16)) for i in idx]
        filler()
        p_ab = [jnp.where(strict, pm[i][:PAIR, :PAIR], 0.0) for i in idx]
        p_ak = [jnp.where(strict, pm[i][:PAIR, PAIR:], 0.0).astype(BF16) for i in idx]
        p_rbk = [jnp.concatenate([jnp.where(incl, pm[i][PAIR:, :PAIR], 0.0),
                                  jnp.where(incl, pm[i][PAIR:, PAIR:], 0.0)], axis=1).astype(BF16) for i in idx]
        akv = [_dot(p_ak[i], v_st[i].astype(BF16)) for i in idx]
        filler()

        t_inv = [eye + p_ab[i] for i in idx]
        xb = [p_ab[i].astype(BF16) for i in idx]
        xb = [_dot(xb[i], xb[i]).astype(BF16) for i in idx]
        filler()
        for it in range(n_steps):
            if it + 1 < n_steps:
                both = [_dot(xb[i], jnp.concatenate([xb[i], t_inv[i].astype(BF16)], axis=1)) for i in idx]
                xb = [both[i][:, :PAIR].astype(BF16) for i in idx]
                t_inv = [t_inv[i] + both[i][:, PAIR:] for i in idx]
            else:
                t_inv = [t_inv[i] + _dot(xb[i], t_inv[i].astype(BF16)) for i in idx]
            filler()

        u_st = [_dot(t_inv[i].astype(BF16), (sh[i][:PAIR] + akv[i]).astype(BF16)) for i in idx]
        filler()
        uv = [jnp.concatenate([u_st[i], v_st[i]], axis=0) for i in idx]
        y_st = [sh[i][PAIR:] + _dot(p_rbk[i], uv[i].astype(BF16)) for i in idx]
        filler()
        s_upd = [_dot(uv[i].T.astype(BF16), bk_end[i]) for i in idx]
        for p in idx:
            s_ref[bi, p] = s_old[p] * d_end[p] + s_upd[p]

        y = [y_st[i][:L] + y_st[i][L:] for i in idx]
        mean = [head_sum(y[i]) * (1.0 / RWKV_HEAD) for i in idx]
        yc = [y[i] - mean[i] for i in idx]
        var = [head_sum(yc[i] * yc[i]) * (1.0 / RWKV_HEAD) for i in idx]
        for i in idx:
            sl = sls[i]
            yn = yc[i] * lax.rsqrt(var[i] + RWKV_LNX_EPS) * lw_ref[:, sl] + lb_ref[:, sl]
            o_ref[bi, :, sl] = ((yn + bonus_s[i] * v[:, sl]) * gate[:, sl]).astype(o_ref.dtype)

    prepared = [prepare(bi) for bi in range(n_batch)]
    elems = [elementwise(prepared[0], p) for p in idx]
    for bi in range(n_batch):
        nxt_elems, queue = [], []
        if bi + 1 < n_batch:
            queue = [functools.partial(lambda row, p: nxt_elems.append(elementwise(prepared[row], p)), bi + 1, p)
                     for p in idx]

        def filler():
            if queue:
                queue.pop(0)()

        stages(bi, prepared[bi], elems, filler)
        while queue:
            queue.pop(0)()
        elems = nxt_elems


def _rwkv_time_mix(z, mu, w0, wdec, a0, wa, wg, k_k, k_a, r_k, lnx_w, lnx_b, *, batch, seq):
    L = RWKV_CHUNK
    n_chunks = seq // L
    n_pairs = RWKV_WIDTH // PAIR
    full = lambda shape: pl.BlockSpec(shape, lambda c: (0,) * len(shape))
    out = pl.pallas_call(
        _rwkv_kernel,
        grid=(n_chunks,),
        in_specs=[
            pl.BlockSpec((batch, L, RW_PAD_COLS), lambda c: (0, c, 0)),
            full((1, RW_PAD_COLS)),
            full((1, RWKV_WIDTH)), full((LANES, RWKV_WIDTH)),
            full((1, RWKV_WIDTH)), full((LANES, RWKV_WIDTH)),
            full((2 * LANES, RWKV_WIDTH)),
            full((1, RWKV_WIDTH)), full((1, RWKV_WIDTH)), full((1, RWKV_WIDTH)),
            full((1, RWKV_WIDTH)), full((1, RWKV_WIDTH)),
        ],
        out_specs=pl.BlockSpec((batch, L, RWKV_WIDTH), lambda c: (0, c, 0)),
        out_shape=jax.ShapeDtypeStruct((batch, seq, RWKV_WIDTH), BF16),
        scratch_shapes=[pltpu.VMEM((batch, n_pairs, PAIR, PAIR), F32), pltpu.VMEM((batch, 8, RW_PAD_COLS), F32)],
        compiler_params=_cparams(("arbitrary",)),
        name="rwkv7_time_mix",
    )(z.reshape(batch, seq, RW_PAD_COLS), mu, w0, wdec, a0, wa, wg, k_k, k_a, r_k, lnx_w, lnx_b)
    return out.reshape(batch * seq, RWKV_WIDTH)


def _attn_kernel(lam_ref, g_ref, q_ref, k_ref, v_ref, o_ref, vt_ref, acc_ref, qq_ref, s_ref, smax_ref, *, seq,
                 n_sub, lambda_init):
    i = pl.program_id(2)
    sb = ATTN_SUB
    two = 2 * sb
    kb = ATTN_KEYS
    steps_per_q = (n_sub * sb) // kb

    @pl.when(i == 0)
    def _():
        def transpose_block(t, carry):
            start = pl.multiple_of(t * kb, kb)
            vt_ref[t, :DIFF_HEAD, :] = v_ref[pl.ds(start, kb), :].astype(F32).T.astype(BF16)
            vt_ref[t, DIFF_HEAD:, :] = jnp.ones((ATTN_ONES_ROWS, kb), BF16)
            return carry
        lax.fori_loop(0, seq // kb, transpose_block, 0)

    lane = lax.broadcasted_iota(jnp.int32, (sb, DIFF_HEAD), 1)
    for c in range(n_sub):
        qc = q_ref[c * sb:(c + 1) * sb, :]
        zero = jnp.zeros_like(qc)
        qq_ref[c] = jnp.concatenate([jnp.where(lane < DIFF_HALF, qc, zero),
                                     jnp.where(lane < DIFF_HALF, zero, qc)], axis=0)
        acc_ref[c] = jnp.zeros((DIFF_HEAD + ATTN_ONES_ROWS, two), F32)

    key_idx = lax.broadcasted_iota(jnp.int32, (kb, two), 0)
    qry_idx = lax.broadcasted_iota(jnp.int32, (kb, two), 1) & (sb - 1)

    def scores(c, j):
        start = pl.multiple_of(j * kb, kb)
        return _dot_nt(k_ref[pl.ds(start, kb), :], qq_ref[c])

    def softmax_pv(c, j, st, st_max, m):
        m_new = jnp.maximum(m, st_max)
        alpha = jnp.exp2(m - m_new)
        pr = jnp.exp2(st - m_new)
        acc_ref[c] = alpha * acc_ref[c] + _dot(vt_ref[j], pr.astype(BF16))
        return m_new

    def stage_scores(slot, c, j):
        st = scores(c, j)
        s_ref[slot, c] = st
        smax_ref[slot, c] = jnp.max(st, axis=0, keepdims=True)

    for c in range(n_sub):
        stage_scores(0, c, 0)

    def full_steps(j0, n_steps, carry):
        ms = list(carry)
        for h in range(n_steps):
            for c in range(n_sub):
                stage_scores(1 - h % 2, c, j0 + h + 1)
            for c in range(n_sub):
                ms[c] = softmax_pv(c, j0 + h, s_ref[h % 2, c], smax_ref[h % 2, c], ms[c])
        return tuple(ms)

    init = tuple(jnp.full((1, two), NEG_INF, F32) for _ in range(n_sub))
    n_full = i * steps_per_q
    n_short = (n_full % ATTN_TRIP) // 2
    carry = lax.fori_loop(0, n_short, lambda t, cr: full_steps(2 * t, 2, cr), init)
    ms = list(lax.fori_loop(0, n_full // ATTN_TRIP,
                            lambda t, cr: full_steps(2 * n_short + ATTN_TRIP * t, ATTN_TRIP, cr), carry))

    first_diag = i * steps_per_q
    needed = lambda s, c: s * kb < (c + 1) * sb
    tail = {(s, c): scores(c, first_diag + s)
            for s in range(1, steps_per_q) for c in range(n_sub) if needed(s, c)}
    for s in range(steps_per_q):
        for c in range(n_sub):
            if not needed(s, c):
                continue
            st = s_ref[0, c] if s == 0 else tail[(s, c)]
            if (s + 1) * kb - 1 > c * sb:
                st = jnp.where(key_idx + s * kb <= qry_idx + c * sb, st, NEG_INF)
                st_max = jnp.max(st, axis=0, keepdims=True)
            else:
                st_max = smax_ref[0, c] if s == 0 else jnp.max(st, axis=0, keepdims=True)
            ms[c] = softmax_pv(c, first_diag + s, st, st_max, ms[c])

    lam_p = lam_ref[...]
    lam = (jnp.exp(jnp.sum(lam_p[0:1] * lam_p[1:2], axis=-1, keepdims=True))
           - jnp.exp(jnp.sum(lam_p[2:3] * lam_p[3:4], axis=-1, keepdims=True)) + lambda_init)
    for c in range(n_sub):
        acc = acc_ref[c]
        o = acc[:DIFF_HEAD] / acc[DIFF_HEAD:DIFF_HEAD + 1]
        d = o[:, :sb] - lam * o[:, sb:]
        ms_d = jnp.mean(d * d, axis=0, keepdims=True)
        d = d * lax.rsqrt(ms_d + SUBLN_EPS) * g_ref[...] * (1.0 - lambda_init)
        o_ref[c * sb:(c + 1) * sb, :] = d.T.astype(o_ref.dtype)


def _diff_attention(qkv, lam_params, subln_g_col, *, batch, seq, n_sub, lambda_init):
    qb = n_sub * ATTN_SUB
    assert qb % (2 * ATTN_KEYS) == 0 and seq % qb == 0
    nq = seq // qb
    h = DIFF_HEADS
    return pl.pallas_call(
        functools.partial(_attn_kernel, seq=seq, n_sub=n_sub, lambda_init=lambda_init),
        grid=(batch, h, nq),
        in_specs=[
            pl.BlockSpec((8, LANES), lambda b, hh, i: (0, 0)),
            pl.BlockSpec((DIFF_HEAD, 1), lambda b, hh, i: (0, 0)),
            pl.BlockSpec((qb, DIFF_HEAD), lambda b, hh, i: (b * nq + i, hh)),
            pl.BlockSpec((seq, DIFF_HEAD), lambda b, hh, i: (b, h + hh)),
            pl.BlockSpec((seq, DIFF_HEAD), lambda b, hh, i: (b, 2 * h + hh)),
        ],
        out_specs=pl.BlockSpec((qb, DIFF_HEAD), lambda b, hh, i: (b * nq + i, hh)),
        out_shape=jax.ShapeDtypeStruct((batch * seq, DIFF_WIDTH), BF16),
        scratch_shapes=[pltpu.VMEM((seq // ATTN_KEYS, DIFF_HEAD + ATTN_ONES_ROWS, ATTN_KEYS), BF16),
                        pltpu.VMEM((n_sub, DIFF_HEAD + ATTN_ONES_ROWS, 2 * ATTN_SUB), F32),
                        pltpu.VMEM((n_sub, 2 * ATTN_SUB, DIFF_HEAD), BF16),
                        pltpu.VMEM((2, n_sub, ATTN_KEYS, 2 * ATTN_SUB), F32),
                        pltpu.VMEM((2, n_sub, 1, 2 * ATTN_SUB), F32)],
        compiler_params=_cparams(("arbitrary", "arbitrary", "arbitrary")),
        name="diff_attention",
    )(lam_params, subln_g_col, qkv, qkv, qkv)


def _outproj2_kernel(ya_ref, yb_ref, x_ref, wa_ref, wb_ref, o_ref):
    o_ref[...] = x_ref[...] + _dot(ya_ref[...], wa_ref[...]) + _dot(yb_ref[...], wb_ref[...])


def _outproj2(ya, yb, x, wa, wb, *, tm):
    m, d = x.shape
    return pl.pallas_call(
        _outproj2_kernel,
        grid=(m // tm,),
        in_specs=[
            pl.BlockSpec((tm, ya.shape[1]), lambda i: (i, 0)),
            pl.BlockSpec((tm, yb.shape[1]), lambda i: (i, 0)),
            pl.BlockSpec((tm, d), lambda i: (i, 0)),
            pl.BlockSpec(wa.shape, lambda i: (0, 0)),
            pl.BlockSpec(wb.shape, lambda i: (0, 0)),
        ],
        out_specs=pl.BlockSpec((tm, d), lambda i: (i, 0)),
        out_shape=jax.ShapeDtypeStruct((m, d), F32),
        compiler_params=_cparams(("arbitrary",)),
        name="even_out_proj",
    )(ya, yb, x, wa, wb)


def _sgu_kernel(u_ref, v_ref, x_ref, lng_ref, lnb_ref, ws_ref, bs_ref, wo_ref, o_ref, vn_ref, gated_ref, *, tm):
    ch = GMLP_CHUNK
    v = v_ref[...].astype(F32)
    mu = jnp.mean(v, axis=-1, keepdims=True)
    vc = v - mu
    var = jnp.mean(vc * vc, axis=-1, keepdims=True)
    vn_ref[...] = (vc * lax.rsqrt(var + LN_EPS) * lng_ref[...] + lnb_ref[...]).astype(BF16)
    ti = lax.broadcasted_iota(jnp.int32, (ch, ch), 0)
    tj = lax.broadcasted_iota(jnp.int32, (ch, ch), 1)
    causal = ti >= tj
    for g in range(GMLP_GROUPS):
        cols = slice(g * LANES, (g + 1) * LANES)
        wg = jnp.where(causal, ws_ref[g], 0.0).astype(BF16)
        bias = bs_ref[:, g:g + 1]
        for c in range(tm // ch):
            rows = slice(c * ch, (c + 1) * ch)
            sv = _dot(wg, vn_ref[rows, cols]) + bias
            gated_ref[rows, cols] = (u_ref[rows, cols].astype(F32) * sv).astype(BF16)
    o_ref[...] = x_ref[...] + _dot(gated_ref[...], wo_ref[...])


def _sgu_outproj(u_v, x, ln_g, ln_b, w_s, b_s_t, w_out, *, tm):
    m, d = x.shape
    n_half = GMLP_WIDTH // d
    return pl.pallas_call(
        functools.partial(_sgu_kernel, tm=tm),
        grid=(m // tm,),
        in_specs=[
            pl.BlockSpec((tm, GMLP_WIDTH), lambda i: (i, 0)),
            pl.BlockSpec((tm, GMLP_WIDTH), lambda i: (i, n_half)),
            pl.BlockSpec((tm, d), lambda i: (i, 0)),
            pl.BlockSpec((1, GMLP_WIDTH), lambda i: (0, 0)),
            pl.BlockSpec((1, GMLP_WIDTH), lambda i: (0, 0)),
            pl.BlockSpec(w_s.shape, lambda i: (0, 0, 0)),
            pl.BlockSpec(b_s_t.shape, lambda i: (0, 0)),
            pl.BlockSpec(w_out.shape, lambda i: (0, 0)),
        ],
        out_specs=pl.BlockSpec((tm, d), lambda i: (i, 0)),
        out_shape=jax.ShapeDtypeStruct((m, d), F32),
        scratch_shapes=[pltpu.VMEM((tm, GMLP_WIDTH), BF16), pltpu.VMEM((tm, GMLP_WIDTH), BF16)],
        compiler_params=_cparams(("arbitrary",)),
        name="sgu_out_proj",
    )(u_v, u_v, x, ln_g, ln_b, w_s, b_s_t, w_out)


def _ffn_kernel(x_ref, g_ref, wg_ref, wu_ref, wd_ref, fg_ref, o_ref, h_ref, *, final):
    j = pl.program_id(1)

    @pl.when(j == 0)
    def _():
        x = x_ref[...]
        h_ref[...] = _rms_norm_rows(x, g_ref[...], NORM_EPS).astype(BF16)
        o_ref[...] = x

    h = h_ref[...]
    gate = _dot(h, wg_ref[...])
    up = _dot(h, wu_ref[...])
    act = gate * _sigmoid(gate) * up
    o_ref[...] += _dot(act.astype(BF16), wd_ref[...])

    if final:
        @pl.when(j == pl.num_programs(1) - 1)
        def _():
            o_ref[...] = _rms_norm_rows(o_ref[...], fg_ref[...], NORM_EPS)


def _ffn(x, g, w_gate, w_up, w_down, final_g, *, tm, th, final):
    m, d = x.shape
    hid = w_gate.shape[1]
    return pl.pallas_call(
        functools.partial(_ffn_kernel, final=final),
        grid=(m // tm, hid // th),
        in_specs=[
            pl.BlockSpec((tm, d), lambda i, j: (i, 0)),
            pl.BlockSpec((1, d), lambda i, j: (0, 0)),
            pl.BlockSpec((d, th), lambda i, j: (0, j)),
            pl.BlockSpec((d, th), lambda i, j: (0, j)),
            pl.BlockSpec((th, d), lambda i, j: (j, 0)),
            pl.BlockSpec((1, d), lambda i, j: (0, 0)),
        ],
        out_specs=pl.BlockSpec((tm, d), lambda i, j: (i, 0)),
        out_shape=jax.ShapeDtypeStruct((m, d), F32),
        scratch_shapes=[pltpu.VMEM((tm, d), BF16)],
        compiler_params=_cparams(("arbitrary", "arbitrary")),
        name="swiglu_ffn",
    )(x, g, w_gate, w_up, w_down, final_g)


def _pad_cols(a, width):
    return jnp.pad(a, ((0, 0), (0, width - a.shape[1])))


def _pad_rows(a, height):
    return jnp.pad(a, ((0, height - a.shape[0]), (0, 0)))


def _rwkv_col_layout(a):
    c = RWKV_WIDTH
    xw = a[:, 3 * c:3 * c + DECAY_LORA]
    xa = a[:, 3 * c + DECAY_LORA:3 * c + DECAY_LORA + AAA_LORA]
    xg = a[:, 3 * c + DECAY_LORA + AAA_LORA:]
    return jnp.concatenate(
        [a[:, :3 * c], _pad_cols(xw, LANES), _pad_cols(xa, LANES), _pad_cols(xg, 2 * LANES)], axis=1)


def _rope_tables(seq):
    inv = ROPE_THETA ** (-jnp.arange(0, DIFF_HALF, 2, dtype=F32) / DIFF_HALF)
    ang = jnp.arange(seq, dtype=F32)[:, None] * inv[None, :]
    cos, sin = jnp.cos(ang), jnp.sin(ang)
    reps = LANES // DIFF_HALF
    return (jnp.tile(jnp.concatenate([cos, cos], axis=1), (1, reps)),
            jnp.tile(jnp.concatenate([-sin, sin], axis=1), (1, reps)))


def _tile_sizes(m, seq):
    return dict(
        proj_tm=min(1024, seq), proj_tn=1024, rwkv_proj_tn=RW_PAD_COLS // 2,
        ffn_tm=min(1024, m), ffn_th=512,
        out_tm=min(512, m), sgu_tm=min(512, m),
        attn_sub=min(4, seq // ATTN_SUB),
    )


def kernel(x, mix_norm, ffn_norm, ffn_w_gate, ffn_w_up, ffn_w_down, ev_w_in, ev_mu, ev_w0, ev_w_dec_up, ev_a0,
           ev_w_a_up, ev_w_g_up, ev_k_k, ev_k_a, ev_r_k, ev_lnx_w, ev_lnx_b, ev_lam_q1, ev_lam_k1, ev_lam_q2,
           ev_lam_k2, ev_subln_g, ev_w_out, od_w_in, od_ln_g, od_ln_b, od_w_s, od_b_s, od_w_out, final_norm):
    batch, seq, d = x.shape
    m = batch * seq
    ts = _tile_sizes(m, seq)
    cos, sin = _rope_tables(seq)
    row = lambda a: a.reshape(1, -1).astype(F32)
    xf = x.reshape(m, d)
    for i in range(DEPTH):
        j = i // 2
        g_mix = row(mix_norm[i])
        if i % 2 == 0:
            lambda_init = 0.8 - 0.6 * math.exp(-0.3 * i)
            w_in = ev_w_in[j]
            w_rwkv = _rwkv_col_layout(w_in[:, :RWKV_COLS]).astype(BF16)
            w_diff = w_in[:, RWKV_COLS:].astype(BF16)
            z = _norm_matmul(xf, g_mix, w_rwkv, tm=ts["proj_tm"], tn=ts["rwkv_proj_tn"], out_dtype=BF16)
            qkv = _norm_matmul_rope(xf, g_mix, w_diff, cos, sin, tm=ts["proj_tm"], tn=ts["proj_tn"], seq=seq)
            y_a = _rwkv_time_mix(
                z, _rwkv_col_layout(row(ev_mu[j])), row(ev_w0[j]),
                _pad_rows(ev_w_dec_up[j], LANES).astype(BF16), row(ev_a0[j]),
                _pad_rows(ev_w_a_up[j], LANES).astype(BF16), _pad_rows(ev_w_g_up[j], 2 * LANES).astype(BF16),
                row(ev_k_k[j]), row(ev_k_a[j]), row(ev_r_k[j]), row(ev_lnx_w[j]), row(ev_lnx_b[j]),
                batch=batch, seq=seq)
            lam_params = _pad_rows(_pad_cols(
                jnp.stack([ev_lam_q1[j], ev_lam_k1[j], ev_lam_q2[j], ev_lam_k2[j]]).astype(F32), LANES), 8)
            y_b = _diff_attention(qkv, lam_params, ev_subln_g[j].reshape(-1, 1).astype(F32), batch=batch, seq=seq,
                                  n_sub=ts["attn_sub"], lambda_init=lambda_init)
            w_out = ev_w_out[j].astype(BF16)
            xf = _outproj2(y_a, y_b, xf, w_out[:RWKV_WIDTH], w_out[RWKV_WIDTH:], tm=ts["out_tm"])
        else:
            u_v = _norm_matmul(xf, g_mix, od_w_in[j].astype(BF16), tm=ts["proj_tm"], tn=ts["proj_tn"],
                               out_dtype=BF16, gelu=True)
            xf = _sgu_outproj(u_v, xf, row(od_ln_g[j]), row(od_ln_b[j]), od_w_s[j].astype(F32),
                              od_b_s[j].T.astype(F32), od_w_out[j].astype(BF16), tm=ts["sgu_tm"])
        xf = _ffn(xf, row(ffn_norm[i]), ffn_w_gate[i].astype(BF16), ffn_w_up[i].astype(BF16),
                  ffn_w_down[i].astype(BF16), row(final_norm), tm=ts["ffn_tm"], th=ts["ffn_th"],
                  final=(i == DEPTH - 1))
    return xf.reshape(batch, seq, d)
```

```python
import functools
import math

import jax
import jax.numpy as jnp
from jax import lax
from jax.experimental import pallas as pl
from jax.experimental.pallas import tpu as pltpu

F32 = jnp.float32
BF16 = jnp.bfloat16

D_MODEL = 2048
DEPTH = 4
RWKV_WIDTH = D_MODEL // 2
RWKV_HEAD = 64
DECAY_LORA = 64
AAA_LORA = 64
GATE_LORA = 160
RWKV_COLS = 3 * RWKV_WIDTH + DECAY_LORA + AAA_LORA + GATE_LORA
RWKV_LNX_EPS = 64e-5
DIFF_WIDTH = D_MODEL - RWKV_WIDTH
DIFF_HEAD = 128
DIFF_HALF = DIFF_HEAD // 2
DIFF_HEADS = DIFF_WIDTH // DIFF_HEAD
ROPE_THETA = 10000.0
NEG_INF = -1e30
GMLP_WIDTH = D_MODEL
GMLP_CHUNK = 128
GMLP_GROUPS = 16
FFN_HIDDEN = 5632
NORM_EPS = 1e-6
SUBLN_EPS = 1e-5
LN_EPS = 1e-5

LANES = 128
MXU_COLS = 256
VMEM_LIMIT_BYTES = 56 * 1024 * 1024

RW_XW = 3 * RWKV_WIDTH
RW_XA = RW_XW + LANES
RW_XG = RW_XA + LANES
RW_PAD_COLS = RW_XG + 2 * LANES

RWKV_CHUNK = 64
RWKV_CHUNKS_PER_STEP = 8
ATTN_SUB = 128
ATTN_KEYS = 256
ATTN_ONES_ROWS = 16
ATTN_TRIP = 8
LOG2_E = 1.4426950408889634
PAIR = 2 * RWKV_HEAD


def _cparams(semantics):
    return pltpu.CompilerParams(dimension_semantics=semantics, vmem_limit_bytes=VMEM_LIMIT_BYTES)


def _dot(a, b):
    return jnp.dot(a, b, preferred_element_type=F32)


def _dot_nt(a, b):
    return lax.dot_general(a, b, (((1,), (1,)), ((), ())), preferred_element_type=F32)


def _rms_norm_rows(x, g, eps):
    ms = jnp.mean(x * x, axis=-1, keepdims=True)
    return x * lax.rsqrt(ms + eps) * g


def _norm_matmul_kernel(x_ref, g_ref, w_ref, o_ref, h_ref, *, tn, gelu):
    @pl.when(pl.program_id(1) == 0)
    def _():
        h_ref[...] = _rms_norm_rows(x_ref[...], g_ref[...], NORM_EPS).astype(BF16)

    for n0 in range(0, tn, MXU_COLS):
        acc = _dot(h_ref[...], w_ref[:, n0:n0 + MXU_COLS])
        if gelu:
            acc = 0.5 * acc * (1.0 + lax.erf(acc * math.sqrt(0.5)))
        o_ref[:, n0:n0 + MXU_COLS] = acc.astype(o_ref.dtype)


def _norm_matmul(x, g, w, *, tm, tn, out_dtype, gelu=False):
    m, k = x.shape
    n = w.shape[1]
    return pl.pallas_call(
        functools.partial(_norm_matmul_kernel, tn=tn, gelu=gelu),
        grid=(m // tm, n // tn),
        in_specs=[
            pl.BlockSpec((tm, k), lambda i, j: (i, 0)),
            pl.BlockSpec((1, k), lambda i, j: (0, 0)),
            pl.BlockSpec((k, tn), lambda i, j: (0, j)),
        ],
        out_specs=pl.BlockSpec((tm, tn), lambda i, j: (i, j)),
        out_shape=jax.ShapeDtypeStruct((m, n), out_dtype),
        scratch_shapes=[pltpu.VMEM((tm, k), BF16)],
        compiler_params=_cparams(("arbitrary", "arbitrary")),
        name="norm_matmul_gelu" if gelu else "norm_matmul",
    )(x, g, w)


def _norm_matmul_rope_kernel(x_ref, g_ref, w_ref, cos_ref, sin_ref, o_ref, h_ref, *, tn, q_tiles, rope_tiles):
    j = pl.program_id(1)

    @pl.when(j == 0)
    def _():
        h_ref[...] = _rms_norm_rows(x_ref[...], g_ref[...], NORM_EPS).astype(BF16)

    def chunks(epilogue):
        for n0 in range(0, tn, MXU_COLS):
            acc = _dot(h_ref[...], w_ref[:, n0:n0 + MXU_COLS])
            o_ref[:, n0:n0 + MXU_COLS] = epilogue(acc).astype(o_ref.dtype)

    def rope(acc):
        reps = MXU_COLS // LANES
        c = jnp.concatenate([cos_ref[...]] * reps, axis=1)
        s = jnp.concatenate([sin_ref[...]] * reps, axis=1)
        lane = lax.broadcasted_iota(jnp.int32, acc.shape, 1)
        partner = jnp.where((lane & (DIFF_HALF // 2)) == 0,
                            pltpu.roll(acc, MXU_COLS - DIFF_HALF // 2, 1),
                            pltpu.roll(acc, DIFF_HALF // 2, 1))
        scale = jnp.where(j < q_tiles, DIFF_HALF ** -0.5 * LOG2_E, 1.0).astype(F32)
        return (acc * c + partner * s) * scale

    @pl.when(j < rope_tiles)
    def _():
        chunks(rope)

    @pl.when(j >= rope_tiles)
    def _():
        chunks(lambda acc: acc)


def _norm_matmul_rope(x, g, w, cos, sin, *, tm, tn, seq):
    m, k = x.shape
    n = w.shape[1]
    q_tiles = DIFF_WIDTH // tn
    t_tiles = seq // tm
    return pl.pallas_call(
        functools.partial(_norm_matmul_rope_kernel, tn=tn, q_tiles=q_tiles, rope_tiles=2 * q_tiles),
        grid=(m // tm, n // tn),
        in_specs=[
            pl.BlockSpec((tm, k), lambda i, j: (i, 0)),
            pl.BlockSpec((1, k), lambda i, j: (0, 0)),
            pl.BlockSpec((k, tn), lambda i, j: (0, j)),
            pl.BlockSpec((tm, LANES), lambda i, j: (i % t_tiles, 0)),
            pl.BlockSpec((tm, LANES), lambda i, j: (i % t_tiles, 0)),
        ],
        out_specs=pl.BlockSpec((tm, tn), lambda i, j: (i, j)),
        out_shape=jax.ShapeDtypeStruct((m, n), BF16),
        scratch_shapes=[pltpu.VMEM((tm, k), BF16)],
        compiler_params=_cparams(("arbitrary", "arbitrary")),
        name="norm_matmul_rope",
    )(x, g, w, cos, sin)


def _sigmoid(y):
    return 1.0 / (1.0 + jnp.exp(-y))


def _split_bf16(x, parts):
    out = []
    rem = x
    for _ in range(parts):
        p = rem.astype(BF16)
        out.append(p)
        rem = rem - p.astype(F32)
    return out


def _rwkv_kernel(z_ref, mu_ref, w0_ref, wdec_ref, a0_ref, wa_ref, wg_ref, kk_ref, ka_ref, rk_ref,
                 lw_ref, lb_ref, o_ref, s_ref, prev_ref):
    L = RWKV_CHUNK
    C = RWKV_WIDTH
    n_pairs = C // PAIR
    n_batch = z_ref.shape[0]

    @pl.when(pl.program_id(0) == 0)
    def _():
        s_ref[...] = jnp.zeros_like(s_ref)
        prev_ref[...] = jnp.zeros_like(prev_ref)

    ti = lax.broadcasted_iota(jnp.int32, (L, L), 0)
    tj = lax.broadcasted_iota(jnp.int32, (L, L), 1)
    tril_ones = jnp.where(ti >= tj, 1.0, 0.0).astype(BF16)

    def prepare(bi, row0):
        z = z_ref[bi, pl.ds(row0, L), :].astype(F32)
        row = lax.broadcasted_iota(jnp.int32, z.shape, 0)
        z_prev = jnp.where(row == 0, prev_ref[bi, 0:1, :], pltpu.roll(z, 1, 0))
        prev_ref[bi, 0:1, :] = z[L - 1:L, :]
        zs = z + (z_prev - z) * mu_ref[...]

        r = zs[:, 0:C]
        k = zs[:, C:2 * C]
        v = zs[:, 2 * C:3 * C]
        xw = zs[:, RW_XW:RW_XA]
        xa = zs[:, RW_XA:RW_XG]
        xg = zs[:, RW_XG:RW_PAD_COLS]

        y_dec = w0_ref[...] + _dot(jnp.tanh(xw).astype(BF16), wdec_ref[...])
        logd = -math.exp(-0.5) * _sigmoid(y_dec)
        a_lr = _sigmoid(a0_ref[...] + _dot(xa.astype(BF16), wa_ref[...]))
        gate = _dot(_sigmoid(xg).astype(BF16), wg_ref[...])
        kk_raw = k * kk_ref[...]
        k2 = k * (1.0 + (a_lr - 1.0) * ka_ref[...])
        rkk = r * k2 * rk_ref[...]
        cum = sum(_dot(tril_ones, p) for p in _split_bf16(logd, 2))
        return r, k2, v, logd, a_lr, gate, kk_raw, rkk, cum

    rr = lax.broadcasted_iota(jnp.int32, (PAIR, PAIR), 0)
    cc = lax.broadcasted_iota(jnp.int32, (PAIR, PAIR), 1)
    same_head = (rr // RWKV_HEAD) == (cc // RWKV_HEAD)
    strict = jnp.logical_and(same_head, rr > cc)
    incl = jnp.logical_and(same_head, rr >= cc)
    head_ones = jnp.where(same_head, 1.0, 0.0).astype(BF16)
    eye = jnp.where(rr == cc, 1.0, 0.0).astype(F32)
    first_head = lax.broadcasted_iota(jnp.int32, (L, PAIR), 1) < RWKV_HEAD

    def head_sum(x):
        return _dot(x.astype(BF16), head_ones)

    def stack(x):
        return jnp.concatenate([jnp.where(first_head, x, 0.0), jnp.where(first_head, 0.0, x)], axis=0)

    n_steps = int(math.log2(L)) - 1
    idx = range(n_pairs)
    sls = [slice(p * PAIR, (p + 1) * PAIR) for p in idx]

    def elementwise(prep, p):
        r, k2, v, logd, a_lr, _, kk_raw, rkk, cum = prep
        sl = sls[p]
        kk_p = kk_raw[:, sl]
        kk_n = kk_p * lax.rsqrt(jnp.maximum(head_sum(kk_p * kk_p), 1e-24))
        bonus = head_sum(rkk[:, sl])
        a_p = -kk_n
        b_p = kk_n * a_lr[:, sl]
        r_p, k_p = r[:, sl], k2[:, sl]
        cu = cum[:, sl]
        c_end = cu[L - 1:L, :]
        e_pos = jnp.exp(cu)
        e_neg = jnp.exp(-cu)
        e_prev = jnp.exp(cu - logd[:, sl])
        e_end = jnp.exp(c_end - cu)
        lhs = jnp.concatenate([stack(a_p * e_prev), stack(r_p * e_pos)], axis=0).astype(BF16)
        rhs = jnp.concatenate([stack(b_p * e_neg), stack(k_p * e_neg)], axis=0).astype(BF16)
        bk_end = jnp.concatenate([stack(b_p * e_end), stack(k_p * e_end)], axis=0).astype(BF16)
        return lhs, rhs, bk_end, stack(v[:, sl]), jnp.exp(c_end), bonus

    def stages(bi, row0, prep, elems, filler):
        v, gate = prep[2], prep[5]
        lhs, rhs, bk_end, v_st, d_end, bonus_s = zip(*elems)
        pm = [_dot_nt(lhs[i], rhs[i]) for i in idx]
        filler()
        s_old = [s_ref[bi, p] for p in idx]
        sh = [_dot_nt(lhs[i], s_old[i].astype(BF16)) for i in idx]
        filler()
        p_ab = [jnp.where(strict, pm[i][:PAIR, :PAIR], 0.0) for i in idx]
        p_ak = [jnp.where(strict, pm[i][:PAIR, PAIR:], 0.0).astype(BF16) for i in idx]
        p_rbk = [jnp.concatenate([jnp.where(incl, pm[i][PAIR:, :PAIR], 0.0),
                                  jnp.where(incl, pm[i][PAIR:, PAIR:], 0.0)], axis=1).astype(BF16) for i in idx]
        akv = [_dot(p_ak[i], v_st[i].astype(BF16)) for i in idx]
        filler()

        t_inv = [eye + p_ab[i] for i in idx]
        xb = [p_ab[i].astype(BF16) for i in idx]
        xb = [_dot(xb[i], xb[i]).astype(BF16) for i in idx]
        filler()
        for it in range(n_steps):
            if it + 1 < n_steps:
                both = [_dot(xb[i], jnp.concatenate([xb[i], t_inv[i].astype(BF16)], axis=1)) for i in idx]
                xb = [both[i][:, :PAIR].astype(BF16) for i in idx]
                t_inv = [t_inv[i] + both[i][:, PAIR:] for i in idx]
            else:
                t_inv = [t_inv[i] + _dot(xb[i], t_inv[i].astype(BF16)) for i in idx]
            filler()

        u_st = [_dot(t_inv[i].astype(BF16), (sh[i][:PAIR] + akv[i]).astype(BF16)) for i in idx]
        filler()
        uv = [jnp.concatenate([u_st[i], v_st[i]], axis=0) for i in idx]
        y_st = [sh[i][PAIR:] + _dot(p_rbk[i], uv[i].astype(BF16)) for i in idx]
        filler()
        s_upd = [_dot(uv[i].T.astype(BF16), bk_end[i]) for i in idx]
        for p in idx:
            s_ref[bi, p] = s_old[p] * d_end[p] + s_upd[p]

        y = [y_st[i][:L] + y_st[i][L:] for i in idx]
        mean = [head_sum(y[i]) * (1.0 / RWKV_HEAD) for i in idx]
        yc = [y[i] - mean[i] for i in idx]
        var = [head_sum(yc[i] * yc[i]) * (1.0 / RWKV_HEAD) for i in idx]
        for i in idx:
            sl = sls[i]
            yn = yc[i] * lax.rsqrt(var[i] + RWKV_LNX_EPS) * lw_ref[:, sl] + lb_ref[:, sl]
            o_ref[bi, pl.ds(row0, L), sl] = ((yn + bonus_s[i] * v[:, sl]) * gate[:, sl]).astype(o_ref.dtype)

    def chunk(ci, carry):
        row0 = pl.multiple_of(ci * L, L)
        prepared = [prepare(bi, row0) for bi in range(n_batch)]
        elems = [elementwise(prepared[0], p) for p in idx]
        for bi in range(n_batch):
            nxt_elems, queue = [], []
            if bi + 1 < n_batch:
                queue = [functools.partial(lambda row, p: nxt_elems.append(elementwise(prepared[row], p)), bi + 1, p)
                         for p in idx]

            def filler():
                if queue:
                    queue.pop(0)()

            stages(bi, row0, prepared[bi], elems, filler)
            while queue:
                queue.pop(0)()
            elems = nxt_elems
        return carry

    lax.fori_loop(0, z_ref.shape[1] // L, chunk, 0)


def _rwkv_time_mix(z, mu, w0, wdec, a0, wa, wg, k_k, k_a, r_k, lnx_w, lnx_b, *, batch, seq):
    rows = min(RWKV_CHUNK * RWKV_CHUNKS_PER_STEP, seq)
    n_pairs = RWKV_WIDTH // PAIR
    full = lambda shape: pl.BlockSpec(shape, lambda c: (0,) * len(shape))
    out = pl.pallas_call(
        _rwkv_kernel,
        grid=(seq // rows,),
        in_specs=[
            pl.BlockSpec((batch, rows, RW_PAD_COLS), lambda c: (0, c, 0)),
            full((1, RW_PAD_COLS)),
            full((1, RWKV_WIDTH)), full((LANES, RWKV_WIDTH)),
            full((1, RWKV_WIDTH)), full((LANES, RWKV_WIDTH)),
            full((2 * LANES, RWKV_WIDTH)),
            full((1, RWKV_WIDTH)), full((1, RWKV_WIDTH)), full((1, RWKV_WIDTH)),
            full((1, RWKV_WIDTH)), full((1, RWKV_WIDTH)),
        ],
        out_specs=pl.BlockSpec((batch, rows, RWKV_WIDTH), lambda c: (0, c, 0)),
        out_shape=jax.ShapeDtypeStruct((batch, seq, RWKV_WIDTH), BF16),
        scratch_shapes=[pltpu.VMEM((batch, n_pairs, PAIR, PAIR), F32), pltpu.VMEM((batch, 8, RW_PAD_COLS), F32)],
        compiler_params=_cparams(("arbitrary",)),
        name="rwkv7_time_mix",
    )(z.reshape(batch, seq, RW_PAD_COLS), mu, w0, wdec, a0, wa, wg, k_k, k_a, r_k, lnx_w, lnx_b)
    return out.reshape(batch * seq, RWKV_WIDTH)


def _attn_kernel(lam_ref, g_ref, q_ref, k_ref, v_ref, o_ref, vt_ref, acc_ref, qq_ref, s_ref, smax_ref, *, seq,
                 n_sub, lambda_init):
    sb = ATTN_SUB
    two = 2 * sb
    kb = ATTN_KEYS
    qb = n_sub * sb
    steps_per_q = qb // kb
    nq = seq // qb

    def transpose_block(t, carry):
        start = pl.multiple_of(t * kb, kb)
        vt_ref[t, :DIFF_HEAD, :] = v_ref[pl.ds(start, kb), :].astype(F32).T.astype(BF16)
        vt_ref[t, DIFF_HEAD:, :] = jnp.ones((ATTN_ONES_ROWS, kb), BF16)
        return carry
    lax.fori_loop(0, seq // kb, transpose_block, 0)

    lane = lax.broadcasted_iota(jnp.int32, (sb, DIFF_HEAD), 1)
    key_idx = lax.broadcasted_iota(jnp.int32, (kb, two), 0)
    qry_idx = lax.broadcasted_iota(jnp.int32, (kb, two), 1) & (sb - 1)
    lam_p = lam_ref[...]
    lam = (jnp.exp(jnp.sum(lam_p[0:1] * lam_p[1:2], axis=-1, keepdims=True))
           - jnp.exp(jnp.sum(lam_p[2:3] * lam_p[3:4], axis=-1, keepdims=True)) + lambda_init)

    def scores(qslot, c, j):
        start = pl.multiple_of(j * kb, kb)
        return _dot_nt(k_ref[pl.ds(start, kb), :], qq_ref[qslot, c])

    def softmax_pv(c, j, st, st_max, m):
        m_new = jnp.maximum(m, st_max)
        alpha = jnp.exp2(m - m_new)
        pr = jnp.exp2(st - m_new)
        acc_ref[c] = alpha * acc_ref[c] + _dot(vt_ref[j], pr.astype(BF16))
        return m_new

    def stage_scores(qslot, slot, c, j):
        st = scores(qslot, c, j)
        s_ref[slot, c] = st
        smax_ref[slot, c] = jnp.max(st, axis=0, keepdims=True)

    def open_block(i):
        qslot = i % 2
        for c in range(n_sub):
            row = pl.multiple_of(i * qb + c * sb, sb)
            qc = q_ref[pl.ds(row, sb), :]
            zero = jnp.zeros_like(qc)
            qq_ref[qslot, c] = jnp.concatenate([jnp.where(lane < DIFF_HALF, qc, zero),
                                                jnp.where(lane < DIFF_HALF, zero, qc)], axis=0)
        for c in range(n_sub):
            stage_scores(qslot, 0, c, 0)

    def q_block(i, carry):
        qslot = i % 2
        for c in range(n_sub):
            acc_ref[c] = jnp.zeros((DIFF_HEAD + ATTN_ONES_ROWS, two), F32)

        def full_steps(j0, n_steps, ms):
            ms = list(ms)
            for h in range(n_steps):
                for c in range(n_sub):
                    stage_scores(qslot, 1 - h % 2, c, j0 + h + 1)
                for c in range(n_sub):
                    ms[c] = softmax_pv(c, j0 + h, s_ref[h % 2, c], smax_ref[h % 2, c], ms[c])
            return tuple(ms)

        init = tuple(jnp.full((1, two), NEG_INF, F32) for _ in range(n_sub))
        n_full = i * steps_per_q
        n_short = (n_full % ATTN_TRIP) // 2
        ms = lax.fori_loop(0, n_short, lambda t, cr: full_steps(2 * t, 2, cr), init)
        ms = list(lax.fori_loop(0, n_full // ATTN_TRIP,
                                lambda t, cr: full_steps(2 * n_short + ATTN_TRIP * t, ATTN_TRIP, cr), ms))

        first_diag = n_full
        needed = lambda s, c: s * kb < (c + 1) * sb
        tail = {(s, c): scores(qslot, c, first_diag + s)
                for s in range(1, steps_per_q) for c in range(n_sub) if needed(s, c)}
        for s in range(steps_per_q):
            for c in range(n_sub):
                if not needed(s, c):
                    continue
                st = s_ref[0, c] if s == 0 else tail[(s, c)]
                if (s + 1) * kb - 1 > c * sb:
                    st = jnp.where(key_idx + s * kb <= qry_idx + c * sb, st, NEG_INF)
                    st_max = jnp.max(st, axis=0, keepdims=True)
                else:
                    st_max = smax_ref[0, c] if s == 0 else jnp.max(st, axis=0, keepdims=True)
                ms[c] = softmax_pv(c, first_diag + s, st, st_max, ms[c])

        open_block(jnp.minimum(i + 1, nq - 1))

        for c in range(n_sub):
            acc = acc_ref[c]
            o = acc[:DIFF_HEAD] / acc[DIFF_HEAD:DIFF_HEAD + 1]
            d = o[:, :sb] - lam * o[:, sb:]
            ms_d = jnp.mean(d * d, axis=0, keepdims=True)
            d = d * lax.rsqrt(ms_d + SUBLN_EPS) * g_ref[...] * (1.0 - lambda_init)
            row = pl.multiple_of(i * qb + c * sb, sb)
            o_ref[pl.ds(row, sb), :] = d.T.astype(o_ref.dtype)
        return carry

    open_block(0)
    lax.fori_loop(0, nq, q_block, 0)


def _diff_attention(qkv, lam_params, subln_g_col, *, batch, seq, n_sub, lambda_init):
    qb = n_sub * ATTN_SUB
    assert qb % (2 * ATTN_KEYS) == 0 and seq % qb == 0
    h = DIFF_HEADS
    head_cols = lambda off: pl.BlockSpec((seq, DIFF_HEAD), lambda b, hh: (b, off + hh))
    return pl.pallas_call(
        functools.partial(_attn_kernel, seq=seq, n_sub=n_sub, lambda_init=lambda_init),
        grid=(batch, h),
        in_specs=[
            pl.BlockSpec((8, LANES), lambda b, hh: (0, 0)),
            pl.BlockSpec((DIFF_HEAD, 1), lambda b, hh: (0, 0)),
            head_cols(0), head_cols(h), head_cols(2 * h),
        ],
        out_specs=head_cols(0),
        out_shape=jax.ShapeDtypeStruct((batch * seq, DIFF_WIDTH), BF16),
        scratch_shapes=[pltpu.VMEM((seq // ATTN_KEYS, DIFF_HEAD + ATTN_ONES_ROWS, ATTN_KEYS), BF16),
                        pltpu.VMEM((n_sub, DIFF_HEAD + ATTN_ONES_ROWS, 2 * ATTN_SUB), F32),
                        pltpu.VMEM((2, n_sub, 2 * ATTN_SUB, DIFF_HEAD), BF16),
                        pltpu.VMEM((2, n_sub, ATTN_KEYS, 2 * ATTN_SUB), F32),
                        pltpu.VMEM((2, n_sub, 1, 2 * ATTN_SUB), F32)],
        compiler_params=_cparams(("arbitrary", "arbitrary")),
        name="diff_attention",
    )(lam_params, subln_g_col, qkv, qkv, qkv)


def _outproj2_kernel(ya_ref, yb_ref, x_ref, wa_ref, wb_ref, o_ref):
    o_ref[...] = x_ref[...] + _dot(ya_ref[...], wa_ref[...]) + _dot(yb_ref[...], wb_ref[...])


def _outproj2(ya, yb, x, wa, wb, *, tm):
    m, d = x.shape
    return pl.pallas_call(
        _outproj2_kernel,
        grid=(m // tm,),
        in_specs=[
            pl.BlockSpec((tm, ya.shape[1]), lambda i: (i, 0)),
            pl.BlockSpec((tm, yb.shape[1]), lambda i: (i, 0)),
            pl.BlockSpec((tm, d), lambda i: (i, 0)),
            pl.BlockSpec(wa.shape, lambda i: (0, 0)),
            pl.BlockSpec(wb.shape, lambda i: (0, 0)),
        ],
        out_specs=pl.BlockSpec((tm, d), lambda i: (i, 0)),
        out_shape=jax.ShapeDtypeStruct((m, d), F32),
        compiler_params=_cparams(("arbitrary",)),
        name="even_out_proj",
    )(ya, yb, x, wa, wb)


def _sgu_kernel(u_ref, v_ref, x_ref, lng_ref, lnb_ref, ws_ref, bs_ref, wo_ref, o_ref, vn_ref, gated_ref, *, tm):
    ch = GMLP_CHUNK
    v = v_ref[...].astype(F32)
    mu = jnp.mean(v, axis=-1, keepdims=True)
    vc = v - mu
    var = jnp.mean(vc * vc, axis=-1, keepdims=True)
    vn_ref[...] = (vc * lax.rsqrt(var + LN_EPS) * lng_ref[...] + lnb_ref[...]).astype(BF16)
    ti = lax.broadcasted_iota(jnp.int32, (ch, ch), 0)
    tj = lax.broadcasted_iota(jnp.int32, (ch, ch), 1)
    causal = ti >= tj
    for g in range(GMLP_GROUPS):
        cols = slice(g * LANES, (g + 1) * LANES)
        wg = jnp.where(causal, ws_ref[g], 0.0).astype(BF16)
        bias = bs_ref[:, g:g + 1]
        for c in range(tm // ch):
            rows = slice(c * ch, (c + 1) * ch)
            sv = _dot(wg, vn_ref[rows, cols]) + bias
            gated_ref[rows, cols] = (u_ref[rows, cols].astype(F32) * sv).astype(BF16)
    o_ref[...] = x_ref[...] + _dot(gated_ref[...], wo_ref[...])


def _sgu_outproj(u_v, x, ln_g, ln_b, w_s, b_s_t, w_out, *, tm):
    m, d = x.shape
    n_half = GMLP_WIDTH // d
    return pl.pallas_call(
        functools.partial(_sgu_kernel, tm=tm),
        grid=(m // tm,),
        in_specs=[
            pl.BlockSpec((tm, GMLP_WIDTH), lambda i: (i, 0)),
            pl.BlockSpec((tm, GMLP_WIDTH), lambda i: (i, n_half)),
            pl.BlockSpec((tm, d), lambda i: (i, 0)),
            pl.BlockSpec((1, GMLP_WIDTH), lambda i: (0, 0)),
            pl.BlockSpec((1, GMLP_WIDTH), lambda i: (0, 0)),
            pl.BlockSpec(w_s.shape, lambda i: (0, 0, 0)),
            pl.BlockSpec(b_s_t.shape, lambda i: (0, 0)),
            pl.BlockSpec(w_out.shape, lambda i: (0, 0)),
        ],
        out_specs=pl.BlockSpec((tm, d), lambda i: (i, 0)),
        out_shape=jax.ShapeDtypeStruct((m, d), F32),
        scratch_shapes=[pltpu.VMEM((tm, GMLP_WIDTH), BF16), pltpu.VMEM((tm, GMLP_WIDTH), BF16)],
        compiler_params=_cparams(("arbitrary",)),
        name="sgu_out_proj",
    )(u_v, u_v, x, ln_g, ln_b, w_s, b_s_t, w_out)


def _ffn_kernel(x_ref, g_ref, wg_ref, wu_ref, wd_ref, fg_ref, o_ref, h_ref, *, final):
    j = pl.program_id(1)

    @pl.when(j == 0)
    def _():
        x = x_ref[...]
        h_ref[...] = _rms_norm_rows(x, g_ref[...], NORM_EPS).astype(BF16)
        o_ref[...] = x

    h = h_ref[...]
    gate = _dot(h, wg_ref[...])
    up = _dot(h, wu_ref[...])
    act = gate * _sigmoid(gate) * up
    o_ref[...] += _dot(act.astype(BF16), wd_ref[...])

    if final:
        @pl.when(j == pl.num_programs(1) - 1)
        def _():
            o_ref[...] = _rms_norm_rows(o_ref[...], fg_ref[...], NORM_EPS)


def _ffn(x, g, w_gate, w_up, w_down, final_g, *, tm, th, final):
    m, d = x.shape
    hid = w_gate.shape[1]
    return pl.pallas_call(
        functools.partial(_ffn_kernel, final=final),
        grid=(m // tm, hid // th),
        in_specs=[
            pl.BlockSpec((tm, d), lambda i, j: (i, 0)),
            pl.BlockSpec((1, d), lambda i, j: (0, 0)),
            pl.BlockSpec((d, th), lambda i, j: (0, j)),
            pl.BlockSpec((d, th), lambda i, j: (0, j)),
            pl.BlockSpec((th, d), lambda i, j: (j, 0)),
            pl.BlockSpec((1, d), lambda i, j: (0, 0)),
        ],
        out_specs=pl.BlockSpec((tm, d), lambda i, j: (i, 0)),
        out_shape=jax.ShapeDtypeStruct((m, d), F32),
        scratch_shapes=[pltpu.VMEM((tm, d), BF16)],
        compiler_params=_cparams(("arbitrary", "arbitrary")),
        name="swiglu_ffn",
    )(x, g, w_gate, w_up, w_down, final_g)


def _pad_cols(a, width):
    return jnp.pad(a, ((0, 0), (0, width - a.shape[1])))


def _pad_rows(a, height):
    return jnp.pad(a, ((0, height - a.shape[0]), (0, 0)))


def _rwkv_col_layout(a):
    c = RWKV_WIDTH
    xw = a[:, 3 * c:3 * c + DECAY_LORA]
    xa = a[:, 3 * c + DECAY_LORA:3 * c + DECAY_LORA + AAA_LORA]
    xg = a[:, 3 * c + DECAY_LORA + AAA_LORA:]
    return jnp.concatenate(
        [a[:, :3 * c], _pad_cols(xw, LANES), _pad_cols(xa, LANES), _pad_cols(xg, 2 * LANES)], axis=1)


def _rope_tables(seq):
    inv = ROPE_THETA ** (-jnp.arange(0, DIFF_HALF, 2, dtype=F32) / DIFF_HALF)
    ang = jnp.arange(seq, dtype=F32)[:, None] * inv[None, :]
    cos, sin = jnp.cos(ang), jnp.sin(ang)
    reps = LANES // DIFF_HALF
    return (jnp.tile(jnp.concatenate([cos, cos], axis=1), (1, reps)),
            jnp.tile(jnp.concatenate([-sin, sin], axis=1), (1, reps)))


def _tile_sizes(m, seq):
    return dict(
        proj_tm=min(1024, seq), proj_tn=1024, rwkv_proj_tn=RW_PAD_COLS // 2,
        ffn_tm=min(1024, m), ffn_th=512,
        out_tm=min(512, m), sgu_tm=min(512, m),
        attn_sub=min(4, seq // ATTN_SUB),
    )


def kernel(x, mix_norm, ffn_norm, ffn_w_gate, ffn_w_up, ffn_w_down, ev_w_in, ev_mu, ev_w0, ev_w_dec_up, ev_a0,
           ev_w_a_up, ev_w_g_up, ev_k_k, ev_k_a, ev_r_k, ev_lnx_w, ev_lnx_b, ev_lam_q1, ev_lam_k1, ev_lam_q2,
           ev_lam_k2, ev_subln_g, ev_w_out, od_w_in, od_ln_g, od_ln_b, od_w_s, od_b_s, od_w_out, final_norm):
    batch, seq, d = x.shape
    m = batch * seq
    ts = _tile_sizes(m, seq)
    cos, sin = _rope_tables(seq)
    row = lambda a: a.reshape(1, -1).astype(F32)
    xf = x.reshape(m, d)
    for i in range(DEPTH):
        j = i // 2
        g_mix = row(mix_norm[i])
        if i % 2 == 0:
            lambda_init = 0.8 - 0.6 * math.exp(-0.3 * i)
            w_in = ev_w_in[j]
            w_rwkv = _rwkv_col_layout(w_in[:, :RWKV_COLS]).astype(BF16)
            w_diff = w_in[:, RWKV_COLS:].astype(BF16)
            z = _norm_matmul(xf, g_mix, w_rwkv, tm=ts["proj_tm"], tn=ts["rwkv_proj_tn"], out_dtype=BF16)
            qkv = _norm_matmul_rope(xf, g_mix, w_diff, cos, sin, tm=ts["proj_tm"], tn=ts["proj_tn"], seq=seq)
            y_a = _rwkv_time_mix(
                z, _rwkv_col_layout(row(ev_mu[j])), row(ev_w0[j]),
                _pad_rows(ev_w_dec_up[j], LANES).astype(BF16), row(ev_a0[j]),
                _pad_rows(ev_w_a_up[j], LANES).astype(BF16), _pad_rows(ev_w_g_up[j], 2 * LANES).astype(BF16),
                row(ev_k_k[j]), row(ev_k_a[j]), row(ev_r_k[j]), row(ev_lnx_w[j]), row(ev_lnx_b[j]),
                batch=batch, seq=seq)
            lam_params = _pad_rows(_pad_cols(
                jnp.stack([ev_lam_q1[j], ev_lam_k1[j], ev_lam_q2[j], ev_lam_k2[j]]).astype(F32), LANES), 8)
            y_b = _diff_attention(qkv, lam_params, ev_subln_g[j].reshape(-1, 1).astype(F32), batch=batch, seq=seq,
                                  n_sub=ts["attn_sub"], lambda_init=lambda_init)
            w_out = ev_w_out[j].astype(BF16)
            xf = _outproj2(y_a, y_b, xf, w_out[:RWKV_WIDTH], w_out[RWKV_WIDTH:], tm=ts["out_tm"])
        else:
            u_v = _norm_matmul(xf, g_mix, od_w_in[j].astype(BF16), tm=ts["proj_tm"], tn=ts["proj_tn"],
                               out_dtype=BF16, gelu=True)
            xf = _sgu_outproj(u_v, xf, row(od_ln_g[j]), row(od_ln_b[j]), od_w_s[j].astype(F32),
                              od_b_s[j].T.astype(F32), od_w_out[j].astype(BF16), tm=ts["sgu_tm"])
        xf = _ffn(xf, row(ffn_norm[i]), ffn_w_gate[i].astype(BF16), ffn_w_up[i].astype(BF16),
                  ffn_w_down[i].astype(BF16), row(final_norm), tm=ts["ffn_tm"], th=ts["ffn_th"],
                  final=(i == DEPTH - 1))
    return xf.reshape(batch, seq, d)
```

```python
import functools
import math

import jax
import jax.numpy as jnp
from jax import lax
from jax.experimental import pallas as pl
from jax.experimental.pallas import tpu as pltpu

F32 = jnp.float32
BF16 = jnp.bfloat16

D_MODEL = 2048
DEPTH = 4
RWKV_WIDTH = D_MODEL // 2
RWKV_HEAD = 64
DECAY_LORA = 64
AAA_LORA = 64
GATE_LORA = 160
RWKV_COLS = 3 * RWKV_WIDTH + DECAY_LORA + AAA_LORA + GATE_LORA
RWKV_LNX_EPS = 64e-5
DIFF_WIDTH = D_MODEL - RWKV_WIDTH
DIFF_HEAD = 128
DIFF_HALF = DIFF_HEAD // 2
DIFF_HEADS = DIFF_WIDTH // DIFF_HEAD
ROPE_THETA = 10000.0
NEG_INF = -1e30
GMLP_WIDTH = D_MODEL
GMLP_CHUNK = 128
GMLP_GROUPS = 16
FFN_HIDDEN = 5632
NORM_EPS = 1e-6
SUBLN_EPS = 1e-5
LN_EPS = 1e-5

LANES = 128
MXU_COLS = 256
PROLOGUE_PARTS = 4
VMEM_LIMIT_BYTES = 60 * 1024 * 1024

RW_XW = 3 * RWKV_WIDTH
RW_XA = RW_XW + LANES
RW_XG = RW_XA + LANES
RW_PAD_COLS = RW_XG + 2 * LANES

RWKV_CHUNK = 64
RWKV_CHUNKS_PER_STEP = 8
ATTN_SUB = 128
ATTN_KEYS = 256
ATTN_ONES_ROWS = 16
ATTN_TRIP = 8
LOG2_E = 1.4426950408889634
PAIR = 2 * RWKV_HEAD


def _cparams(semantics):
    return pltpu.CompilerParams(dimension_semantics=semantics, vmem_limit_bytes=VMEM_LIMIT_BYTES)


def _dot(a, b):
    return jnp.dot(a, b, preferred_element_type=F32)


def _dot_nt(a, b):
    return lax.dot_general(a, b, (((1,), (1,)), ((), ())), preferred_element_type=F32)


def _rms_norm_rows(x, g, eps):
    ms = jnp.mean(x * x, axis=-1, keepdims=True)
    return x * lax.rsqrt(ms + eps) * g


def _row_parts(tm):
    part = tm // PROLOGUE_PARTS
    return [slice(r0, r0 + part) for r0 in range(0, tm, part)]


def _norm_matmul_kernel(x_ref, g_ref, w_ref, o_ref, h_ref, *, tn, gelu):
    j = pl.program_id(1)

    def columns(rows):
        for n0 in range(0, tn, MXU_COLS):
            acc = _dot(h_ref[rows, :], w_ref[:, n0:n0 + MXU_COLS])
            if gelu:
                acc = 0.5 * acc * (1.0 + lax.erf(acc * math.sqrt(0.5)))
            o_ref[rows, n0:n0 + MXU_COLS] = acc.astype(o_ref.dtype)

    @pl.when(j == 0)
    def _():
        for rows in _row_parts(x_ref.shape[0]):
            h_ref[rows, :] = _rms_norm_rows(x_ref[rows, :], g_ref[...], NORM_EPS).astype(BF16)
            columns(rows)

    @pl.when(j > 0)
    def _():
        columns(slice(None))


def _norm_matmul(x, g, w, *, tm, tn, out_dtype, gelu=False):
    m, k = x.shape
    n = w.shape[1]
    return pl.pallas_call(
        functools.partial(_norm_matmul_kernel, tn=tn, gelu=gelu),
        grid=(m // tm, n // tn),
        in_specs=[
            pl.BlockSpec((tm, k), lambda i, j: (i, 0)),
            pl.BlockSpec((1, k), lambda i, j: (0, 0)),
            pl.BlockSpec((k, tn), lambda i, j: (0, j)),
        ],
        out_specs=pl.BlockSpec((tm, tn), lambda i, j: (i, j)),
        out_shape=jax.ShapeDtypeStruct((m, n), out_dtype),
        scratch_shapes=[pltpu.VMEM((tm, k), BF16)],
        compiler_params=_cparams(("arbitrary", "arbitrary")),
        name="norm_matmul_gelu" if gelu else "norm_matmul",
    )(x, g, w)


def _norm_matmul_rope_kernel(x_ref, g_ref, w_ref, cos_ref, sin_ref, o_ref, h_ref, *, tn, q_tiles, rope_tiles):
    j = pl.program_id(1)

    def columns(rows, epilogue):
        for n0 in range(0, tn, MXU_COLS):
            acc = _dot(h_ref[rows, :], w_ref[:, n0:n0 + MXU_COLS])
            o_ref[rows, n0:n0 + MXU_COLS] = epilogue(rows, acc).astype(o_ref.dtype)

    def rope(rows, acc):
        reps = MXU_COLS // LANES
        c = jnp.concatenate([cos_ref[rows, :]] * reps, axis=1)
        s = jnp.concatenate([sin_ref[rows, :]] * reps, axis=1)
        lane = lax.broadcasted_iota(jnp.int32, acc.shape, 1)
        partner = jnp.where((lane & (DIFF_HALF // 2)) == 0,
                            pltpu.roll(acc, MXU_COLS - DIFF_HALF // 2, 1),
                            pltpu.roll(acc, DIFF_HALF // 2, 1))
        scale = jnp.where(j < q_tiles, DIFF_HALF ** -0.5 * LOG2_E, 1.0).astype(F32)
        return (acc * c + partner * s) * scale

    @pl.when(j == 0)
    def _():
        for rows in _row_parts(x_ref.shape[0]):
            h_ref[rows, :] = _rms_norm_rows(x_ref[rows, :], g_ref[...], NORM_EPS).astype(BF16)
            columns(rows, rope)

    @pl.when(jnp.logical_and(j > 0, j < rope_tiles))
    def _():
        columns(slice(None), rope)

    @pl.when(j >= rope_tiles)
    def _():
        columns(slice(None), lambda rows, acc: acc)


def _norm_matmul_rope(x, g, w, cos, sin, *, tm, tn, seq):
    m, k = x.shape
    n = w.shape[1]
    q_tiles = DIFF_WIDTH // tn
    t_tiles = seq // tm
    return pl.pallas_call(
        functools.partial(_norm_matmul_rope_kernel, tn=tn, q_tiles=q_tiles, rope_tiles=2 * q_tiles),
        grid=(m // tm, n // tn),
        in_specs=[
            pl.BlockSpec((tm, k), lambda i, j: (i, 0)),
            pl.BlockSpec((1, k), lambda i, j: (0, 0)),
            pl.BlockSpec((k, tn), lambda i, j: (0, j)),
            pl.BlockSpec((tm, LANES), lambda i, j: (i % t_tiles, 0)),
            pl.BlockSpec((tm, LANES), lambda i, j: (i % t_tiles, 0)),
        ],
        out_specs=pl.BlockSpec((tm, tn), lambda i, j: (i, j)),
        out_shape=jax.ShapeDtypeStruct((m, n), BF16),
        scratch_shapes=[pltpu.VMEM((tm, k), BF16)],
        compiler_params=_cparams(("arbitrary", "arbitrary")),
        name="norm_matmul_rope",
    )(x, g, w, cos, sin)


def _sigmoid(y):
    return 1.0 / (1.0 + jnp.exp(-y))


def _split_bf16(x, parts):
    out = []
    rem = x
    for _ in range(parts):
        p = rem.astype(BF16)
        out.append(p)
        rem = rem - p.astype(F32)
    return out


def _rwkv_kernel(z_ref, mu_ref, w0_ref, wdec_ref, a0_ref, wa_ref, wg_ref, kk_ref, ka_ref, rk_ref,
                 lw_ref, lb_ref, o_ref, s_ref, prev_ref):
    L = RWKV_CHUNK
    C = RWKV_WIDTH
    n_pairs = C // PAIR
    n_batch = z_ref.shape[0]

    @pl.when(pl.program_id(0) == 0)
    def _():
        s_ref[...] = jnp.zeros_like(s_ref)
        prev_ref[...] = jnp.zeros_like(prev_ref)

    ti = lax.broadcasted_iota(jnp.int32, (L, L), 0)
    tj = lax.broadcasted_iota(jnp.int32, (L, L), 1)
    tril_ones = jnp.where(ti >= tj, 1.0, 0.0).astype(BF16)

    def prepare(bi, row0):
        z = z_ref[bi, pl.ds(row0, L), :].astype(F32)
        row = lax.broadcasted_iota(jnp.int32, z.shape, 0)
        z_prev = jnp.where(row == 0, prev_ref[bi, 0:1, :], pltpu.roll(z, 1, 0))
        prev_ref[bi, 0:1, :] = z[L - 1:L, :]
        zs = z + (z_prev - z) * mu_ref[...]

        r = zs[:, 0:C]
        k = zs[:, C:2 * C]
        v = zs[:, 2 * C:3 * C]
        xw = zs[:, RW_XW:RW_XA]
        xa = zs[:, RW_XA:RW_XG]
        xg = zs[:, RW_XG:RW_PAD_COLS]

        y_dec = w0_ref[...] + _dot(jnp.tanh(xw).astype(BF16), wdec_ref[...])
        logd = -math.exp(-0.5) * _sigmoid(y_dec)
        a_lr = _sigmoid(a0_ref[...] + _dot(xa.astype(BF16), wa_ref[...]))
        gate = _dot(_sigmoid(xg).astype(BF16), wg_ref[...])
        kk_raw = k * kk_ref[...]
        k2 = k * (1.0 + (a_lr - 1.0) * ka_ref[...])
        rkk = r * k2 * rk_ref[...]
        cum = sum(_dot(tril_ones, p) for p in _split_bf16(logd, 2))
        return r, k2, v, logd, a_lr, gate, kk_raw, rkk, cum

    rr = lax.broadcasted_iota(jnp.int32, (PAIR, PAIR), 0)
    cc = lax.broadcasted_iota(jnp.int32, (PAIR, PAIR), 1)
    same_head = (rr // RWKV_HEAD) == (cc // RWKV_HEAD)
    strict = jnp.logical_and(same_head, rr > cc)
    incl = jnp.logical_and(same_head, rr >= cc)
    head_ones = jnp.where(same_head, 1.0, 0.0).astype(BF16)
    eye = jnp.where(rr == cc, 1.0, 0.0).astype(F32)
    first_head = lax.broadcasted_iota(jnp.int32, (L, PAIR), 1) < RWKV_HEAD

    def head_sum(x):
        return _dot(x.astype(BF16), head_ones)

    def stack(x):
        return jnp.concatenate([jnp.where(first_head, x, 0.0), jnp.where(first_head, 0.0, x)], axis=0)

    n_steps = int(math.log2(L)) - 1
    idx = range(n_pairs)
    sls = [slice(p * PAIR, (p + 1) * PAIR) for p in idx]

    def elementwise(prep, p):
        r, k2, v, logd, a_lr, _, kk_raw, rkk, cum = prep
        sl = sls[p]
        kk_p = kk_raw[:, sl]
        kk_n = kk_p * lax.rsqrt(jnp.maximum(head_sum(kk_p * kk_p), 1e-24))
        bonus = head_sum(rkk[:, sl])
        a_p = -kk_n
        b_p = kk_n * a_lr[:, sl]
        r_p, k_p = r[:, sl], k2[:, sl]
        cu = cum[:, sl]
        c_end = cu[L - 1:L, :]
        e_pos = jnp.exp(cu)
        e_neg = jnp.exp(-cu)
        e_prev = jnp.exp(cu - logd[:, sl])
        e_end = jnp.exp(c_end - cu)
        lhs = jnp.concatenate([stack(a_p * e_prev), stack(r_p * e_pos)], axis=0).astype(BF16)
        rhs = jnp.concatenate([stack(b_p * e_neg), stack(k_p * e_neg)], axis=0).astype(BF16)
        bk_end = jnp.concatenate([stack(b_p * e_end), stack(k_p * e_end)], axis=0).astype(BF16)
        return lhs, rhs, bk_end, stack(v[:, sl]), jnp.exp(c_end), bonus

    def stages(bi, row0, prep, elems, filler):
        v, gate = prep[2], prep[5]
        lhs, rhs, bk_end, v_st, d_end, bonus_s = zip(*elems)
        pm = [_dot_nt(lhs[i], rhs[i]) for i in idx]
        filler()
        s_old = [s_ref[bi, p] for p in idx]
        sh = [_dot_nt(lhs[i], s_old[i].astype(BF16)) for i in idx]
        filler()
        p_ab = [jnp.where(strict, pm[i][:PAIR, :PAIR], 0.0) for i in idx]
        p_ak = [jnp.where(strict, pm[i][:PAIR, PAIR:], 0.0).astype(BF16) for i in idx]
        p_rbk = [jnp.concatenate([jnp.where(incl, pm[i][PAIR:, :PAIR], 0.0),
                                  jnp.where(incl, pm[i][PAIR:, PAIR:], 0.0)], axis=1).astype(BF16) for i in idx]
        akv = [_dot(p_ak[i], v_st[i].astype(BF16)) for i in idx]
        filler()

        t_inv = [eye + p_ab[i] for i in idx]
        xb = [p_ab[i].astype(BF16) for i in idx]
        xb = [_dot(xb[i], xb[i]).astype(BF16) for i in idx]
        filler()
        for it in range(n_steps):
            if it + 1 < n_steps:
                both = [_dot(xb[i], jnp.concatenate([xb[i], t_inv[i].astype(BF16)], axis=1)) for i in idx]
                xb = [both[i][:, :PAIR].astype(BF16) for i in idx]
                t_inv = [t_inv[i] + both[i][:, PAIR:] for i in idx]
            else:
                t_inv = [t_inv[i] + _dot(xb[i], t_inv[i].astype(BF16)) for i in idx]
            filler()

        u_st = [_dot(t_inv[i].astype(BF16), (sh[i][:PAIR] + akv[i]).astype(BF16)) for i in idx]
        filler()
        uv = [jnp.concatenate([u_st[i], v_st[i]], axis=0) for i in idx]
        y_st = [sh[i][PAIR:] + _dot(p_rbk[i], uv[i].astype(BF16)) for i in idx]
        filler()
        s_upd = [_dot(uv[i].T.astype(BF16), bk_end[i]) for i in idx]
        for p in idx:
            s_ref[bi, p] = s_old[p] * d_end[p] + s_upd[p]

        y = [y_st[i][:L] + y_st[i][L:] for i in idx]
        mean = [head_sum(y[i]) * (1.0 / RWKV_HEAD) for i in idx]
        yc = [y[i] - mean[i] for i in idx]
        var = [head_sum(yc[i] * yc[i]) * (1.0 / RWKV_HEAD) for i in idx]
        for i in idx:
            sl = sls[i]
            yn = yc[i] * lax.rsqrt(var[i] + RWKV_LNX_EPS) * lw_ref[:, sl] + lb_ref[:, sl]
            o_ref[bi, pl.ds(row0, L), sl] = ((yn + bonus_s[i] * v[:, sl]) * gate[:, sl]).astype(o_ref.dtype)

    def chunk(ci, carry):
        row0 = pl.multiple_of(ci * L, L)
        prepared = [prepare(bi, row0) for bi in range(n_batch)]
        elems = [elementwise(prepared[0], p) for p in idx]
        for bi in range(n_batch):
            nxt_elems, queue = [], []
            if bi + 1 < n_batch:
                queue = [functools.partial(lambda row, p: nxt_elems.append(elementwise(prepared[row], p)), bi + 1, p)
                         for p in idx]

            def filler():
                if queue:
                    queue.pop(0)()

            stages(bi, row0, prepared[bi], elems, filler)
            while queue:
                queue.pop(0)()
            elems = nxt_elems
        return carry

    lax.fori_loop(0, z_ref.shape[1] // L, chunk, 0)


def _rwkv_time_mix(z, mu, w0, wdec, a0, wa, wg, k_k, k_a, r_k, lnx_w, lnx_b, *, batch, seq):
    rows = min(RWKV_CHUNK * RWKV_CHUNKS_PER_STEP, seq)
    n_pairs = RWKV_WIDTH // PAIR
    full = lambda shape: pl.BlockSpec(shape, lambda c: (0,) * len(shape))
    out = pl.pallas_call(
        _rwkv_kernel,
        grid=(seq // rows,),
        in_specs=[
            pl.BlockSpec((batch, rows, RW_PAD_COLS), lambda c: (0, c, 0)),
            full((1, RW_PAD_COLS)),
            full((1, RWKV_WIDTH)), full((LANES, RWKV_WIDTH)),
            full((1, RWKV_WIDTH)), full((LANES, RWKV_WIDTH)),
            full((2 * LANES, RWKV_WIDTH)),
            full((1, RWKV_WIDTH)), full((1, RWKV_WIDTH)), full((1, RWKV_WIDTH)),
            full((1, RWKV_WIDTH)), full((1, RWKV_WIDTH)),
        ],
        out_specs=pl.BlockSpec((batch, rows, RWKV_WIDTH), lambda c: (0, c, 0)),
        out_shape=jax.ShapeDtypeStruct((batch, seq, RWKV_WIDTH), BF16),
        scratch_shapes=[pltpu.VMEM((batch, n_pairs, PAIR, PAIR), F32), pltpu.VMEM((batch, 8, RW_PAD_COLS), F32)],
        compiler_params=_cparams(("arbitrary",)),
        name="rwkv7_time_mix",
    )(z.reshape(batch, seq, RW_PAD_COLS), mu, w0, wdec, a0, wa, wg, k_k, k_a, r_k, lnx_w, lnx_b)
    return out.reshape(batch * seq, RWKV_WIDTH)


def _attn_kernel(lam_ref, g_ref, q_ref, k_ref, v_ref, o_ref, vt_ref, acc_ref, qq_ref, s_ref, smax_ref, *, seq,
                 n_sub, lambda_init):
    sb = ATTN_SUB
    two = 2 * sb
    kb = ATTN_KEYS
    qb = n_sub * sb
    steps_per_q = qb // kb
    nq = seq // qb

    def transpose_block(t, carry):
        start = pl.multiple_of(t * kb, kb)
        vt_ref[t, :DIFF_HEAD, :] = v_ref[pl.ds(start, kb), :].astype(F32).T.astype(BF16)
        vt_ref[t, DIFF_HEAD:, :] = jnp.ones((ATTN_ONES_ROWS, kb), BF16)
        return carry
    lax.fori_loop(0, seq // kb, transpose_block, 0)

    lane = lax.broadcasted_iota(jnp.int32, (sb, DIFF_HEAD), 1)
    key_idx = lax.broadcasted_iota(jnp.int32, (kb, two), 0)
    qry_idx = lax.broadcasted_iota(jnp.int32, (kb, two), 1) & (sb - 1)
    lam_p = lam_ref[...]
    lam = (jnp.exp(jnp.sum(lam_p[0:1] * lam_p[1:2], axis=-1, keepdims=True))
           - jnp.exp(jnp.sum(lam_p[2:3] * lam_p[3:4], axis=-1, keepdims=True)) + lambda_init)

    def scores(qslot, c, j):
        start = pl.multiple_of(j * kb, kb)
        return _dot_nt(k_ref[pl.ds(start, kb), :], qq_ref[qslot, c])

    def softmax_pv(c, j, st, st_max, m):
        m_new = jnp.maximum(m, st_max)
        alpha = jnp.exp2(m - m_new)
        pr = jnp.exp2(st - m_new)
        acc_ref[c] = alpha * acc_ref[c] + _dot(vt_ref[j], pr.astype(BF16))
        return m_new

    def stage_scores(qslot, slot, c, j):
        st = scores(qslot, c, j)
        s_ref[slot, c] = st
        smax_ref[slot, c] = jnp.max(st, axis=0, keepdims=True)

    def open_block(i):
        qslot = i % 2
        for c in range(n_sub):
            row = pl.multiple_of(i * qb + c * sb, sb)
            qc = q_ref[pl.ds(row, sb), :]
            zero = jnp.zeros_like(qc)
            qq_ref[qslot, c] = jnp.concatenate([jnp.where(lane < DIFF_HALF, qc, zero),
                                                jnp.where(lane < DIFF_HALF, zero, qc)], axis=0)
        for c in range(n_sub):
            stage_scores(qslot, 0, c, 0)

    def q_block(i, carry):
        qslot = i % 2
        for c in range(n_sub):
            acc_ref[c] = jnp.zeros((DIFF_HEAD + ATTN_ONES_ROWS, two), F32)

        def full_steps(j0, n_steps, ms):
            ms = list(ms)
            for h in range(n_steps):
                for c in range(n_sub):
                    stage_scores(qslot, 1 - h % 2, c, j0 + h + 1)
                for c in range(n_sub):
                    ms[c] = softmax_pv(c, j0 + h, s_ref[h % 2, c], smax_ref[h % 2, c], ms[c])
            return tuple(ms)

        init = tuple(jnp.full((1, two), NEG_INF, F32) for _ in range(n_sub))
        n_full = i * steps_per_q
        n_short = (n_full % ATTN_TRIP) // 2
        ms = lax.fori_loop(0, n_short, lambda t, cr: full_steps(2 * t, 2, cr), init)
        ms = list(lax.fori_loop(0, n_full // ATTN_TRIP,
                                lambda t, cr: full_steps(2 * n_short + ATTN_TRIP * t, ATTN_TRIP, cr), ms))

        first_diag = n_full
        needed = lambda s, c: s * kb < (c + 1) * sb
        tail = {(s, c): scores(qslot, c, first_diag + s)
                for s in range(1, steps_per_q) for c in range(n_sub) if needed(s, c)}
        for s in range(steps_per_q):
            for c in range(n_sub):
                if not needed(s, c):
                    continue
                st = s_ref[0, c] if s == 0 else tail[(s, c)]
                if (s + 1) * kb - 1 > c * sb:
                    st = jnp.where(key_idx + s * kb <= qry_idx + c * sb, st, NEG_INF)
                    st_max = jnp.max(st, axis=0, keepdims=True)
                else:
                    st_max = smax_ref[0, c] if s == 0 else jnp.max(st, axis=0, keepdims=True)
                ms[c] = softmax_pv(c, first_diag + s, st, st_max, ms[c])

        open_block(jnp.minimum(i + 1, nq - 1))

        for c in range(n_sub):
            acc = acc_ref[c]
            o = acc[:DIFF_HEAD] / acc[DIFF_HEAD:DIFF_HEAD + 1]
            d = o[:, :sb] - lam * o[:, sb:]
            ms_d = jnp.mean(d * d, axis=0, keepdims=True)
            d = d * lax.rsqrt(ms_d + SUBLN_EPS) * g_ref[...] * (1.0 - lambda_init)
            row = pl.multiple_of(i * qb + c * sb, sb)
            o_ref[pl.ds(row, sb), :] = d.T.astype(o_ref.dtype)
        return carry

    open_block(0)
    lax.fori_loop(0, nq, q_block, 0)


def _diff_attention(qkv, lam_params, subln_g_col, *, batch, seq, n_sub, lambda_init):
    qb = n_sub * ATTN_SUB
    assert qb % (2 * ATTN_KEYS) == 0 and seq % qb == 0
    h = DIFF_HEADS
    head_cols = lambda off: pl.BlockSpec((seq, DIFF_HEAD), lambda b, hh: (b, off + hh))
    return pl.pallas_call(
        functools.partial(_attn_kernel, seq=seq, n_sub=n_sub, lambda_init=lambda_init),
        grid=(batch, h),
        in_specs=[
            pl.BlockSpec((8, LANES), lambda b, hh: (0, 0)),
            pl.BlockSpec((DIFF_HEAD, 1), lambda b, hh: (0, 0)),
            head_cols(0), head_cols(h), head_cols(2 * h),
        ],
        out_specs=head_cols(0),
        out_shape=jax.ShapeDtypeStruct((batch * seq, DIFF_WIDTH), BF16),
        scratch_shapes=[pltpu.VMEM((seq // ATTN_KEYS, DIFF_HEAD + ATTN_ONES_ROWS, ATTN_KEYS), BF16),
                        pltpu.VMEM((n_sub, DIFF_HEAD + ATTN_ONES_ROWS, 2 * ATTN_SUB), F32),
                        pltpu.VMEM((2, n_sub, 2 * ATTN_SUB, DIFF_HEAD), BF16),
                        pltpu.VMEM((2, n_sub, ATTN_KEYS, 2 * ATTN_SUB), F32),
                        pltpu.VMEM((2, n_sub, 1, 2 * ATTN_SUB), F32)],
        compiler_params=_cparams(("arbitrary", "arbitrary")),
        name="diff_attention",
    )(lam_params, subln_g_col, qkv, qkv, qkv)


def _outproj2_kernel(ya_ref, yb_ref, x_ref, wa_ref, wb_ref, o_ref):
    o_ref[...] = x_ref[...] + _dot(ya_ref[...], wa_ref[...]) + _dot(yb_ref[...], wb_ref[...])


def _outproj2(ya, yb, x, wa, wb, *, tm):
    m, d = x.shape
    return pl.pallas_call(
        _outproj2_kernel,
        grid=(m // tm,),
        in_specs=[
            pl.BlockSpec((tm, ya.shape[1]), lambda i: (i, 0)),
            pl.BlockSpec((tm, yb.shape[1]), lambda i: (i, 0)),
            pl.BlockSpec((tm, d), lambda i: (i, 0)),
            pl.BlockSpec(wa.shape, lambda i: (0, 0)),
            pl.BlockSpec(wb.shape, lambda i: (0, 0)),
        ],
        out_specs=pl.BlockSpec((tm, d), lambda i: (i, 0)),
        out_shape=jax.ShapeDtypeStruct((m, d), F32),
        compiler_params=_cparams(("arbitrary",)),
        name="even_out_proj",
    )(ya, yb, x, wa, wb)


def _sgu_kernel(u_ref, v_ref, x_ref, lng_ref, lnb_ref, ws_ref, bs_ref, wo_ref, o_ref, vn_ref, gated_ref, *, tm):
    ch = GMLP_CHUNK
    ti = lax.broadcasted_iota(jnp.int32, (ch, ch), 0)
    tj = lax.broadcasted_iota(jnp.int32, (ch, ch), 1)
    causal = ti >= tj
    half = tm // 2
    for r0 in range(0, tm, half):
        part = slice(r0, r0 + half)
        v = v_ref[part, :].astype(F32)
        mu = jnp.mean(v, axis=-1, keepdims=True)
        vc = v - mu
        var = jnp.mean(vc * vc, axis=-1, keepdims=True)
        vn_ref[part, :] = (vc * lax.rsqrt(var + LN_EPS) * lng_ref[...] + lnb_ref[...]).astype(BF16)
        for g in range(GMLP_GROUPS):
            cols = slice(g * LANES, (g + 1) * LANES)
            wg = jnp.where(causal, ws_ref[g], 0.0).astype(BF16)
            bias = bs_ref[:, g:g + 1]
            for c in range(half // ch):
                rows = slice(r0 + c * ch, r0 + (c + 1) * ch)
                sv = _dot(wg, vn_ref[rows, cols]) + bias
                gated_ref[rows, cols] = (u_ref[rows, cols].astype(F32) * sv).astype(BF16)
        o_ref[part, :] = x_ref[part, :] + _dot(gated_ref[part, :], wo_ref[...])


def _sgu_outproj(u_v, x, ln_g, ln_b, w_s, b_s_t, w_out, *, tm):
    m, d = x.shape
    n_half = GMLP_WIDTH // d
    return pl.pallas_call(
        functools.partial(_sgu_kernel, tm=tm),
        grid=(m // tm,),
        in_specs=[
            pl.BlockSpec((tm, GMLP_WIDTH), lambda i: (i, 0)),
            pl.BlockSpec((tm, GMLP_WIDTH), lambda i: (i, n_half)),
            pl.BlockSpec((tm, d), lambda i: (i, 0)),
            pl.BlockSpec((1, GMLP_WIDTH), lambda i: (0, 0)),
            pl.BlockSpec((1, GMLP_WIDTH), lambda i: (0, 0)),
            pl.BlockSpec(w_s.shape, lambda i: (0, 0, 0)),
            pl.BlockSpec(b_s_t.shape, lambda i: (0, 0)),
            pl.BlockSpec(w_out.shape, lambda i: (0, 0)),
        ],
        out_specs=pl.BlockSpec((tm, d), lambda i: (i, 0)),
        out_shape=jax.ShapeDtypeStruct((m, d), F32),
        scratch_shapes=[pltpu.VMEM((tm, GMLP_WIDTH), BF16), pltpu.VMEM((tm, GMLP_WIDTH), BF16)],
        compiler_params=_cparams(("arbitrary",)),
        name="sgu_out_proj",
    )(u_v, u_v, x, ln_g, ln_b, w_s, b_s_t, w_out)


def _ffn_kernel(x_ref, g_ref, wg_ref, wu_ref, wd_ref, fg_ref, o_ref, h_ref, *, final):
    j = pl.program_id(1)

    def hidden_tile(rows):
        h = h_ref[rows, :]
        gate = _dot(h, wg_ref[...])
        up = _dot(h, wu_ref[...])
        act = gate * _sigmoid(gate) * up
        o_ref[rows, :] += _dot(act.astype(BF16), wd_ref[...].astype(BF16))

    @pl.when(j == 0)
    def _():
        for rows in _row_parts(x_ref.shape[0]):
            x = x_ref[rows, :]
            h_ref[rows, :] = _rms_norm_rows(x, g_ref[...], NORM_EPS).astype(BF16)
            o_ref[rows, :] = x
            hidden_tile(rows)

    @pl.when(j > 0)
    def _():
        hidden_tile(slice(None))

    if final:
        @pl.when(j == pl.num_programs(1) - 1)
        def _():
            o_ref[...] = _rms_norm_rows(o_ref[...], fg_ref[...], NORM_EPS)


def _ffn(x, g, w_gate, w_up, w_down, final_g, *, tm, th, final):
    m, d = x.shape
    hid = w_gate.shape[1]
    return pl.pallas_call(
        functools.partial(_ffn_kernel, final=final),
        grid=(m // tm, hid // th),
        in_specs=[
            pl.BlockSpec((tm, d), lambda i, j: (i, 0)),
            pl.BlockSpec((1, d), lambda i, j: (0, 0)),
            pl.BlockSpec((d, th), lambda i, j: (0, j)),
            pl.BlockSpec((d, th), lambda i, j: (0, j)),
            pl.BlockSpec((th, d), lambda i, j: (j, 0)),
            pl.BlockSpec((1, d), lambda i, j: (0, 0)),
        ],
        out_specs=pl.BlockSpec((tm, d), lambda i, j: (i, 0)),
        out_shape=jax.ShapeDtypeStruct((m, d), F32),
        scratch_shapes=[pltpu.VMEM((tm, d), BF16)],
        compiler_params=_cparams(("arbitrary", "arbitrary")),
        name="swiglu_ffn",
    )(x, g, w_gate, w_up, w_down, final_g)


def _pad_cols(a, width):
    return jnp.pad(a, ((0, 0), (0, width - a.shape[1])))


def _pad_rows(a, height):
    return jnp.pad(a, ((0, height - a.shape[0]), (0, 0)))


def _rwkv_col_layout(a):
    c = RWKV_WIDTH
    xw = a[:, 3 * c:3 * c + DECAY_LORA]
    xa = a[:, 3 * c + DECAY_LORA:3 * c + DECAY_LORA + AAA_LORA]
    xg = a[:, 3 * c + DECAY_LORA + AAA_LORA:]
    return jnp.concatenate(
        [a[:, :3 * c], _pad_cols(xw, LANES), _pad_cols(xa, LANES), _pad_cols(xg, 2 * LANES)], axis=1)


def _rope_tables(seq):
    inv = ROPE_THETA ** (-jnp.arange(0, DIFF_HALF, 2, dtype=F32) / DIFF_HALF)
    ang = jnp.arange(seq, dtype=F32)[:, None] * inv[None, :]
    cos, sin = jnp.cos(ang), jnp.sin(ang)
    reps = LANES // DIFF_HALF
    return (jnp.tile(jnp.concatenate([cos, cos], axis=1), (1, reps)),
            jnp.tile(jnp.concatenate([-sin, sin], axis=1), (1, reps)))


def _tile_sizes(m, seq):
    return dict(
        proj_tm=min(1024, seq), proj_tn=1024, rwkv_proj_tn=RW_PAD_COLS // 2,
        ffn_tm=min(1024, m), ffn_th=512,
        out_tm=min(512, m), sgu_tm=min(512, m),
        attn_sub=min(4, seq // ATTN_SUB),
    )


def kernel(x, mix_norm, ffn_norm, ffn_w_gate, ffn_w_up, ffn_w_down, ev_w_in, ev_mu, ev_w0, ev_w_dec_up, ev_a0,
           ev_w_a_up, ev_w_g_up, ev_k_k, ev_k_a, ev_r_k, ev_lnx_w, ev_lnx_b, ev_lam_q1, ev_lam_k1, ev_lam_q2,
           ev_lam_k2, ev_subln_g, ev_w_out, od_w_in, od_ln_g, od_ln_b, od_w_s, od_b_s, od_w_out, final_norm):
    batch, seq, d = x.shape
    m = batch * seq
    ts = _tile_sizes(m, seq)
    cos, sin = _rope_tables(seq)
    row = lambda a: a.reshape(1, -1).astype(F32)
    xf = x.reshape(m, d)
    for i in range(DEPTH):
        j = i // 2
        g_mix = row(mix_norm[i])
        if i % 2 == 0:
            lambda_init = 0.8 - 0.6 * math.exp(-0.3 * i)
            w_in = ev_w_in[j]
            w_rwkv = _rwkv_col_layout(w_in[:, :RWKV_COLS]).astype(BF16)
            w_diff = w_in[:, RWKV_COLS:].astype(BF16)
            z = _norm_matmul(xf, g_mix, w_rwkv, tm=ts["proj_tm"], tn=ts["rwkv_proj_tn"], out_dtype=BF16)
            qkv = _norm_matmul_rope(xf, g_mix, w_diff, cos, sin, tm=ts["proj_tm"], tn=ts["proj_tn"], seq=seq)
            y_a = _rwkv_time_mix(
                z, _rwkv_col_layout(row(ev_mu[j])), row(ev_w0[j]),
                _pad_rows(ev_w_dec_up[j], LANES).astype(BF16), row(ev_a0[j]),
                _pad_rows(ev_w_a_up[j], LANES).astype(BF16), _pad_rows(ev_w_g_up[j], 2 * LANES).astype(BF16),
                row(ev_k_k[j]), row(ev_k_a[j]), row(ev_r_k[j]), row(ev_lnx_w[j]), row(ev_lnx_b[j]),
                batch=batch, seq=seq)
            lam_params = _pad_rows(_pad_cols(
                jnp.stack([ev_lam_q1[j], ev_lam_k1[j], ev_lam_q2[j], ev_lam_k2[j]]).astype(F32), LANES), 8)
            y_b = _diff_attention(qkv, lam_params, ev_subln_g[j].reshape(-1, 1).astype(F32), batch=batch, seq=seq,
                                  n_sub=ts["attn_sub"], lambda_init=lambda_init)
            w_out = ev_w_out[j].astype(BF16)
            xf = _outproj2(y_a, y_b, xf, w_out[:RWKV_WIDTH], w_out[RWKV_WIDTH:], tm=ts["out_tm"])
        else:
            u_v = _norm_matmul(xf, g_mix, od_w_in[j].astype(BF16), tm=ts["proj_tm"], tn=ts["proj_tn"],
                               out_dtype=BF16, gelu=True)
            xf = _sgu_outproj(u_v, xf, row(od_ln_g[j]), row(od_ln_b[j]), od_w_s[j].astype(F32),
                              od_b_s[j].T.astype(F32), od_w_out[j].astype(BF16), tm=ts["sgu_tm"])
        xf = _ffn(xf, row(ffn_norm[i]), ffn_w_gate[i].astype(BF16), ffn_w_up[i].astype(BF16),
                  ffn_w_down[i].astype(F32), row(final_norm), tm=ts["ffn_tm"], th=ts["ffn_th"],
                  final=(i == DEPTH - 1))
    return xf.reshape(batch, seq, d)
```

```python
import functools
import math

import jax
import jax.numpy as jnp
from jax import lax
from jax.experimental import pallas as pl
from jax.experimental.pallas import tpu as pltpu

F32 = jnp.float32
BF16 = jnp.bfloat16

D_MODEL = 2048
DEPTH = 4
RWKV_WIDTH = D_MODEL // 2
RWKV_HEAD = 64
DECAY_LORA = 64
AAA_LORA = 64
GATE_LORA = 160
RWKV_COLS = 3 * RWKV_WIDTH + DECAY_LORA + AAA_LORA + GATE_LORA
RWKV_LNX_EPS = 64e-5
DIFF_WIDTH = D_MODEL - RWKV_WIDTH
DIFF_HEAD = 128
DIFF_HALF = DIFF_HEAD // 2
DIFF_HEADS = DIFF_WIDTH // DIFF_HEAD
ROPE_THETA = 10000.0
NEG_INF = -1e30
GMLP_WIDTH = D_MODEL
GMLP_CHUNK = 128
GMLP_GROUPS = 16
FFN_HIDDEN = 5632
NORM_EPS = 1e-6
SUBLN_EPS = 1e-5
LN_EPS = 1e-5

LANES = 128
MXU_COLS = 256
PROLOGUE_PARTS = 4
VMEM_LIMIT_BYTES = 60 * 1024 * 1024

RW_XW = 3 * RWKV_WIDTH
RW_XA = RW_XW + LANES
RW_XG = RW_XA + LANES
RW_PAD_COLS = RW_XG + 2 * LANES

RWKV_CHUNK = 64
RWKV_CHUNKS_PER_STEP = 8
ATTN_SUB = 128
ATTN_KEYS = 256
ATTN_ONES_ROWS = 16
ATTN_TRIP = 8
LOG2_E = 1.4426950408889634
PAIR = 2 * RWKV_HEAD


def _cparams(semantics):
    return pltpu.CompilerParams(dimension_semantics=semantics, vmem_limit_bytes=VMEM_LIMIT_BYTES)


def _dot(a, b):
    return jnp.dot(a, b, preferred_element_type=F32)


def _dot_nt(a, b):
    return lax.dot_general(a, b, (((1,), (1,)), ((), ())), preferred_element_type=F32)


def _rms_norm_rows(x, g, eps):
    ms = jnp.mean(x * x, axis=-1, keepdims=True)
    return x * lax.rsqrt(ms + eps) * g


def _row_parts(tm):
    part = tm // PROLOGUE_PARTS
    return [slice(r0, r0 + part) for r0 in range(0, tm, part)]


def _norm_matmul_kernel(x_ref, g_ref, w_ref, o_ref, h_ref, *, tn, gelu):
    j = pl.program_id(1)

    def columns(rows):
        for n0 in range(0, tn, MXU_COLS):
            acc = _dot(h_ref[rows, :], w_ref[:, n0:n0 + MXU_COLS])
            if gelu:
                acc = 0.5 * acc * (1.0 + lax.erf(acc * math.sqrt(0.5)))
            o_ref[rows, n0:n0 + MXU_COLS] = acc.astype(o_ref.dtype)

    @pl.when(j == 0)
    def _():
        for rows in _row_parts(x_ref.shape[0]):
            h_ref[rows, :] = _rms_norm_rows(x_ref[rows, :], g_ref[...], NORM_EPS).astype(BF16)
            columns(rows)

    @pl.when(j > 0)
    def _():
        columns(slice(None))


def _norm_matmul(x, g, w, *, tm, tn, out_dtype, gelu=False):
    m, k = x.shape
    n = w.shape[1]
    return pl.pallas_call(
        functools.partial(_norm_matmul_kernel, tn=tn, gelu=gelu),
        grid=(m // tm, n // tn),
        in_specs=[
            pl.BlockSpec((tm, k), lambda i, j: (i, 0)),
            pl.BlockSpec((1, k), lambda i, j: (0, 0)),
            pl.BlockSpec((k, tn), lambda i, j: (0, j)),
        ],
        out_specs=pl.BlockSpec((tm, tn), lambda i, j: (i, j)),
        out_shape=jax.ShapeDtypeStruct((m, n), out_dtype),
        scratch_shapes=[pltpu.VMEM((tm, k), BF16)],
        compiler_params=_cparams(("arbitrary", "arbitrary")),
        name="norm_matmul_gelu" if gelu else "norm_matmul",
    )(x, g, w)


def _norm_matmul_rope_kernel(x_ref, g_ref, w_ref, cos_ref, sin_ref, o_ref, h_ref, *, tn, q_tiles, rope_tiles):
    j = pl.program_id(1)

    def columns(rows, epilogue):
        for n0 in range(0, tn, MXU_COLS):
            acc = _dot(h_ref[rows, :], w_ref[:, n0:n0 + MXU_COLS])
            o_ref[rows, n0:n0 + MXU_COLS] = epilogue(rows, acc).astype(o_ref.dtype)

    def rope(rows, acc):
        reps = MXU_COLS // LANES
        c = jnp.concatenate([cos_ref[rows, :]] * reps, axis=1)
        s = jnp.concatenate([sin_ref[rows, :]] * reps, axis=1)
        lane = lax.broadcasted_iota(jnp.int32, acc.shape, 1)
        partner = jnp.where((lane & (DIFF_HALF // 2)) == 0,
                            pltpu.roll(acc, MXU_COLS - DIFF_HALF // 2, 1),
                            pltpu.roll(acc, DIFF_HALF // 2, 1))
        scale = jnp.where(j < q_tiles, DIFF_HALF ** -0.5 * LOG2_E, 1.0).astype(F32)
        return (acc * c + partner * s) * scale

    @pl.when(j == 0)
    def _():
        for rows in _row_parts(x_ref.shape[0]):
            h_ref[rows, :] = _rms_norm_rows(x_ref[rows, :], g_ref[...], NORM_EPS).astype(BF16)
            columns(rows, rope)

    @pl.when(jnp.logical_and(j > 0, j < rope_tiles))
    def _():
        columns(slice(None), rope)

    @pl.when(j >= rope_tiles)
    def _():
        columns(slice(None), lambda rows, acc: acc)


def _norm_matmul_rope(x, g, w, cos, sin, *, tm, tn, seq):
    m, k = x.shape
    n = w.shape[1]
    q_tiles = DIFF_WIDTH // tn
    t_tiles = seq // tm
    return pl.pallas_call(
        functools.partial(_norm_matmul_rope_kernel, tn=tn, q_tiles=q_tiles, rope_tiles=2 * q_tiles),
        grid=(m // tm, n // tn),
        in_specs=[
            pl.BlockSpec((tm, k), lambda i, j: (i, 0)),
            pl.BlockSpec((1, k), lambda i, j: (0, 0)),
            pl.BlockSpec((k, tn), lambda i, j: (0, j)),
            pl.BlockSpec((tm, LANES), lambda i, j: (i % t_tiles, 0)),
            pl.BlockSpec((tm, LANES), lambda i, j: (i % t_tiles, 0)),
        ],
        out_specs=pl.BlockSpec((tm, tn), lambda i, j: (i, j)),
        out_shape=jax.ShapeDtypeStruct((m, n), BF16),
        scratch_shapes=[pltpu.VMEM((tm, k), BF16)],
        compiler_params=_cparams(("arbitrary", "arbitrary")),
        name="norm_matmul_rope",
    )(x, g, w, cos, sin)


def _sigmoid(y):
    return 1.0 / (1.0 + jnp.exp(-y))


def _split_bf16(x, parts):
    out = []
    rem = x
    for _ in range(parts):
        p = rem.astype(BF16)
        out.append(p)
        rem = rem - p.astype(F32)
    return out


def _rwkv_kernel(z_ref, mu_ref, w0_ref, wdec_ref, a0_ref, wa_ref, wg_ref, kk_ref, ka_ref, rk_ref,
                 lw_ref, lb_ref, o_ref, s_ref, prev_ref):
    L = RWKV_CHUNK
    C = RWKV_WIDTH
    n_pairs = C // PAIR
    n_batch = z_ref.shape[0]

    @pl.when(pl.program_id(0) == 0)
    def _():
        s_ref[...] = jnp.zeros_like(s_ref)
        prev_ref[...] = jnp.zeros_like(prev_ref)

    ti = lax.broadcasted_iota(jnp.int32, (L, L), 0)
    tj = lax.broadcasted_iota(jnp.int32, (L, L), 1)
    tril_ones = jnp.where(ti >= tj, 1.0, 0.0).astype(BF16)

    def prepare(bi, row0):
        z = z_ref[bi, pl.ds(row0, L), :].astype(F32)
        row = lax.broadcasted_iota(jnp.int32, z.shape, 0)
        z_prev = jnp.where(row == 0, prev_ref[bi, 0:1, :], pltpu.roll(z, 1, 0))
        prev_ref[bi, 0:1, :] = z[L - 1:L, :]
        zs = z + (z_prev - z) * mu_ref[...]

        r = zs[:, 0:C]
        k = zs[:, C:2 * C]
        v = zs[:, 2 * C:3 * C]
        xw = zs[:, RW_XW:RW_XA]
        xa = zs[:, RW_XA:RW_XG]
        xg = zs[:, RW_XG:RW_PAD_COLS]

        y_dec = w0_ref[...] + _dot(jnp.tanh(xw).astype(BF16), wdec_ref[...])
        logd = -math.exp(-0.5) * _sigmoid(y_dec)
        a_lr = _sigmoid(a0_ref[...] + _dot(xa.astype(BF16), wa_ref[...]))
        gate = _dot(_sigmoid(xg).astype(BF16), wg_ref[...])
        kk_raw = k * kk_ref[...]
        k2 = k * (1.0 + (a_lr - 1.0) * ka_ref[...])
        rkk = r * k2 * rk_ref[...]
        cum = sum(_dot(tril_ones, p) for p in _split_bf16(logd, 2))
        return r, k2, v, logd, a_lr, gate, kk_raw, rkk, cum

    rr = lax.broadcasted_iota(jnp.int32, (PAIR, PAIR), 0)
    cc = lax.broadcasted_iota(jnp.int32, (PAIR, PAIR), 1)
    same_head = (rr // RWKV_HEAD) == (cc // RWKV_HEAD)
    strict = jnp.logical_and(same_head, rr > cc)
    incl = jnp.logical_and(same_head, rr >= cc)
    head_ones = jnp.where(same_head, 1.0, 0.0).astype(BF16)
    eye = jnp.where(rr == cc, 1.0, 0.0).astype(F32)
    first_head = lax.broadcasted_iota(jnp.int32, (L, PAIR), 1) < RWKV_HEAD

    def head_sum(x):
        return _dot(x.astype(BF16), head_ones)

    def stack(x):
        return jnp.concatenate([jnp.where(first_head, x, 0.0), jnp.where(first_head, 0.0, x)], axis=0)

    n_steps = int(math.log2(L)) - 1
    idx = range(n_pairs)
    sls = [slice(p * PAIR, (p + 1) * PAIR) for p in idx]

    def elementwise(prep, p):
        r, k2, v, logd, a_lr, _, kk_raw, rkk, cum = prep
        sl = sls[p]
        kk_p = kk_raw[:, sl]
        kk_n = kk_p * lax.rsqrt(jnp.maximum(head_sum(kk_p * kk_p), 1e-24))
        bonus = head_sum(rkk[:, sl])
        a_p = -kk_n
        b_p = kk_n * a_lr[:, sl]
        r_p, k_p = r[:, sl], k2[:, sl]
        cu = cum[:, sl]
        c_end = cu[L - 1:L, :]
        e_pos = jnp.exp(cu)
        e_neg = jnp.exp(-cu)
        e_prev = jnp.exp(cu - logd[:, sl])
        e_end = jnp.exp(c_end - cu)
        lhs = jnp.concatenate([stack(a_p * e_prev), stack(r_p * e_pos)], axis=0).astype(BF16)
        rhs = jnp.concatenate([stack(b_p * e_neg), stack(k_p * e_neg)], axis=0).astype(BF16)
        bk_end = jnp.concatenate([stack(b_p * e_end), stack(k_p * e_end)], axis=0).astype(BF16)
        return lhs, rhs, bk_end, stack(v[:, sl]), jnp.exp(c_end), bonus

    def stages(bi, row0, prep, elems, filler):
        v, gate = prep[2], prep[5]
        lhs, rhs, bk_end, v_st, d_end, bonus_s = zip(*elems)
        pm = [_dot_nt(lhs[i], rhs[i]) for i in idx]
        filler()
        s_old = [s_ref[bi, p] for p in idx]
        sh = [_dot_nt(lhs[i], s_old[i].astype(BF16)) for i in idx]
        filler()
        p_ab = [jnp.where(strict, pm[i][:PAIR, :PAIR], 0.0) for i in idx]
        p_ak = [jnp.where(strict, pm[i][:PAIR, PAIR:], 0.0).astype(BF16) for i in idx]
        p_rbk = [jnp.concatenate([jnp.where(incl, pm[i][PAIR:, :PAIR], 0.0),
                                  jnp.where(incl, pm[i][PAIR:, PAIR:], 0.0)], axis=1).astype(BF16) for i in idx]
        akv = [_dot(p_ak[i], v_st[i].astype(BF16)) for i in idx]
        filler()

        t_inv = [eye + p_ab[i] for i in idx]
        xb = [p_ab[i].astype(BF16) for i in idx]
        xb = [_dot(xb[i], xb[i]).astype(BF16) for i in idx]
        filler()
        for it in range(n_steps):
            if it + 1 < n_steps:
                both = [_dot(xb[i], jnp.concatenate([xb[i], t_inv[i].astype(BF16)], axis=1)) for i in idx]
                xb = [both[i][:, :PAIR].astype(BF16) for i in idx]
                t_inv = [t_inv[i] + both[i][:, PAIR:] for i in idx]
            else:
                t_inv = [t_inv[i] + _dot(xb[i], t_inv[i].astype(BF16)) for i in idx]
            filler()

        u_st = [_dot(t_inv[i].astype(BF16), (sh[i][:PAIR] + akv[i]).astype(BF16)) for i in idx]
        filler()
        uv = [jnp.concatenate([u_st[i], v_st[i]], axis=0) for i in idx]
        y_st = [sh[i][PAIR:] + _dot(p_rbk[i], uv[i].astype(BF16)) for i in idx]
        filler()
        s_upd = [_dot(uv[i].T.astype(BF16), bk_end[i]) for i in idx]
        for p in idx:
            s_ref[bi, p] = s_old[p] * d_end[p] + s_upd[p]

        y = [y_st[i][:L] + y_st[i][L:] for i in idx]
        mean = [head_sum(y[i]) * (1.0 / RWKV_HEAD) for i in idx]
        yc = [y[i] - mean[i] for i in idx]
        var = [head_sum(yc[i] * yc[i]) * (1.0 / RWKV_HEAD) for i in idx]
        for i in idx:
            sl = sls[i]
            yn = yc[i] * lax.rsqrt(var[i] + RWKV_LNX_EPS) * lw_ref[:, sl] + lb_ref[:, sl]
            o_ref[bi, pl.ds(row0, L), sl] = ((yn + bonus_s[i] * v[:, sl]) * gate[:, sl]).astype(o_ref.dtype)

    def chunk(ci, carry):
        row0 = pl.multiple_of(ci * L, L)
        prepared = [prepare(bi, row0) for bi in range(n_batch)]
        elems = [elementwise(prepared[0], p) for p in idx]
        for bi in range(n_batch):
            nxt_elems, queue = [], []
            if bi + 1 < n_batch:
                queue = [functools.partial(lambda row, p: nxt_elems.append(elementwise(prepared[row], p)), bi + 1, p)
                         for p in idx]

            def filler():
                if queue:
                    queue.pop(0)()

            stages(bi, row0, prepared[bi], elems, filler)
            while queue:
                queue.pop(0)()
            elems = nxt_elems
        return carry

    lax.fori_loop(0, z_ref.shape[1] // L, chunk, 0)


def _rwkv_time_mix(z, mu, w0, wdec, a0, wa, wg, k_k, k_a, r_k, lnx_w, lnx_b, *, batch, seq):
    rows = min(RWKV_CHUNK * RWKV_CHUNKS_PER_STEP, seq)
    n_pairs = RWKV_WIDTH // PAIR
    full = lambda shape: pl.BlockSpec(shape, lambda c: (0,) * len(shape))
    out = pl.pallas_call(
        _rwkv_kernel,
        grid=(seq // rows,),
        in_specs=[
            pl.BlockSpec((batch, rows, RW_PAD_COLS), lambda c: (0, c, 0)),
            full((1, RW_PAD_COLS)),
            full((1, RWKV_WIDTH)), full((LANES, RWKV_WIDTH)),
            full((1, RWKV_WIDTH)), full((LANES, RWKV_WIDTH)),
            full((2 * LANES, RWKV_WIDTH)),
            full((1, RWKV_WIDTH)), full((1, RWKV_WIDTH)), full((1, RWKV_WIDTH)),
            full((1, RWKV_WIDTH)), full((1, RWKV_WIDTH)),
        ],
        out_specs=pl.BlockSpec((batch, rows, RWKV_WIDTH), lambda c: (0, c, 0)),
        out_shape=jax.ShapeDtypeStruct((batch, seq, RWKV_WIDTH), BF16),
        scratch_shapes=[pltpu.VMEM((batch, n_pairs, PAIR, PAIR), F32), pltpu.VMEM((batch, 8, RW_PAD_COLS), F32)],
        compiler_params=_cparams(("arbitrary",)),
        name="rwkv7_time_mix",
    )(z.reshape(batch, seq, RW_PAD_COLS), mu, w0, wdec, a0, wa, wg, k_k, k_a, r_k, lnx_w, lnx_b)
    return out.reshape(batch * seq, RWKV_WIDTH)


def _attn_kernel(lam_ref, g_ref, q_ref, k_ref, v_ref, o_ref, vt_ref, acc_ref, qq_ref, s_ref, smax_ref, *, seq,
                 n_sub, lambda_init):
    sb = ATTN_SUB
    two = 2 * sb
    kb = ATTN_KEYS
    qb = n_sub * sb
    steps_per_q = qb // kb
    nq = seq // qb

    def transpose_block(t, carry):
        start = pl.multiple_of(t * kb, kb)
        vt_ref[t, :DIFF_HEAD, :] = v_ref[pl.ds(start, kb), :].astype(F32).T.astype(BF16)
        vt_ref[t, DIFF_HEAD:, :] = jnp.ones((ATTN_ONES_ROWS, kb), BF16)
        return carry
    lax.fori_loop(0, seq // kb, transpose_block, 0)

    lane = lax.broadcasted_iota(jnp.int32, (sb, DIFF_HEAD), 1)
    key_idx = lax.broadcasted_iota(jnp.int32, (kb, two), 0)
    qry_idx = lax.broadcasted_iota(jnp.int32, (kb, two), 1) & (sb - 1)
    lam_p = lam_ref[...]
    lam = (jnp.exp(jnp.sum(lam_p[0:1] * lam_p[1:2], axis=-1, keepdims=True))
           - jnp.exp(jnp.sum(lam_p[2:3] * lam_p[3:4], axis=-1, keepdims=True)) + lambda_init)

    def scores(qslot, c, j):
        start = pl.multiple_of(j * kb, kb)
        return _dot_nt(k_ref[pl.ds(start, kb), :], qq_ref[qslot, c])

    def softmax_pv(c, j, st, st_max, m):
        m_new = jnp.maximum(m, st_max)
        alpha = jnp.exp2(m - m_new)
        pr = jnp.exp2(st - m_new)
        acc_ref[c] = alpha * acc_ref[c] + _dot(vt_ref[j], pr.astype(BF16))
        return m_new

    def stage_scores(qslot, slot, c, j):
        st = scores(qslot, c, j)
        s_ref[slot, c] = st
        smax_ref[slot, c] = jnp.max(st, axis=0, keepdims=True)

    def open_block(i):
        qslot = i % 2
        for c in range(n_sub):
            row = pl.multiple_of(i * qb + c * sb, sb)
            qc = q_ref[pl.ds(row, sb), :]
            zero = jnp.zeros_like(qc)
            qq_ref[qslot, c] = jnp.concatenate([jnp.where(lane < DIFF_HALF, qc, zero),
                                                jnp.where(lane < DIFF_HALF, zero, qc)], axis=0)
        for c in range(n_sub):
            stage_scores(qslot, 0, c, 0)

    def q_block(i, carry):
        qslot = i % 2
        for c in range(n_sub):
            acc_ref[c] = jnp.zeros((DIFF_HEAD + ATTN_ONES_ROWS, two), F32)

        def full_steps(j0, n_steps, ms):
            ms = list(ms)
            for h in range(n_steps):
                for c in range(n_sub):
                    stage_scores(qslot, 1 - h % 2, c, j0 + h + 1)
                for c in range(n_sub):
                    ms[c] = softmax_pv(c, j0 + h, s_ref[h % 2, c], smax_ref[h % 2, c], ms[c])
            return tuple(ms)

        init = tuple(jnp.full((1, two), NEG_INF, F32) for _ in range(n_sub))
        n_full = i * steps_per_q
        n_short = (n_full % ATTN_TRIP) // 2
        ms = lax.fori_loop(0, n_short, lambda t, cr: full_steps(2 * t, 2, cr), init)
        ms = list(lax.fori_loop(0, n_full // ATTN_TRIP,
                                lambda t, cr: full_steps(2 * n_short + ATTN_TRIP * t, ATTN_TRIP, cr), ms))

        first_diag = n_full
        needed = lambda s, c: s * kb < (c + 1) * sb
        tail = {(s, c): scores(qslot, c, first_diag + s)
                for s in range(1, steps_per_q) for c in range(n_sub) if needed(s, c)}
        for s in range(steps_per_q):
            for c in range(n_sub):
                if not needed(s, c):
                    continue
                st = s_ref[0, c] if s == 0 else tail[(s, c)]
                if (s + 1) * kb - 1 > c * sb:
                    st = jnp.where(key_idx + s * kb <= qry_idx + c * sb, st, NEG_INF)
                    st_max = jnp.max(st, axis=0, keepdims=True)
                else:
                    st_max = smax_ref[0, c] if s == 0 else jnp.max(st, axis=0, keepdims=True)
                ms[c] = softmax_pv(c, first_diag + s, st, st_max, ms[c])

        open_block(jnp.minimum(i + 1, nq - 1))

        for c in range(n_sub):
            acc = acc_ref[c]
            o = acc[:DIFF_HEAD] / acc[DIFF_HEAD:DIFF_HEAD + 1]
            d = o[:, :sb] - lam * o[:, sb:]
            ms_d = jnp.mean(d * d, axis=0, keepdims=True)
            d = d * lax.rsqrt(ms_d + SUBLN_EPS) * g_ref[...] * (1.0 - lambda_init)
            row = pl.multiple_of(i * qb + c * sb, sb)
            o_ref[pl.ds(row, sb), :] = d.T.astype(o_ref.dtype)
        return carry

    open_block(0)
    lax.fori_loop(0, nq, q_block, 0)


def _diff_attention(qkv, lam_params, subln_g_col, *, batch, seq, n_sub, lambda_init):
    qb = n_sub * ATTN_SUB
    assert qb % (2 * ATTN_KEYS) == 0 and seq % qb == 0
    h = DIFF_HEADS
    head_cols = lambda off: pl.BlockSpec((seq, DIFF_HEAD), lambda b, hh: (b, off + hh))
    return pl.pallas_call(
        functools.partial(_attn_kernel, seq=seq, n_sub=n_sub, lambda_init=lambda_init),
        grid=(batch, h),
        in_specs=[
            pl.BlockSpec((8, LANES), lambda b, hh: (0, 0)),
            pl.BlockSpec((DIFF_HEAD, 1), lambda b, hh: (0, 0)),
            head_cols(0), head_cols(h), head_cols(2 * h),
        ],
        out_specs=head_cols(0),
        out_shape=jax.ShapeDtypeStruct((batch * seq, DIFF_WIDTH), BF16),
        scratch_shapes=[pltpu.VMEM((seq // ATTN_KEYS, DIFF_HEAD + ATTN_ONES_ROWS, ATTN_KEYS), BF16),
                        pltpu.VMEM((n_sub, DIFF_HEAD + ATTN_ONES_ROWS, 2 * ATTN_SUB), F32),
                        pltpu.VMEM((2, n_sub, 2 * ATTN_SUB, DIFF_HEAD), BF16),
                        pltpu.VMEM((2, n_sub, ATTN_KEYS, 2 * ATTN_SUB), F32),
                        pltpu.VMEM((2, n_sub, 1, 2 * ATTN_SUB), F32)],
        compiler_params=_cparams(("arbitrary", "arbitrary")),
        name="diff_attention",
    )(lam_params, subln_g_col, qkv, qkv, qkv)


def _outproj2_kernel(ya_ref, yb_ref, x_ref, wa_ref, wb_ref, o_ref):
    o_ref[...] = x_ref[...] + _dot(ya_ref[...], wa_ref[...]) + _dot(yb_ref[...], wb_ref[...])


def _outproj2(ya, yb, x, wa, wb, *, tm):
    m, d = x.shape
    return pl.pallas_call(
        _outproj2_kernel,
        grid=(m // tm,),
        in_specs=[
            pl.BlockSpec((tm, ya.shape[1]), lambda i: (i, 0)),
            pl.BlockSpec((tm, yb.shape[1]), lambda i: (i, 0)),
            pl.BlockSpec((tm, d), lambda i: (i, 0)),
            pl.BlockSpec(wa.shape, lambda i: (0, 0)),
            pl.BlockSpec(wb.shape, lambda i: (0, 0)),
        ],
        out_specs=pl.BlockSpec((tm, d), lambda i: (i, 0)),
        out_shape=jax.ShapeDtypeStruct((m, d), F32),
        compiler_params=_cparams(("arbitrary",)),
        name="even_out_proj",
    )(ya, yb, x, wa, wb)


def _sgu_kernel(u_ref, v_ref, x_ref, lng_ref, lnb_ref, ws_ref, bs_ref, wo_ref, o_ref, vn_ref, gated_ref, *, tm):
    ch = GMLP_CHUNK
    ti = lax.broadcasted_iota(jnp.int32, (ch, ch), 0)
    tj = lax.broadcasted_iota(jnp.int32, (ch, ch), 1)
    causal = ti >= tj
    half = tm // 2
    for r0 in range(0, tm, half):
        part = slice(r0, r0 + half)
        v = v_ref[part, :].astype(F32)
        mu = jnp.mean(v, axis=-1, keepdims=True)
        vc = v - mu
        var = jnp.mean(vc * vc, axis=-1, keepdims=True)
        vn_ref[part, :] = (vc * lax.rsqrt(var + LN_EPS) * lng_ref[...] + lnb_ref[...]).astype(BF16)
        for g in range(GMLP_GROUPS):
            cols = slice(g * LANES, (g + 1) * LANES)
            wg = jnp.where(causal, ws_ref[g], 0.0).astype(BF16)
            bias = bs_ref[:, g:g + 1]
            for c in range(half // ch):
                rows = slice(r0 + c * ch, r0 + (c + 1) * ch)
                sv = _dot(wg, vn_ref[rows, cols]) + bias
                gated_ref[rows, cols] = (u_ref[rows, cols].astype(F32) * sv).astype(BF16)
        o_ref[part, :] = x_ref[part, :] + _dot(gated_ref[part, :], wo_ref[...])


def _sgu_outproj(u_v, x, ln_g, ln_b, w_s, b_s_t, w_out, *, tm):
    m, d = x.shape
    n_half = GMLP_WIDTH // d
    return pl.pallas_call(
        functools.partial(_sgu_kernel, tm=tm),
        grid=(m // tm,),
        in_specs=[
            pl.BlockSpec((tm, GMLP_WIDTH), lambda i: (i, 0)),
            pl.BlockSpec((tm, GMLP_WIDTH), lambda i: (i, n_half)),
            pl.BlockSpec((tm, d), lambda i: (i, 0)),
            pl.BlockSpec((1, GMLP_WIDTH), lambda i: (0, 0)),
            pl.BlockSpec((1, GMLP_WIDTH), lambda i: (0, 0)),
            pl.BlockSpec(w_s.shape, lambda i: (0, 0, 0)),
            pl.BlockSpec(b_s_t.shape, lambda i: (0, 0)),
            pl.BlockSpec(w_out.shape, lambda i: (0, 0)),
        ],
        out_specs=pl.BlockSpec((tm, d), lambda i: (i, 0)),
        out_shape=jax.ShapeDtypeStruct((m, d), F32),
        scratch_shapes=[pltpu.VMEM((tm, GMLP_WIDTH), BF16), pltpu.VMEM((tm, GMLP_WIDTH), BF16)],
        compiler_params=_cparams(("arbitrary",)),
        name="sgu_out_proj",
    )(u_v, u_v, x, ln_g, ln_b, w_s, b_s_t, w_out)


def _ffn_kernel(x_ref, g_ref, wg_ref, wu_ref, wd_ref, fg_ref, o_ref, h_ref, *, final):
    j = pl.program_id(1)

    def hidden_tile(rows):
        h = h_ref[rows, :]
        gate = _dot(h, wg_ref[...])
        up = _dot(h, wu_ref[...])
        act = gate * _sigmoid(gate) * up
        o_ref[rows, :] += _dot(act.astype(BF16), wd_ref[...].astype(BF16))

    @pl.when(j == 0)
    def _():
        for rows in _row_parts(x_ref.shape[0]):
            x = x_ref[rows, :]
            h_ref[rows, :] = _rms_norm_rows(x, g_ref[...], NORM_EPS).astype(BF16)
            o_ref[rows, :] = x
            hidden_tile(rows)

    @pl.when(j > 0)
    def _():
        hidden_tile(slice(None))

    if final:
        @pl.when(j == pl.num_programs(1) - 1)
        def _():
            o_ref[...] = _rms_norm_rows(o_ref[...], fg_ref[...], NORM_EPS)


def _ffn(x, g, w_gate, w_up, w_down_all, final_g, *, layer, tm, th, final):
    m, d = x.shape
    hid = w_gate.shape[1]
    return pl.pallas_call(
        functools.partial(_ffn_kernel, final=final),
        grid=(m // tm, hid // th),
        in_specs=[
            pl.BlockSpec((tm, d), lambda i, j: (i, 0)),
            pl.BlockSpec((1, d), lambda i, j: (0, 0)),
            pl.BlockSpec((d, th), lambda i, j: (0, j)),
            pl.BlockSpec((d, th), lambda i, j: (0, j)),
            pl.BlockSpec((None, th, d), lambda i, j: (layer, j, 0)),
            pl.BlockSpec((1, d), lambda i, j: (0, 0)),
        ],
        out_specs=pl.BlockSpec((tm, d), lambda i, j: (i, 0)),
        out_shape=jax.ShapeDtypeStruct((m, d), F32),
        scratch_shapes=[pltpu.VMEM((tm, d), BF16)],
        compiler_params=_cparams(("arbitrary", "arbitrary")),
        name="swiglu_ffn",
    )(x, g, w_gate, w_up, w_down_all, final_g)


def _pad_cols(a, width):
    return jnp.pad(a, ((0, 0), (0, width - a.shape[1])))


def _pad_rows(a, height):
    return jnp.pad(a, ((0, height - a.shape[0]), (0, 0)))


def _rwkv_col_layout(a):
    c = RWKV_WIDTH
    xw = a[:, 3 * c:3 * c + DECAY_LORA]
    xa = a[:, 3 * c + DECAY_LORA:3 * c + DECAY_LORA + AAA_LORA]
    xg = a[:, 3 * c + DECAY_LORA + AAA_LORA:]
    return jnp.concatenate(
        [a[:, :3 * c], _pad_cols(xw, LANES), _pad_cols(xa, LANES), _pad_cols(xg, 2 * LANES)], axis=1)


def _rwkv_weight_layout(w_in):
    c = RWKV_WIDTH
    out = jnp.zeros((w_in.shape[0], RW_PAD_COLS), BF16)
    out = lax.dynamic_update_slice(out, w_in[:, :3 * c].astype(BF16), (0, 0))
    starts = (3 * c, 3 * c + DECAY_LORA, 3 * c + DECAY_LORA + AAA_LORA, RWKV_COLS)
    for src0, src1, dst in zip(starts[:-1], starts[1:], (RW_XW, RW_XA, RW_XG)):
        out = lax.dynamic_update_slice(out, w_in[:, src0:src1].astype(BF16), (0, dst))
    return out


def _rope_tables(seq):
    inv = ROPE_THETA ** (-jnp.arange(0, DIFF_HALF, 2, dtype=F32) / DIFF_HALF)
    ang = jnp.arange(seq, dtype=F32)[:, None] * inv[None, :]
    cos, sin = jnp.cos(ang), jnp.sin(ang)
    reps = LANES // DIFF_HALF
    return (jnp.tile(jnp.concatenate([cos, cos], axis=1), (1, reps)),
            jnp.tile(jnp.concatenate([-sin, sin], axis=1), (1, reps)))


def _tile_sizes(m, seq):
    return dict(
        proj_tm=min(1024, seq), proj_tn=1024, rwkv_proj_tn=RW_PAD_COLS // 2,
        ffn_tm=min(1024, m), ffn_th=512,
        out_tm=min(512, m), sgu_tm=min(512, m),
        attn_sub=min(4, seq // ATTN_SUB),
    )


def kernel(x, mix_norm, ffn_norm, ffn_w_gate, ffn_w_up, ffn_w_down, ev_w_in, ev_mu, ev_w0, ev_w_dec_up, ev_a0,
           ev_w_a_up, ev_w_g_up, ev_k_k, ev_k_a, ev_r_k, ev_lnx_w, ev_lnx_b, ev_lam_q1, ev_lam_k1, ev_lam_q2,
           ev_lam_k2, ev_subln_g, ev_w_out, od_w_in, od_ln_g, od_ln_b, od_w_s, od_b_s, od_w_out, final_norm):
    batch, seq, d = x.shape
    m = batch * seq
    ts = _tile_sizes(m, seq)
    cos, sin = _rope_tables(seq)
    row = lambda a: a.reshape(1, -1).astype(F32)
    xf = x.reshape(m, d)
    for i in range(DEPTH):
        j = i // 2
        g_mix = row(mix_norm[i])
        if i % 2 == 0:
            lambda_init = 0.8 - 0.6 * math.exp(-0.3 * i)
            w_in = ev_w_in[j]
            w_rwkv = _rwkv_weight_layout(w_in)
            w_diff = w_in[:, RWKV_COLS:].astype(BF16)
            z = _norm_matmul(xf, g_mix, w_rwkv, tm=ts["proj_tm"], tn=ts["rwkv_proj_tn"], out_dtype=BF16)
            qkv = _norm_matmul_rope(xf, g_mix, w_diff, cos, sin, tm=ts["proj_tm"], tn=ts["proj_tn"], seq=seq)
            y_a = _rwkv_time_mix(
                z, _rwkv_col_layout(row(ev_mu[j])), row(ev_w0[j]),
                _pad_rows(ev_w_dec_up[j], LANES).astype(BF16), row(ev_a0[j]),
                _pad_rows(ev_w_a_up[j], LANES).astype(BF16), _pad_rows(ev_w_g_up[j], 2 * LANES).astype(BF16),
                row(ev_k_k[j]), row(ev_k_a[j]), row(ev_r_k[j]), row(ev_lnx_w[j]), row(ev_lnx_b[j]),
                batch=batch, seq=seq)
            lam_params = _pad_rows(_pad_cols(
                jnp.stack([ev_lam_q1[j], ev_lam_k1[j], ev_lam_q2[j], ev_lam_k2[j]]).astype(F32), LANES), 8)
            y_b = _diff_attention(qkv, lam_params, ev_subln_g[j].reshape(-1, 1).astype(F32), batch=batch, seq=seq,
                                  n_sub=ts["attn_sub"], lambda_init=lambda_init)
            w_out = ev_w_out[j].astype(BF16)
            xf = _outproj2(y_a, y_b, xf, w_out[:RWKV_WIDTH], w_out[RWKV_WIDTH:], tm=ts["out_tm"])
        else:
            u_v = _norm_matmul(xf, g_mix, od_w_in[j].astype(BF16), tm=ts["proj_tm"], tn=ts["proj_tn"],
                               out_dtype=BF16, gelu=True)
            xf = _sgu_outproj(u_v, xf, row(od_ln_g[j]), row(od_ln_b[j]), od_w_s[j].astype(F32),
                              od_b_s[j].T.astype(F32), od_w_out[j].astype(BF16), tm=ts["sgu_tm"])
        xf = _ffn(xf, row(ffn_norm[i]), ffn_w_gate[i].astype(BF16), ffn_w_up[i].astype(BF16),
                  ffn_w_down.astype(F32), row(final_norm), layer=i, tm=ts["ffn_tm"], th=ts["ffn_th"],
                  final=(i == DEPTH - 1))
    return xf.reshape(batch, seq, d)
```

```python
import functools
import math

import jax
import jax.numpy as jnp
from jax import lax
from jax.experimental import pallas as pl
from jax.experimental.pallas import tpu as pltpu

F32 = jnp.float32
BF16 = jnp.bfloat16

D_MODEL = 2048
DEPTH = 4
RWKV_WIDTH = D_MODEL // 2
RWKV_HEAD = 64
DECAY_LORA = 64
AAA_LORA = 64
GATE_LORA = 160
RWKV_COLS = 3 * RWKV_WIDTH + DECAY_LORA + AAA_LORA + GATE_LORA
RWKV_LNX_EPS = 64e-5
DIFF_WIDTH = D_MODEL - RWKV_WIDTH
DIFF_HEAD = 128
DIFF_HALF = DIFF_HEAD // 2
DIFF_HEADS = DIFF_WIDTH // DIFF_HEAD
DIFF_COLS = 3 * DIFF_WIDTH
ROPE_THETA = 10000.0
NEG_INF = -1e30
GMLP_WIDTH = D_MODEL
GMLP_CHUNK = 128
GMLP_GROUPS = 16
FFN_HIDDEN = 5632
NORM_EPS = 1e-6
SUBLN_EPS = 1e-5
LN_EPS = 1e-5

LANES = 128
MXU_COLS = 256
PROLOGUE_PARTS = 4
VMEM_LIMIT_BYTES = 60 * 1024 * 1024

RW_XW = 3 * RWKV_WIDTH
RW_XA = RW_XW + LANES
RW_XG = RW_XA + LANES
RW_PAD_COLS = RW_XG + 2 * LANES

RWKV_CHUNK = 64
RWKV_CHUNKS_PER_STEP = 8
ATTN_SUB = 128
ATTN_KEYS = 256
ATTN_ONES_ROWS = 16
ATTN_TRIP = 8
LOG2_E = 1.4426950408889634
PAIR = 2 * RWKV_HEAD


def _cparams(semantics):
    return pltpu.CompilerParams(dimension_semantics=semantics, vmem_limit_bytes=VMEM_LIMIT_BYTES)


def _dot(a, b):
    return jnp.dot(a, b, preferred_element_type=F32)


def _dot_nt(a, b):
    return lax.dot_general(a, b, (((1,), (1,)), ((), ())), preferred_element_type=F32)


def _rms_norm_rows(x, g, eps):
    ms = jnp.mean(x * x, axis=-1, keepdims=True)
    return x * lax.rsqrt(ms + eps) * g


def _row_parts(tm):
    part = tm // PROLOGUE_PARTS
    return [slice(r0, r0 + part) for r0 in range(0, tm, part)]


def _norm_matmul_kernel(x_ref, g_ref, w_ref, o_ref, h_ref, *, tn, gelu):
    j = pl.program_id(1)

    def columns(rows):
        for n0 in range(0, tn, MXU_COLS):
            acc = _dot(h_ref[rows, :], w_ref[:, n0:n0 + MXU_COLS])
            if gelu:
                acc = 0.5 * acc * (1.0 + lax.erf(acc * math.sqrt(0.5)))
            o_ref[rows, n0:n0 + MXU_COLS] = acc.astype(o_ref.dtype)

    @pl.when(j == 0)
    def _():
        for rows in _row_parts(x_ref.shape[0]):
            h_ref[rows, :] = _rms_norm_rows(x_ref[rows, :], g_ref[...], NORM_EPS).astype(BF16)
            columns(rows)

    @pl.when(j > 0)
    def _():
        columns(slice(None))


def _norm_matmul(x, g, w, *, tm, tn, out_dtype, gelu=False):
    m, k = x.shape
    n = w.shape[1]
    return pl.pallas_call(
        functools.partial(_norm_matmul_kernel, tn=tn, gelu=gelu),
        grid=(m // tm, n // tn),
        in_specs=[
            pl.BlockSpec((tm, k), lambda i, j: (i, 0)),
            pl.BlockSpec((1, k), lambda i, j: (0, 0)),
            pl.BlockSpec((k, tn), lambda i, j: (0, j)),
        ],
        out_specs=pl.BlockSpec((tm, tn), lambda i, j: (i, j)),
        out_shape=jax.ShapeDtypeStruct((m, n), out_dtype),
        scratch_shapes=[pltpu.VMEM((tm, k), BF16)],
        compiler_params=_cparams(("arbitrary", "arbitrary")),
        name="norm_matmul_gelu" if gelu else "norm_matmul",
    )(x, g, w)


def _norm_matmul_rope_kernel(x_ref, g_ref, w_ref, cos_ref, sin_ref, o_ref, h_ref, *, tn, q_tiles, rope_tiles):
    j = pl.program_id(1)

    def columns(rows, epilogue):
        for n0 in range(0, tn, MXU_COLS):
            acc = _dot(h_ref[rows, :], w_ref[:, n0:n0 + MXU_COLS])
            o_ref[rows, n0:n0 + MXU_COLS] = epilogue(rows, acc).astype(o_ref.dtype)

    def rope(rows, acc):
        reps = MXU_COLS // LANES
        c = jnp.concatenate([cos_ref[rows, :]] * reps, axis=1)
        s = jnp.concatenate([sin_ref[rows, :]] * reps, axis=1)
        lane = lax.broadcasted_iota(jnp.int32, acc.shape, 1)
        partner = jnp.where((lane & (DIFF_HALF // 2)) == 0,
                            pltpu.roll(acc, MXU_COLS - DIFF_HALF // 2, 1),
                            pltpu.roll(acc, DIFF_HALF // 2, 1))
        scale = jnp.where(j < q_tiles, DIFF_HALF ** -0.5 * LOG2_E, 1.0).astype(F32)
        return (acc * c + partner * s) * scale

    @pl.when(j == 0)
    def _():
        for rows in _row_parts(x_ref.shape[0]):
            h_ref[rows, :] = _rms_norm_rows(x_ref[rows, :], g_ref[...], NORM_EPS).astype(BF16)
            columns(rows, rope)

    @pl.when(jnp.logical_and(j > 0, j < rope_tiles))
    def _():
        columns(slice(None), rope)

    @pl.when(j >= rope_tiles)
    def _():
        columns(slice(None), lambda rows, acc: acc)


def _norm_matmul_rope(x, g, w, cos, sin, *, tm, tn, seq):
    m, k = x.shape
    n = w.shape[1]
    q_tiles = DIFF_WIDTH // tn
    t_tiles = seq // tm
    return pl.pallas_call(
        functools.partial(_norm_matmul_rope_kernel, tn=tn, q_tiles=q_tiles, rope_tiles=2 * q_tiles),
        grid=(m // tm, n // tn),
        in_specs=[
            pl.BlockSpec((tm, k), lambda i, j: (i, 0)),
            pl.BlockSpec((1, k), lambda i, j: (0, 0)),
            pl.BlockSpec((k, tn), lambda i, j: (0, j)),
            pl.BlockSpec((tm, LANES), lambda i, j: (i % t_tiles, 0)),
            pl.BlockSpec((tm, LANES), lambda i, j: (i % t_tiles, 0)),
        ],
        out_specs=pl.BlockSpec((tm, tn), lambda i, j: (i, j)),
        out_shape=jax.ShapeDtypeStruct((m, n), BF16),
        scratch_shapes=[pltpu.VMEM((tm, k), BF16)],
        compiler_params=_cparams(("arbitrary", "arbitrary")),
        name="norm_matmul_rope",
    )(x, g, w, cos, sin)


def _sigmoid(y):
    return 1.0 / (1.0 + jnp.exp(-y))


def _split_bf16(x, parts):
    out = []
    rem = x
    for _ in range(parts):
        p = rem.astype(BF16)
        out.append(p)
        rem = rem - p.astype(F32)
    return out


def _rwkv_kernel(z_ref, mu_ref, w0_ref, wdec_ref, a0_ref, wa_ref, wg_ref, kk_ref, ka_ref, rk_ref,
                 lw_ref, lb_ref, o_ref, s_ref, prev_ref):
    L = RWKV_CHUNK
    C = RWKV_WIDTH
    n_pairs = C // PAIR
    n_batch = z_ref.shape[0]

    @pl.when(pl.program_id(0) == 0)
    def _():
        s_ref[...] = jnp.zeros_like(s_ref)
        prev_ref[...] = jnp.zeros_like(prev_ref)

    ti = lax.broadcasted_iota(jnp.int32, (L, L), 0)
    tj = lax.broadcasted_iota(jnp.int32, (L, L), 1)
    tril_ones = jnp.where(ti >= tj, 1.0, 0.0).astype(BF16)

    def prepare(bi, row0):
        z = z_ref[bi, pl.ds(row0, L), :].astype(F32)
        row = lax.broadcasted_iota(jnp.int32, z.shape, 0)
        z_prev = jnp.where(row == 0, prev_ref[bi, 0:1, :], pltpu.roll(z, 1, 0))
        prev_ref[bi, 0:1, :] = z[L - 1:L, :]
        zs = z + (z_prev - z) * mu_ref[...]

        r = zs[:, 0:C]
        k = zs[:, C:2 * C]
        v = zs[:, 2 * C:3 * C]
        xw = zs[:, RW_XW:RW_XA]
        xa = zs[:, RW_XA:RW_XG]
        xg = zs[:, RW_XG:RW_PAD_COLS]

        y_dec = w0_ref[...] + _dot(jnp.tanh(xw).astype(BF16), wdec_ref[...])
        logd = -math.exp(-0.5) * _sigmoid(y_dec)
        a_lr = _sigmoid(a0_ref[...] + _dot(xa.astype(BF16), wa_ref[...]))
        gate = _dot(_sigmoid(xg).astype(BF16), wg_ref[...])
        kk_raw = k * kk_ref[...]
        k2 = k * (1.0 + (a_lr - 1.0) * ka_ref[...])
        rkk = r * k2 * rk_ref[...]
        cum = sum(_dot(tril_ones, p) for p in _split_bf16(logd, 2))
        return r, k2, v, logd, a_lr, gate, kk_raw, rkk, cum

    rr = lax.broadcasted_iota(jnp.int32, (PAIR, PAIR), 0)
    cc = lax.broadcasted_iota(jnp.int32, (PAIR, PAIR), 1)
    same_head = (rr // RWKV_HEAD) == (cc // RWKV_HEAD)
    strict = jnp.logical_and(same_head, rr > cc)
    incl = jnp.logical_and(same_head, rr >= cc)
    head_ones = jnp.where(same_head, 1.0, 0.0).astype(BF16)
    eye = jnp.where(rr == cc, 1.0, 0.0).astype(F32)
    first_head = lax.broadcasted_iota(jnp.int32, (L, PAIR), 1) < RWKV_HEAD

    def head_sum(x):
        return _dot(x.astype(BF16), head_ones)

    def stack(x):
        return jnp.concatenate([jnp.where(first_head, x, 0.0), jnp.where(first_head, 0.0, x)], axis=0)

    n_steps = int(math.log2(L)) - 1
    idx = range(n_pairs)
    sls = [slice(p * PAIR, (p + 1) * PAIR) for p in idx]

    def elementwise(prep, p):
        r, k2, v, logd, a_lr, _, kk_raw, rkk, cum = prep
        sl = sls[p]
        kk_p = kk_raw[:, sl]
        kk_n = kk_p * lax.rsqrt(jnp.maximum(head_sum(kk_p * kk_p), 1e-24))
        bonus = head_sum(rkk[:, sl])
        a_p = -kk_n
        b_p = kk_n * a_lr[:, sl]
        r_p, k_p = r[:, sl], k2[:, sl]
        cu = cum[:, sl]
        c_end = cu[L - 1:L, :]
        e_pos = jnp.exp(cu)
        e_neg = jnp.exp(-cu)
        e_prev = jnp.exp(cu - logd[:, sl])
        e_end = jnp.exp(c_end - cu)
        lhs = jnp.concatenate([stack(a_p * e_prev), stack(r_p * e_pos)], axis=0).astype(BF16)
        rhs = jnp.concatenate([stack(b_p * e_neg), stack(k_p * e_neg)], axis=0).astype(BF16)
        bk_end = jnp.concatenate([stack(b_p * e_end), stack(k_p * e_end)], axis=0).astype(BF16)
        return lhs, rhs, bk_end, stack(v[:, sl]), jnp.exp(c_end), bonus

    def stages(bi, row0, prep, elems, filler):
        v, gate = prep[2], prep[5]
        lhs, rhs, bk_end, v_st, d_end, bonus_s = zip(*elems)
        pm = [_dot_nt(lhs[i], rhs[i]) for i in idx]
        filler()
        s_old = [s_ref[bi, p] for p in idx]
        sh = [_dot_nt(lhs[i], s_old[i].astype(BF16)) for i in idx]
        filler()
        p_ab = [jnp.where(strict, pm[i][:PAIR, :PAIR], 0.0) for i in idx]
        p_ak = [jnp.where(strict, pm[i][:PAIR, PAIR:], 0.0).astype(BF16) for i in idx]
        p_rbk = [jnp.concatenate([jnp.where(incl, pm[i][PAIR:, :PAIR], 0.0),
                                  jnp.where(incl, pm[i][PAIR:, PAIR:], 0.0)], axis=1).astype(BF16) for i in idx]
        akv = [_dot(p_ak[i], v_st[i].astype(BF16)) for i in idx]
        filler()

        t_inv = [eye + p_ab[i] for i in idx]
        xb = [p_ab[i].astype(BF16) for i in idx]
        xb = [_dot(xb[i], xb[i]).astype(BF16) for i in idx]
        filler()
        for it in range(n_steps):
            if it + 1 < n_steps:
                both = [_dot(xb[i], jnp.concatenate([xb[i], t_inv[i].astype(BF16)], axis=1)) for i in idx]
                xb = [both[i][:, :PAIR].astype(BF16) for i in idx]
                t_inv = [t_inv[i] + both[i][:, PAIR:] for i in idx]
            else:
                t_inv = [t_inv[i] + _dot(xb[i], t_inv[i].astype(BF16)) for i in idx]
            filler()

        u_st = [_dot(t_inv[i].astype(BF16), (sh[i][:PAIR] + akv[i]).astype(BF16)) for i in idx]
        filler()
        uv = [jnp.concatenate([u_st[i], v_st[i]], axis=0) for i in idx]
        y_st = [sh[i][PAIR:] + _dot(p_rbk[i], uv[i].astype(BF16)) for i in idx]
        filler()
        s_upd = [_dot(uv[i].T.astype(BF16), bk_end[i]) for i in idx]
        for p in idx:
            s_ref[bi, p] = s_old[p] * d_end[p] + s_upd[p]

        y = [y_st[i][:L] + y_st[i][L:] for i in idx]
        mean = [head_sum(y[i]) * (1.0 / RWKV_HEAD) for i in idx]
        yc = [y[i] - mean[i] for i in idx]
        var = [head_sum(yc[i] * yc[i]) * (1.0 / RWKV_HEAD) for i in idx]
        for i in idx:
            sl = sls[i]
            yn = yc[i] * lax.rsqrt(var[i] + RWKV_LNX_EPS) * lw_ref[:, sl] + lb_ref[:, sl]
            o_ref[bi, pl.ds(row0, L), sl] = ((yn + bonus_s[i] * v[:, sl]) * gate[:, sl]).astype(o_ref.dtype)

    def chunk(ci, carry):
        row0 = pl.multiple_of(ci * L, L)
        prepared = [prepare(bi, row0) for bi in range(n_batch)]
        elems = [elementwise(prepared[0], p) for p in idx]
        for bi in range(n_batch):
            nxt_elems, queue = [], []
            if bi + 1 < n_batch:
                queue = [functools.partial(lambda row, p: nxt_elems.append(elementwise(prepared[row], p)), bi + 1, p)
                         for p in idx]

            def filler():
                if queue:
                    queue.pop(0)()

            stages(bi, row0, prepared[bi], elems, filler)
            while queue:
                queue.pop(0)()
            elems = nxt_elems
        return carry

    lax.fori_loop(0, z_ref.shape[1] // L, chunk, 0)


def _rwkv_time_mix(z, mu, w0, wdec, a0, wa, wg, k_k, k_a, r_k, lnx_w, lnx_b, *, batch, seq):
    rows = min(RWKV_CHUNK * RWKV_CHUNKS_PER_STEP, seq)
    n_pairs = RWKV_WIDTH // PAIR
    full = lambda shape: pl.BlockSpec(shape, lambda c: (0,) * len(shape))
    out = pl.pallas_call(
        _rwkv_kernel,
        grid=(seq // rows,),
        in_specs=[
            pl.BlockSpec((batch, rows, RW_PAD_COLS), lambda c: (0, c, 0)),
            full((1, RW_PAD_COLS)),
            full((1, RWKV_WIDTH)), full((LANES, RWKV_WIDTH)),
            full((1, RWKV_WIDTH)), full((LANES, RWKV_WIDTH)),
            full((2 * LANES, RWKV_WIDTH)),
            full((1, RWKV_WIDTH)), full((1, RWKV_WIDTH)), full((1, RWKV_WIDTH)),
            full((1, RWKV_WIDTH)), full((1, RWKV_WIDTH)),
        ],
        out_specs=pl.BlockSpec((batch, rows, RWKV_WIDTH), lambda c: (0, c, 0)),
        out_shape=jax.ShapeDtypeStruct((batch, seq, RWKV_WIDTH), BF16),
        scratch_shapes=[pltpu.VMEM((batch, n_pairs, PAIR, PAIR), F32), pltpu.VMEM((batch, 8, RW_PAD_COLS), F32)],
        compiler_params=_cparams(("arbitrary",)),
        name="rwkv7_time_mix",
    )(z.reshape(batch, seq, RW_PAD_COLS), mu, w0, wdec, a0, wa, wg, k_k, k_a, r_k, lnx_w, lnx_b)
    return out.reshape(batch * seq, RWKV_WIDTH)


def _attn_kernel(lam_ref, g_ref, q_ref, k_ref, v_ref, o_ref, vt_ref, acc_ref, qq_ref, s_ref, smax_ref, *, seq,
                 n_sub, lambda_init):
    sb = ATTN_SUB
    two = 2 * sb
    kb = ATTN_KEYS
    qb = n_sub * sb
    steps_per_q = qb // kb
    nq = seq // qb

    def transpose_block(t, carry):
        start = pl.multiple_of(t * kb, kb)
        vt_ref[t, :DIFF_HEAD, :] = v_ref[pl.ds(start, kb), :].astype(F32).T.astype(BF16)
        vt_ref[t, DIFF_HEAD:, :] = jnp.ones((ATTN_ONES_ROWS, kb), BF16)
        return carry
    lax.fori_loop(0, seq // kb, transpose_block, 0)

    lane = lax.broadcasted_iota(jnp.int32, (sb, DIFF_HEAD), 1)
    key_idx = lax.broadcasted_iota(jnp.int32, (kb, two), 0)
    qry_idx = lax.broadcasted_iota(jnp.int32, (kb, two), 1) & (sb - 1)
    lam_p = lam_ref[...]
    lam = (jnp.exp(jnp.sum(lam_p[0:1] * lam_p[1:2], axis=-1, keepdims=True))
           - jnp.exp(jnp.sum(lam_p[2:3] * lam_p[3:4], axis=-1, keepdims=True)) + lambda_init)

    def scores(qslot, c, j):
        start = pl.multiple_of(j * kb, kb)
        return _dot_nt(k_ref[pl.ds(start, kb), :], qq_ref[qslot, c])

    def softmax_pv(c, j, st, st_max, m):
        m_new = jnp.maximum(m, st_max)
        alpha = jnp.exp2(m - m_new)
        pr = jnp.exp2(st - m_new)
        acc_ref[c] = alpha * acc_ref[c] + _dot(vt_ref[j], pr.astype(BF16))
        return m_new

    def stage_scores(qslot, slot, c, j):
        st = scores(qslot, c, j)
        s_ref[slot, c] = st
        smax_ref[slot, c] = jnp.max(st, axis=0, keepdims=True)

    def open_block(i):
        qslot = i % 2
        for c in range(n_sub):
            row = pl.multiple_of(i * qb + c * sb, sb)
            qc = q_ref[pl.ds(row, sb), :]
            zero = jnp.zeros_like(qc)
            qq_ref[qslot, c] = jnp.concatenate([jnp.where(lane < DIFF_HALF, qc, zero),
                                                jnp.where(lane < DIFF_HALF, zero, qc)], axis=0)
        for c in range(n_sub):
            stage_scores(qslot, 0, c, 0)

    def q_block(i, carry):
        qslot = i % 2
        for c in range(n_sub):
            acc_ref[c] = jnp.zeros((DIFF_HEAD + ATTN_ONES_ROWS, two), F32)

        def full_steps(j0, n_steps, ms):
            ms = list(ms)
            for h in range(n_steps):
                for c in range(n_sub):
                    stage_scores(qslot, 1 - h % 2, c, j0 + h + 1)
                for c in range(n_sub):
                    ms[c] = softmax_pv(c, j0 + h, s_ref[h % 2, c], smax_ref[h % 2, c], ms[c])
            return tuple(ms)

        init = tuple(jnp.full((1, two), NEG_INF, F32) for _ in range(n_sub))
        n_full = i * steps_per_q
        n_short = (n_full % ATTN_TRIP) // 2
        ms = lax.fori_loop(0, n_short, lambda t, cr: full_steps(2 * t, 2, cr), init)
        ms = list(lax.fori_loop(0, n_full // ATTN_TRIP,
                                lambda t, cr: full_steps(2 * n_short + ATTN_TRIP * t, ATTN_TRIP, cr), ms))

        first_diag = n_full
        needed = lambda s, c: s * kb < (c + 1) * sb
        tail = {(s, c): scores(qslot, c, first_diag + s)
                for s in range(1, steps_per_q) for c in range(n_sub) if needed(s, c)}
        for s in range(steps_per_q):
            for c in range(n_sub):
                if not needed(s, c):
                    continue
                st = s_ref[0, c] if s == 0 else tail[(s, c)]
                if (s + 1) * kb - 1 > c * sb:
                    st = jnp.where(key_idx + s * kb <= qry_idx + c * sb, st, NEG_INF)
                    st_max = jnp.max(st, axis=0, keepdims=True)
                else:
                    st_max = smax_ref[0, c] if s == 0 else jnp.max(st, axis=0, keepdims=True)
                ms[c] = softmax_pv(c, first_diag + s, st, st_max, ms[c])

        open_block(jnp.minimum(i + 1, nq - 1))

        for c in range(n_sub):
            acc = acc_ref[c]
            o = acc[:DIFF_HEAD] / acc[DIFF_HEAD:DIFF_HEAD + 1]
            d = o[:, :sb] - lam * o[:, sb:]
            ms_d = jnp.mean(d * d, axis=0, keepdims=True)
            d = d * lax.rsqrt(ms_d + SUBLN_EPS) * g_ref[...] * (1.0 - lambda_init)
            row = pl.multiple_of(i * qb + c * sb, sb)
            o_ref[pl.ds(row, sb), :] = d.T.astype(o_ref.dtype)
        return carry

    open_block(0)
    lax.fori_loop(0, nq, q_block, 0)


def _diff_attention(qkv, lam_params, subln_g_col, *, batch, seq, n_sub, lambda_init):
    qb = n_sub * ATTN_SUB
    assert qb % (2 * ATTN_KEYS) == 0 and seq % qb == 0
    h = DIFF_HEADS
    head_cols = lambda off: pl.BlockSpec((seq, DIFF_HEAD), lambda b, hh: (b, off + hh))
    return pl.pallas_call(
        functools.partial(_attn_kernel, seq=seq, n_sub=n_sub, lambda_init=lambda_init),
        grid=(batch, h),
        in_specs=[
            pl.BlockSpec((8, LANES), lambda b, hh: (0, 0)),
            pl.BlockSpec((DIFF_HEAD, 1), lambda b, hh: (0, 0)),
            head_cols(0), head_cols(h), head_cols(2 * h),
        ],
        out_specs=head_cols(0),
        out_shape=jax.ShapeDtypeStruct((batch * seq, DIFF_WIDTH), BF16),
        scratch_shapes=[pltpu.VMEM((seq // ATTN_KEYS, DIFF_HEAD + ATTN_ONES_ROWS, ATTN_KEYS), BF16),
                        pltpu.VMEM((n_sub, DIFF_HEAD + ATTN_ONES_ROWS, 2 * ATTN_SUB), F32),
                        pltpu.VMEM((2, n_sub, 2 * ATTN_SUB, DIFF_HEAD), BF16),
                        pltpu.VMEM((2, n_sub, ATTN_KEYS, 2 * ATTN_SUB), F32),
                        pltpu.VMEM((2, n_sub, 1, 2 * ATTN_SUB), F32)],
        compiler_params=_cparams(("arbitrary", "arbitrary")),
        name="diff_attention",
    )(lam_params, subln_g_col, qkv, qkv, qkv)


def _outproj2_kernel(ya_ref, yb_ref, x_ref, wa_ref, wb_ref, o_ref):
    o_ref[...] = x_ref[...] + _dot(ya_ref[...], wa_ref[...]) + _dot(yb_ref[...], wb_ref[...])


def _outproj2(ya, yb, x, wa, wb, *, tm):
    m, d = x.shape
    return pl.pallas_call(
        _outproj2_kernel,
        grid=(m // tm,),
        in_specs=[
            pl.BlockSpec((tm, ya.shape[1]), lambda i: (i, 0)),
            pl.BlockSpec((tm, yb.shape[1]), lambda i: (i, 0)),
            pl.BlockSpec((tm, d), lambda i: (i, 0)),
            pl.BlockSpec(wa.shape, lambda i: (0, 0)),
            pl.BlockSpec(wb.shape, lambda i: (0, 0)),
        ],
        out_specs=pl.BlockSpec((tm, d), lambda i: (i, 0)),
        out_shape=jax.ShapeDtypeStruct((m, d), F32),
        compiler_params=_cparams(("arbitrary",)),
        name="even_out_proj",
    )(ya, yb, x, wa, wb)


def _sgu_kernel(u_ref, v_ref, x_ref, lng_ref, lnb_ref, ws_ref, bs_ref, wo_ref, o_ref, vn_ref, gated_ref, *, tm):
    ch = GMLP_CHUNK
    ti = lax.broadcasted_iota(jnp.int32, (ch, ch), 0)
    tj = lax.broadcasted_iota(jnp.int32, (ch, ch), 1)
    causal = ti >= tj
    half = tm // 2
    for r0 in range(0, tm, half):
        part = slice(r0, r0 + half)
        v = v_ref[part, :].astype(F32)
        mu = jnp.mean(v, axis=-1, keepdims=True)
        vc = v - mu
        var = jnp.mean(vc * vc, axis=-1, keepdims=True)
        vn_ref[part, :] = (vc * lax.rsqrt(var + LN_EPS) * lng_ref[...] + lnb_ref[...]).astype(BF16)
        for g in range(GMLP_GROUPS):
            cols = slice(g * LANES, (g + 1) * LANES)
            wg = jnp.where(causal, ws_ref[g], 0.0).astype(BF16)
            bias = bs_ref[:, g:g + 1]
            for c in range(half // ch):
                rows = slice(r0 + c * ch, r0 + (c + 1) * ch)
                sv = _dot(wg, vn_ref[rows, cols]) + bias
                gated_ref[rows, cols] = (u_ref[rows, cols].astype(F32) * sv).astype(BF16)
        o_ref[part, :] = x_ref[part, :] + _dot(gated_ref[part, :], wo_ref[...])


def _sgu_outproj(u_v, x, ln_g, ln_b, w_s, b_s_t, w_out, *, tm):
    m, d = x.shape
    n_half = GMLP_WIDTH // d
    return pl.pallas_call(
        functools.partial(_sgu_kernel, tm=tm),
        grid=(m // tm,),
        in_specs=[
            pl.BlockSpec((tm, GMLP_WIDTH), lambda i: (i, 0)),
            pl.BlockSpec((tm, GMLP_WIDTH), lambda i: (i, n_half)),
            pl.BlockSpec((tm, d), lambda i: (i, 0)),
            pl.BlockSpec((1, GMLP_WIDTH), lambda i: (0, 0)),
            pl.BlockSpec((1, GMLP_WIDTH), lambda i: (0, 0)),
            pl.BlockSpec(w_s.shape, lambda i: (0, 0, 0)),
            pl.BlockSpec(b_s_t.shape, lambda i: (0, 0)),
            pl.BlockSpec(w_out.shape, lambda i: (0, 0)),
        ],
        out_specs=pl.BlockSpec((tm, d), lambda i: (i, 0)),
        out_shape=jax.ShapeDtypeStruct((m, d), F32),
        scratch_shapes=[pltpu.VMEM((tm, GMLP_WIDTH), BF16), pltpu.VMEM((tm, GMLP_WIDTH), BF16)],
        compiler_params=_cparams(("arbitrary",)),
        name="sgu_out_proj",
    )(u_v, u_v, x, ln_g, ln_b, w_s, b_s_t, w_out)


def _ffn_kernel(x_ref, g_ref, wg_ref, wu_ref, wd_ref, fg_ref, o_ref, h_ref, *, final):
    j = pl.program_id(1)

    def hidden_tile(rows):
        h = h_ref[rows, :]
        gate = _dot(h, wg_ref[...])
        up = _dot(h, wu_ref[...])
        act = gate * _sigmoid(gate) * up
        o_ref[rows, :] += _dot(act.astype(BF16), wd_ref[...].astype(BF16))

    @pl.when(j == 0)
    def _():
        for rows in _row_parts(x_ref.shape[0]):
            x = x_ref[rows, :]
            h_ref[rows, :] = _rms_norm_rows(x, g_ref[...], NORM_EPS).astype(BF16)
            o_ref[rows, :] = x
            hidden_tile(rows)

    @pl.when(j > 0)
    def _():
        hidden_tile(slice(None))

    if final:
        @pl.when(j == pl.num_programs(1) - 1)
        def _():
            o_ref[...] = _rms_norm_rows(o_ref[...], fg_ref[...], NORM_EPS)


def _ffn(x, g, w_gate, w_up, w_down_all, final_g, *, layer, tm, th, final):
    m, d = x.shape
    hid = w_gate.shape[1]
    return pl.pallas_call(
        functools.partial(_ffn_kernel, final=final),
        grid=(m // tm, hid // th),
        in_specs=[
            pl.BlockSpec((tm, d), lambda i, j: (i, 0)),
            pl.BlockSpec((1, d), lambda i, j: (0, 0)),
            pl.BlockSpec((d, th), lambda i, j: (0, j)),
            pl.BlockSpec((d, th), lambda i, j: (0, j)),
            pl.BlockSpec((None, th, d), lambda i, j: (layer, j, 0)),
            pl.BlockSpec((1, d), lambda i, j: (0, 0)),
        ],
        out_specs=pl.BlockSpec((tm, d), lambda i, j: (i, 0)),
        out_shape=jax.ShapeDtypeStruct((m, d), F32),
        scratch_shapes=[pltpu.VMEM((tm, d), BF16)],
        compiler_params=_cparams(("arbitrary", "arbitrary")),
        name="swiglu_ffn",
    )(x, g, w_gate, w_up, w_down_all, final_g)


def _pad_cols(a, width):
    return jnp.pad(a, ((0, 0), (0, width - a.shape[1])))


def _pad_rows(a, height):
    return jnp.pad(a, ((0, height - a.shape[0]), (0, 0)))


def _rwkv_col_layout(a):
    c = RWKV_WIDTH
    xw = a[:, 3 * c:3 * c + DECAY_LORA]
    xa = a[:, 3 * c + DECAY_LORA:3 * c + DECAY_LORA + AAA_LORA]
    xg = a[:, 3 * c + DECAY_LORA + AAA_LORA:]
    return jnp.concatenate(
        [a[:, :3 * c], _pad_cols(xw, LANES), _pad_cols(xa, LANES), _pad_cols(xg, 2 * LANES)], axis=1)


def _rwkv_weight_layout(w_in):
    c = RWKV_WIDTH
    out = jnp.zeros((w_in.shape[0], RW_PAD_COLS), BF16)
    out = lax.dynamic_update_slice(out, w_in[:, :3 * c].astype(BF16), (0, 0))
    starts = (3 * c, 3 * c + DECAY_LORA, 3 * c + DECAY_LORA + AAA_LORA, RWKV_COLS)
    for src0, src1, dst in zip(starts[:-1], starts[1:], (RW_XW, RW_XA, RW_XG)):
        out = lax.dynamic_update_slice(out, w_in[:, src0:src1].astype(BF16), (0, dst))
    return out


def _shift_cols_kernel(a_ref, b_ref, o_ref, *, shift):
    o_ref[...] = jnp.concatenate([a_ref[:, shift:], b_ref[:, :shift]], axis=1).astype(o_ref.dtype)


def _attention_weight_cols(w_in_all, layer):
    width = MXU_COLS
    first, shift = divmod(RWKV_COLS, width)
    d = w_in_all.shape[1]
    return pl.pallas_call(
        functools.partial(_shift_cols_kernel, shift=shift),
        grid=(DIFF_COLS // width,),
        in_specs=[pl.BlockSpec((None, d, width), lambda t: (layer, 0, first + t)),
                  pl.BlockSpec((None, d, width), lambda t: (layer, 0, first + t + 1))],
        out_specs=pl.BlockSpec((d, width), lambda t: (0, t)),
        out_shape=jax.ShapeDtypeStruct((d, DIFF_COLS), BF16),
        compiler_params=_cparams(("arbitrary",)),
        name="attention_weight_cols",
    )(w_in_all, w_in_all)


def _rope_tables(seq):
    inv = ROPE_THETA ** (-jnp.arange(0, DIFF_HALF, 2, dtype=F32) / DIFF_HALF)
    ang = jnp.arange(seq, dtype=F32)[:, None] * inv[None, :]
    cos, sin = jnp.cos(ang), jnp.sin(ang)
    reps = LANES // DIFF_HALF
    return (jnp.tile(jnp.concatenate([cos, cos], axis=1), (1, reps)),
            jnp.tile(jnp.concatenate([-sin, sin], axis=1), (1, reps)))


def _tile_sizes(m, seq):
    return dict(
        proj_tm=min(1024, seq), proj_tn=1024, rwkv_proj_tn=RW_PAD_COLS // 2,
        ffn_tm=min(1024, m), ffn_th=512,
        out_tm=min(512, m), sgu_tm=min(512, m),
        attn_sub=min(4, seq // ATTN_SUB),
    )


def kernel(x, mix_norm, ffn_norm, ffn_w_gate, ffn_w_up, ffn_w_down, ev_w_in, ev_mu, ev_w0, ev_w_dec_up, ev_a0,
           ev_w_a_up, ev_w_g_up, ev_k_k, ev_k_a, ev_r_k, ev_lnx_w, ev_lnx_b, ev_lam_q1, ev_lam_k1, ev_lam_q2,
           ev_lam_k2, ev_subln_g, ev_w_out, od_w_in, od_ln_g, od_ln_b, od_w_s, od_b_s, od_w_out, final_norm):
    batch, seq, d = x.shape
    m = batch * seq
    ts = _tile_sizes(m, seq)
    cos, sin = _rope_tables(seq)
    row = lambda a: a.reshape(1, -1).astype(F32)
    xf = x.reshape(m, d)
    for i in range(DEPTH):
        j = i // 2
        g_mix = row(mix_norm[i])
        if i % 2 == 0:
            lambda_init = 0.8 - 0.6 * math.exp(-0.3 * i)
            w_in = ev_w_in[j]
            w_rwkv = _rwkv_weight_layout(w_in)
            w_diff = _attention_weight_cols(ev_w_in.astype(F32), j)
            z = _norm_matmul(xf, g_mix, w_rwkv, tm=ts["proj_tm"], tn=ts["rwkv_proj_tn"], out_dtype=BF16)
            qkv = _norm_matmul_rope(xf, g_mix, w_diff, cos, sin, tm=ts["proj_tm"], tn=ts["proj_tn"], seq=seq)
            y_a = _rwkv_time_mix(
                z, _rwkv_col_layout(row(ev_mu[j])), row(ev_w0[j]),
                _pad_rows(ev_w_dec_up[j], LANES).astype(BF16), row(ev_a0[j]),
                _pad_rows(ev_w_a_up[j], LANES).astype(BF16), _pad_rows(ev_w_g_up[j], 2 * LANES).astype(BF16),
                row(ev_k_k[j]), row(ev_k_a[j]), row(ev_r_k[j]), row(ev_lnx_w[j]), row(ev_lnx_b[j]),
                batch=batch, seq=seq)
            lam_params = _pad_rows(_pad_cols(
                jnp.stack([ev_lam_q1[j], ev_lam_k1[j], ev_lam_q2[j], ev_lam_k2[j]]).astype(F32), LANES), 8)
            y_b = _diff_attention(qkv, lam_params, ev_subln_g[j].reshape(-1, 1).astype(F32), batch=batch, seq=seq,
                                  n_sub=ts["attn_sub"], lambda_init=lambda_init)
            w_out = ev_w_out[j].astype(BF16)
            xf = _outproj2(y_a, y_b, xf, w_out[:RWKV_WIDTH], w_out[RWKV_WIDTH:], tm=ts["out_tm"])
        else:
            u_v = _norm_matmul(xf, g_mix, od_w_in[j].astype(BF16), tm=ts["proj_tm"], tn=ts["proj_tn"],
                               out_dtype=BF16, gelu=True)
            xf = _sgu_outproj(u_v, xf, row(od_ln_g[j]), row(od_ln_b[j]), od_w_s[j].astype(F32),
                              od_b_s[j].T.astype(F32), od_w_out[j].astype(BF16), tm=ts["sgu_tm"])
        xf = _ffn(xf, row(ffn_norm[i]), ffn_w_gate[i].astype(BF16), ffn_w_up[i].astype(BF16),
                  ffn_w_down.astype(F32), row(final_norm), layer=i, tm=ts["ffn_tm"], th=ts["ffn_th"],
                  final=(i == DEPTH - 1))
    return xf.reshape(batch, seq, d)
```

```python
import functools
import math

import jax
import jax.numpy as jnp
from jax import lax
from jax.experimental import pallas as pl
from jax.experimental.pallas import tpu as pltpu

F32 = jnp.float32
BF16 = jnp.bfloat16

D_MODEL = 2048
DEPTH = 4
RWKV_WIDTH = D_MODEL // 2
RWKV_HEAD = 64
DECAY_LORA = 64
AAA_LORA = 64
GATE_LORA = 160
RWKV_COLS = 3 * RWKV_WIDTH + DECAY_LORA + AAA_LORA + GATE_LORA
RWKV_LNX_EPS = 64e-5
DIFF_WIDTH = D_MODEL - RWKV_WIDTH
DIFF_HEAD = 128
DIFF_HALF = DIFF_HEAD // 2
DIFF_HEADS = DIFF_WIDTH // DIFF_HEAD
DIFF_COLS = 3 * DIFF_WIDTH
ROPE_THETA = 10000.0
NEG_INF = -1e30
GMLP_WIDTH = D_MODEL
GMLP_CHUNK = 128
GMLP_GROUPS = 16
FFN_HIDDEN = 5632
NORM_EPS = 1e-6
SUBLN_EPS = 1e-5
LN_EPS = 1e-5

LANES = 128
MXU_COLS = 256
PROLOGUE_PARTS = 4
VMEM_LIMIT_BYTES = 60 * 1024 * 1024

RW_XW = 3 * RWKV_WIDTH
RW_XA = RW_XW + LANES
RW_XG = RW_XA + LANES
RW_PAD_COLS = RW_XG + 2 * LANES

RWKV_CHUNK = 64
RWKV_CHUNKS_PER_STEP = 8
ATTN_SUB = 128
ATTN_KEYS = 256
ATTN_ONES_ROWS = 16
ATTN_TRIP = 8
LOG2_E = 1.4426950408889634
PAIR = 2 * RWKV_HEAD


def _cparams(semantics):
    return pltpu.CompilerParams(dimension_semantics=semantics, vmem_limit_bytes=VMEM_LIMIT_BYTES)


def _dot(a, b):
    return jnp.dot(a, b, preferred_element_type=F32)


def _dot_nt(a, b):
    return lax.dot_general(a, b, (((1,), (1,)), ((), ())), preferred_element_type=F32)


def _rms_norm_rows(x, g, eps):
    ms = jnp.mean(x * x, axis=-1, keepdims=True)
    return x * lax.rsqrt(ms + eps) * g


def _row_parts(tm):
    part = tm // PROLOGUE_PARTS
    return [slice(r0, r0 + part) for r0 in range(0, tm, part)]


def _norm_matmul_kernel(x_ref, g_ref, w_ref, o_ref, h_ref, *, tn, gelu):
    j = pl.program_id(1)

    def columns(rows):
        for n0 in range(0, tn, MXU_COLS):
            acc = _dot(h_ref[rows, :], w_ref[:, n0:n0 + MXU_COLS])
            if gelu:
                acc = 0.5 * acc * (1.0 + lax.erf(acc * math.sqrt(0.5)))
            o_ref[rows, n0:n0 + MXU_COLS] = acc.astype(o_ref.dtype)

    @pl.when(j == 0)
    def _():
        for rows in _row_parts(x_ref.shape[0]):
            h_ref[rows, :] = _rms_norm_rows(x_ref[rows, :], g_ref[...], NORM_EPS).astype(BF16)
            columns(rows)

    @pl.when(j > 0)
    def _():
        columns(slice(None))


def _norm_matmul(x, g, w, *, tm, tn, out_dtype, gelu=False, layer=None):
    m, k = x.shape
    n = w.shape[-1]
    if layer is None:
        w_spec = pl.BlockSpec((k, tn), lambda i, j: (0, j))
    else:
        w_spec = pl.BlockSpec((None, k, tn), lambda i, j: (layer, 0, j))
    return pl.pallas_call(
        functools.partial(_norm_matmul_kernel, tn=tn, gelu=gelu),
        grid=(m // tm, n // tn),
        in_specs=[
            pl.BlockSpec((tm, k), lambda i, j: (i, 0)),
            pl.BlockSpec((1, k), lambda i, j: (0, 0)),
            w_spec,
        ],
        out_specs=pl.BlockSpec((tm, tn), lambda i, j: (i, j)),
        out_shape=jax.ShapeDtypeStruct((m, n), out_dtype),
        scratch_shapes=[pltpu.VMEM((tm, k), BF16)],
        compiler_params=_cparams(("arbitrary", "arbitrary")),
        name="norm_matmul_gelu" if gelu else "norm_matmul",
    )(x, g, w)


def _norm_matmul_rope_kernel(x_ref, g_ref, w_ref, cos_ref, sin_ref, o_ref, h_ref, *, tn, q_tiles, rope_tiles):
    j = pl.program_id(1)

    def columns(rows, epilogue):
        for n0 in range(0, tn, MXU_COLS):
            acc = _dot(h_ref[rows, :], w_ref[:, n0:n0 + MXU_COLS])
            o_ref[rows, n0:n0 + MXU_COLS] = epilogue(rows, acc).astype(o_ref.dtype)

    def rope(rows, acc):
        reps = MXU_COLS // LANES
        c = jnp.concatenate([cos_ref[rows, :]] * reps, axis=1)
        s = jnp.concatenate([sin_ref[rows, :]] * reps, axis=1)
        lane = lax.broadcasted_iota(jnp.int32, acc.shape, 1)
        partner = jnp.where((lane & (DIFF_HALF // 2)) == 0,
                            pltpu.roll(acc, MXU_COLS - DIFF_HALF // 2, 1),
                            pltpu.roll(acc, DIFF_HALF // 2, 1))
        scale = jnp.where(j < q_tiles, DIFF_HALF ** -0.5 * LOG2_E, 1.0).astype(F32)
        return (acc * c + partner * s) * scale

    @pl.when(j == 0)
    def _():
        for rows in _row_parts(x_ref.shape[0]):
            h_ref[rows, :] = _rms_norm_rows(x_ref[rows, :], g_ref[...], NORM_EPS).astype(BF16)
            columns(rows, rope)

    @pl.when(jnp.logical_and(j > 0, j < rope_tiles))
    def _():
        columns(slice(None), rope)

    @pl.when(j >= rope_tiles)
    def _():
        columns(slice(None), lambda rows, acc: acc)


def _norm_matmul_rope(x, g, w, cos, sin, *, tm, tn, seq):
    m, k = x.shape
    n = w.shape[1]
    q_tiles = DIFF_WIDTH // tn
    t_tiles = seq // tm
    return pl.pallas_call(
        functools.partial(_norm_matmul_rope_kernel, tn=tn, q_tiles=q_tiles, rope_tiles=2 * q_tiles),
        grid=(m // tm, n // tn),
        in_specs=[
            pl.BlockSpec((tm, k), lambda i, j: (i, 0)),
            pl.BlockSpec((1, k), lambda i, j: (0, 0)),
            pl.BlockSpec((k, tn), lambda i, j: (0, j)),
            pl.BlockSpec((tm, LANES), lambda i, j: (i % t_tiles, 0)),
            pl.BlockSpec((tm, LANES), lambda i, j: (i % t_tiles, 0)),
        ],
        out_specs=pl.BlockSpec((tm, tn), lambda i, j: (i, j)),
        out_shape=jax.ShapeDtypeStruct((m, n), BF16),
        scratch_shapes=[pltpu.VMEM((tm, k), BF16)],
        compiler_params=_cparams(("arbitrary", "arbitrary")),
        name="norm_matmul_rope",
    )(x, g, w, cos, sin)


def _sigmoid(y):
    return 1.0 / (1.0 + jnp.exp(-y))


def _split_bf16(x, parts):
    out = []
    rem = x
    for _ in range(parts):
        p = rem.astype(BF16)
        out.append(p)
        rem = rem - p.astype(F32)
    return out


def _rwkv_kernel(z_ref, mu_ref, w0_ref, wdec_ref, a0_ref, wa_ref, wg_ref, kk_ref, ka_ref, rk_ref,
                 lw_ref, lb_ref, o_ref, s_ref, prev_ref):
    L = RWKV_CHUNK
    C = RWKV_WIDTH
    n_pairs = C // PAIR
    n_batch = z_ref.shape[0]

    @pl.when(pl.program_id(0) == 0)
    def _():
        s_ref[...] = jnp.zeros_like(s_ref)
        prev_ref[...] = jnp.zeros_like(prev_ref)

    ti = lax.broadcasted_iota(jnp.int32, (L, L), 0)
    tj = lax.broadcasted_iota(jnp.int32, (L, L), 1)
    tril_ones = jnp.where(ti >= tj, 1.0, 0.0).astype(BF16)

    def prepare(bi, row0):
        z = z_ref[bi, pl.ds(row0, L), :].astype(F32)
        row = lax.broadcasted_iota(jnp.int32, z.shape, 0)
        z_prev = jnp.where(row == 0, prev_ref[bi, 0:1, :], pltpu.roll(z, 1, 0))
        prev_ref[bi, 0:1, :] = z[L - 1:L, :]
        zs = z + (z_prev - z) * mu_ref[...]

        r = zs[:, 0:C]
        k = zs[:, C:2 * C]
        v = zs[:, 2 * C:3 * C]
        xw = zs[:, RW_XW:RW_XA]
        xa = zs[:, RW_XA:RW_XG]
        xg = zs[:, RW_XG:RW_PAD_COLS]

        y_dec = w0_ref[...] + _dot(jnp.tanh(xw).astype(BF16), wdec_ref[...])
        logd = -math.exp(-0.5) * _sigmoid(y_dec)
        a_lr = _sigmoid(a0_ref[...] + _dot(xa.astype(BF16), wa_ref[...]))
        gate = _dot(_sigmoid(xg).astype(BF16), wg_ref[...])
        kk_raw = k * kk_ref[...]
        k2 = k * (1.0 + (a_lr - 1.0) * ka_ref[...])
        rkk = r * k2 * rk_ref[...]
        cum = sum(_dot(tril_ones, p) for p in _split_bf16(logd, 2))
        return r, k2, v, logd, a_lr, gate, kk_raw, rkk, cum

    rr = lax.broadcasted_iota(jnp.int32, (PAIR, PAIR), 0)
    cc = lax.broadcasted_iota(jnp.int32, (PAIR, PAIR), 1)
    same_head = (rr // RWKV_HEAD) == (cc // RWKV_HEAD)
    strict = jnp.logical_and(same_head, rr > cc)
    incl = jnp.logical_and(same_head, rr >= cc)
    head_ones = jnp.where(same_head, 1.0, 0.0).astype(BF16)
    eye = jnp.where(rr == cc, 1.0, 0.0).astype(F32)
    first_head = lax.broadcasted_iota(jnp.int32, (L, PAIR), 1) < RWKV_HEAD

    def head_sum(x):
        return _dot(x.astype(BF16), head_ones)

    def stack(x):
        return jnp.concatenate([jnp.where(first_head, x, 0.0), jnp.where(first_head, 0.0, x)], axis=0)

    n_steps = int(math.log2(L)) - 1
    idx = range(n_pairs)
    sls = [slice(p * PAIR, (p + 1) * PAIR) for p in idx]

    def elementwise(prep, p):
        r, k2, v, logd, a_lr, _, kk_raw, rkk, cum = prep
        sl = sls[p]
        kk_p = kk_raw[:, sl]
        kk_n = kk_p * lax.rsqrt(jnp.maximum(head_sum(kk_p * kk_p), 1e-24))
        bonus = head_sum(rkk[:, sl])
        a_p = -kk_n
        b_p = kk_n * a_lr[:, sl]
        r_p, k_p = r[:, sl], k2[:, sl]
        cu = cum[:, sl]
        c_end = cu[L - 1:L, :]
        e_pos = jnp.exp(cu)
        e_neg = jnp.exp(-cu)
        e_prev = jnp.exp(cu - logd[:, sl])
        e_end = jnp.exp(c_end - cu)
        lhs = jnp.concatenate([stack(a_p * e_prev), stack(r_p * e_pos)], axis=0).astype(BF16)
        rhs = jnp.concatenate([stack(b_p * e_neg), stack(k_p * e_neg)], axis=0).astype(BF16)
        bk_end = jnp.concatenate([stack(b_p * e_end), stack(k_p * e_end)], axis=0).astype(BF16)
        return lhs, rhs, bk_end, stack(v[:, sl]), jnp.exp(c_end), bonus

    def stages(bi, row0, prep, elems, filler):
        v, gate = prep[2], prep[5]
        lhs, rhs, bk_end, v_st, d_end, bonus_s = zip(*elems)
        pm = [_dot_nt(lhs[i], rhs[i]) for i in idx]
        filler()
        s_old = [s_ref[bi, p] for p in idx]
        sh = [_dot_nt(lhs[i], s_old[i].astype(BF16)) for i in idx]
        filler()
        p_ab = [jnp.where(strict, pm[i][:PAIR, :PAIR], 0.0) for i in idx]
        p_ak = [jnp.where(strict, pm[i][:PAIR, PAIR:], 0.0).astype(BF16) for i in idx]
        p_rbk = [jnp.concatenate([jnp.where(incl, pm[i][PAIR:, :PAIR], 0.0),
                                  jnp.where(incl, pm[i][PAIR:, PAIR:], 0.0)], axis=1).astype(BF16) for i in idx]
        akv = [_dot(p_ak[i], v_st[i].astype(BF16)) for i in idx]
        filler()

        t_inv = [eye + p_ab[i] for i in idx]
        xb = [p_ab[i].astype(BF16) for i in idx]
        xb = [_dot(xb[i], xb[i]).astype(BF16) for i in idx]
        filler()
        for it in range(n_steps):
            if it + 1 < n_steps:
                both = [_dot(xb[i], jnp.concatenate([xb[i], t_inv[i].astype(BF16)], axis=1)) for i in idx]
                xb = [both[i][:, :PAIR].astype(BF16) for i in idx]
                t_inv = [t_inv[i] + both[i][:, PAIR:] for i in idx]
            else:
                t_inv = [t_inv[i] + _dot(xb[i], t_inv[i].astype(BF16)) for i in idx]
            filler()

        u_st = [_dot(t_inv[i].astype(BF16), (sh[i][:PAIR] + akv[i]).astype(BF16)) for i in idx]
        filler()
        uv = [jnp.concatenate([u_st[i], v_st[i]], axis=0) for i in idx]
        y_st = [sh[i][PAIR:] + _dot(p_rbk[i], uv[i].astype(BF16)) for i in idx]
        filler()
        s_upd = [_dot(uv[i].T.astype(BF16), bk_end[i]) for i in idx]
        for p in idx:
            s_ref[bi, p] = s_old[p] * d_end[p] + s_upd[p]

        y = [y_st[i][:L] + y_st[i][L:] for i in idx]
        mean = [head_sum(y[i]) * (1.0 / RWKV_HEAD) for i in idx]
        yc = [y[i] - mean[i] for i in idx]
        var = [head_sum(yc[i] * yc[i]) * (1.0 / RWKV_HEAD) for i in idx]
        for i in idx:
            sl = sls[i]
            yn = yc[i] * lax.rsqrt(var[i] + RWKV_LNX_EPS) * lw_ref[:, sl] + lb_ref[:, sl]
            o_ref[bi, pl.ds(row0, L), sl] = ((yn + bonus_s[i] * v[:, sl]) * gate[:, sl]).astype(o_ref.dtype)

    def chunk(ci, carry):
        row0 = pl.multiple_of(ci * L, L)
        prepared = [prepare(bi, row0) for bi in range(n_batch)]
        elems = [elementwise(prepared[0], p) for p in idx]
        for bi in range(n_batch):
            nxt_elems, queue = [], []
            if bi + 1 < n_batch:
                queue = [functools.partial(lambda row, p: nxt_elems.append(elementwise(prepared[row], p)), bi + 1, p)
                         for p in idx]

            def filler():
                if queue:
                    queue.pop(0)()

            stages(bi, row0, prepared[bi], elems, filler)
            while queue:
                queue.pop(0)()
            elems = nxt_elems
        return carry

    lax.fori_loop(0, z_ref.shape[1] // L, chunk, 0)


def _rwkv_time_mix(z, mu, w0, wdec, a0, wa, wg, k_k, k_a, r_k, lnx_w, lnx_b, *, batch, seq):
    rows = min(RWKV_CHUNK * RWKV_CHUNKS_PER_STEP, seq)
    n_pairs = RWKV_WIDTH // PAIR
    full = lambda shape: pl.BlockSpec(shape, lambda c: (0,) * len(shape))
    out = pl.pallas_call(
        _rwkv_kernel,
        grid=(seq // rows,),
        in_specs=[
            pl.BlockSpec((batch, rows, RW_PAD_COLS), lambda c: (0, c, 0)),
            full((1, RW_PAD_COLS)),
            full((1, RWKV_WIDTH)), full((LANES, RWKV_WIDTH)),
            full((1, RWKV_WIDTH)), full((LANES, RWKV_WIDTH)),
            full((2 * LANES, RWKV_WIDTH)),
            full((1, RWKV_WIDTH)), full((1, RWKV_WIDTH)), full((1, RWKV_WIDTH)),
            full((1, RWKV_WIDTH)), full((1, RWKV_WIDTH)),
        ],
        out_specs=pl.BlockSpec((batch, rows, RWKV_WIDTH), lambda c: (0, c, 0)),
        out_shape=jax.ShapeDtypeStruct((batch, seq, RWKV_WIDTH), BF16),
        scratch_shapes=[pltpu.VMEM((batch, n_pairs, PAIR, PAIR), F32), pltpu.VMEM((batch, 8, RW_PAD_COLS), F32)],
        compiler_params=_cparams(("arbitrary",)),
        name="rwkv7_time_mix",
    )(z.reshape(batch, seq, RW_PAD_COLS), mu, w0, wdec, a0, wa, wg, k_k, k_a, r_k, lnx_w, lnx_b)
    return out.reshape(batch * seq, RWKV_WIDTH)


def _attn_kernel(lam_ref, g_ref, q_ref, k_ref, v_ref, o_ref, vt_ref, acc_ref, qq_ref, s_ref, smax_ref, *, seq,
                 n_sub, lambda_init):
    sb = ATTN_SUB
    two = 2 * sb
    kb = ATTN_KEYS
    qb = n_sub * sb
    steps_per_q = qb // kb
    nq = seq // qb

    def transpose_block(t, carry):
        start = pl.multiple_of(t * kb, kb)
        vt_ref[t, :DIFF_HEAD, :] = v_ref[pl.ds(start, kb), :].astype(F32).T.astype(BF16)
        vt_ref[t, DIFF_HEAD:, :] = jnp.ones((ATTN_ONES_ROWS, kb), BF16)
        return carry
    lax.fori_loop(0, seq // kb, transpose_block, 0)

    lane = lax.broadcasted_iota(jnp.int32, (sb, DIFF_HEAD), 1)
    key_idx = lax.broadcasted_iota(jnp.int32, (kb, two), 0)
    qry_idx = lax.broadcasted_iota(jnp.int32, (kb, two), 1) & (sb - 1)
    lam_p = lam_ref[...]
    lam = (jnp.exp(jnp.sum(lam_p[0:1] * lam_p[1:2], axis=-1, keepdims=True))
           - jnp.exp(jnp.sum(lam_p[2:3] * lam_p[3:4], axis=-1, keepdims=True)) + lambda_init)

    def scores(qslot, c, j):
        start = pl.multiple_of(j * kb, kb)
        return _dot_nt(k_ref[pl.ds(start, kb), :], qq_ref[qslot, c])

    def softmax_pv(c, j, st, st_max, m):
        m_new = jnp.maximum(m, st_max)
        alpha = jnp.exp2(m - m_new)
        pr = jnp.exp2(st - m_new)
        acc_ref[c] = alpha * acc_ref[c] + _dot(vt_ref[j], pr.astype(BF16))
        return m_new

    def stage_scores(qslot, slot, c, j):
        st = scores(qslot, c, j)
        s_ref[slot, c] = st
        smax_ref[slot, c] = jnp.max(st, axis=0, keepdims=True)

    def open_block(i):
        qslot = i % 2
        for c in range(n_sub):
            row = pl.multiple_of(i * qb + c * sb, sb)
            qc = q_ref[pl.ds(row, sb), :]
            zero = jnp.zeros_like(qc)
            qq_ref[qslot, c] = jnp.concatenate([jnp.where(lane < DIFF_HALF, qc, zero),
                                                jnp.where(lane < DIFF_HALF, zero, qc)], axis=0)
        for c in range(n_sub):
            stage_scores(qslot, 0, c, 0)

    def q_block(i, carry):
        qslot = i % 2
        for c in range(n_sub):
            acc_ref[c] = jnp.zeros((DIFF_HEAD + ATTN_ONES_ROWS, two), F32)

        def full_steps(j0, n_steps, ms):
            ms = list(ms)
            for h in range(n_steps):
                for c in range(n_sub):
                    stage_scores(qslot, 1 - h % 2, c, j0 + h + 1)
                for c in range(n_sub):
                    ms[c] = softmax_pv(c, j0 + h, s_ref[h % 2, c], smax_ref[h % 2, c], ms[c])
            return tuple(ms)

        init = tuple(jnp.full((1, two), NEG_INF, F32) for _ in range(n_sub))
        n_full = i * steps_per_q
        n_short = (n_full % ATTN_TRIP) // 2
        ms = lax.fori_loop(0, n_short, lambda t, cr: full_steps(2 * t, 2, cr), init)
        ms = list(lax.fori_loop(0, n_full // ATTN_TRIP,
                                lambda t, cr: full_steps(2 * n_short + ATTN_TRIP * t, ATTN_TRIP, cr), ms))

        first_diag = n_full
        needed = lambda s, c: s * kb < (c + 1) * sb
        tail = {(s, c): scores(qslot, c, first_diag + s)
                for s in range(1, steps_per_q) for c in range(n_sub) if needed(s, c)}
        for s in range(steps_per_q):
            for c in range(n_sub):
                if not needed(s, c):
                    continue
                st = s_ref[0, c] if s == 0 else tail[(s, c)]
                if (s + 1) * kb - 1 > c * sb:
                    st = jnp.where(key_idx + s * kb <= qry_idx + c * sb, st, NEG_INF)
                    st_max = jnp.max(st, axis=0, keepdims=True)
                else:
                    st_max = smax_ref[0, c] if s == 0 else jnp.max(st, axis=0, keepdims=True)
                ms[c] = softmax_pv(c, first_diag + s, st, st_max, ms[c])

        open_block(jnp.minimum(i + 1, nq - 1))

        for c in range(n_sub):
            acc = acc_ref[c]
            o = acc[:DIFF_HEAD] / acc[DIFF_HEAD:DIFF_HEAD + 1]
            d = o[:, :sb] - lam * o[:, sb:]
            ms_d = jnp.mean(d * d, axis=0, keepdims=True)
            d = d * lax.rsqrt(ms_d + SUBLN_EPS) * g_ref[...] * (1.0 - lambda_init)
            row = pl.multiple_of(i * qb + c * sb, sb)
            o_ref[pl.ds(row, sb), :] = d.T.astype(o_ref.dtype)
        return carry

    open_block(0)
    lax.fori_loop(0, nq, q_block, 0)


def _diff_attention(qkv, lam_params, subln_g_col, *, batch, seq, n_sub, lambda_init):
    qb = n_sub * ATTN_SUB
    assert qb % (2 * ATTN_KEYS) == 0 and seq % qb == 0
    h = DIFF_HEADS
    head_cols = lambda off: pl.BlockSpec((seq, DIFF_HEAD), lambda b, hh: (b, off + hh))
    return pl.pallas_call(
        functools.partial(_attn_kernel, seq=seq, n_sub=n_sub, lambda_init=lambda_init),
        grid=(batch, h),
        in_specs=[
            pl.BlockSpec((8, LANES), lambda b, hh: (0, 0)),
            pl.BlockSpec((DIFF_HEAD, 1), lambda b, hh: (0, 0)),
            head_cols(0), head_cols(h), head_cols(2 * h),
        ],
        out_specs=head_cols(0),
        out_shape=jax.ShapeDtypeStruct((batch * seq, DIFF_WIDTH), BF16),
        scratch_shapes=[pltpu.VMEM((seq // ATTN_KEYS, DIFF_HEAD + ATTN_ONES_ROWS, ATTN_KEYS), BF16),
                        pltpu.VMEM((n_sub, DIFF_HEAD + ATTN_ONES_ROWS, 2 * ATTN_SUB), F32),
                        pltpu.VMEM((2, n_sub, 2 * ATTN_SUB, DIFF_HEAD), BF16),
                        pltpu.VMEM((2, n_sub, ATTN_KEYS, 2 * ATTN_SUB), F32),
                        pltpu.VMEM((2, n_sub, 1, 2 * ATTN_SUB), F32)],
        compiler_params=_cparams(("arbitrary", "arbitrary")),
        name="diff_attention",
    )(lam_params, subln_g_col, qkv, qkv, qkv)


def _outproj2_kernel(ya_ref, yb_ref, x_ref, wa_ref, wb_ref, o_ref):
    o_ref[...] = x_ref[...] + _dot(ya_ref[...], wa_ref[...]) + _dot(yb_ref[...], wb_ref[...])


def _outproj2(ya, yb, x, w_all, *, layer, tm):
    m, d = x.shape
    half = w_all.shape[1] // 2
    return pl.pallas_call(
        _outproj2_kernel,
        grid=(m // tm,),
        in_specs=[
            pl.BlockSpec((tm, ya.shape[1]), lambda i: (i, 0)),
            pl.BlockSpec((tm, yb.shape[1]), lambda i: (i, 0)),
            pl.BlockSpec((tm, d), lambda i: (i, 0)),
            pl.BlockSpec((None, half, d), lambda i: (layer, 0, 0)),
            pl.BlockSpec((None, half, d), lambda i: (layer, 1, 0)),
        ],
        out_specs=pl.BlockSpec((tm, d), lambda i: (i, 0)),
        out_shape=jax.ShapeDtypeStruct((m, d), F32),
        compiler_params=_cparams(("arbitrary",)),
        name="even_out_proj",
    )(ya, yb, x, w_all, w_all)


def _sgu_kernel(u_ref, v_ref, x_ref, lng_ref, lnb_ref, ws_ref, bs_ref, wo_ref, o_ref, vn_ref, gated_ref, *, tm):
    ch = GMLP_CHUNK
    ti = lax.broadcasted_iota(jnp.int32, (ch, ch), 0)
    tj = lax.broadcasted_iota(jnp.int32, (ch, ch), 1)
    causal = ti >= tj
    half = tm // 2
    for r0 in range(0, tm, half):
        part = slice(r0, r0 + half)
        v = v_ref[part, :].astype(F32)
        mu = jnp.mean(v, axis=-1, keepdims=True)
        vc = v - mu
        var = jnp.mean(vc * vc, axis=-1, keepdims=True)
        vn_ref[part, :] = (vc * lax.rsqrt(var + LN_EPS) * lng_ref[...] + lnb_ref[...]).astype(BF16)
        for g in range(GMLP_GROUPS):
            cols = slice(g * LANES, (g + 1) * LANES)
            wg = jnp.where(causal, ws_ref[g], 0.0).astype(BF16)
            bias = bs_ref[:, g:g + 1]
            for c in range(half // ch):
                rows = slice(r0 + c * ch, r0 + (c + 1) * ch)
                sv = _dot(wg, vn_ref[rows, cols]) + bias
                gated_ref[rows, cols] = (u_ref[rows, cols].astype(F32) * sv).astype(BF16)
        o_ref[part, :] = x_ref[part, :] + _dot(gated_ref[part, :], wo_ref[...])


def _sgu_outproj(u_v, x, ln_g, ln_b, w_s_all, b_s_t, w_out_all, *, layer, tm):
    m, d = x.shape
    n_half = GMLP_WIDTH // d
    return pl.pallas_call(
        functools.partial(_sgu_kernel, tm=tm),
        grid=(m // tm,),
        in_specs=[
            pl.BlockSpec((tm, GMLP_WIDTH), lambda i: (i, 0)),
            pl.BlockSpec((tm, GMLP_WIDTH), lambda i: (i, n_half)),
            pl.BlockSpec((tm, d), lambda i: (i, 0)),
            pl.BlockSpec((1, GMLP_WIDTH), lambda i: (0, 0)),
            pl.BlockSpec((1, GMLP_WIDTH), lambda i: (0, 0)),
            pl.BlockSpec((None,) + w_s_all.shape[1:], lambda i: (layer, 0, 0, 0)),
            pl.BlockSpec(b_s_t.shape, lambda i: (0, 0)),
            pl.BlockSpec((None,) + w_out_all.shape[1:], lambda i: (layer, 0, 0)),
        ],
        out_specs=pl.BlockSpec((tm, d), lambda i: (i, 0)),
        out_shape=jax.ShapeDtypeStruct((m, d), F32),
        scratch_shapes=[pltpu.VMEM((tm, GMLP_WIDTH), BF16), pltpu.VMEM((tm, GMLP_WIDTH), BF16)],
        compiler_params=_cparams(("arbitrary",)),
        name="sgu_out_proj",
    )(u_v, u_v, x, ln_g, ln_b, w_s_all, b_s_t, w_out_all)


def _ffn_kernel(x_ref, g_ref, wg_ref, wu_ref, wd_ref, fg_ref, o_ref, h_ref, *, final):
    j = pl.program_id(1)

    def hidden_tile(rows):
        h = h_ref[rows, :]
        gate = _dot(h, wg_ref[...])
        up = _dot(h, wu_ref[...])
        act = gate * _sigmoid(gate) * up
        o_ref[rows, :] += _dot(act.astype(BF16), wd_ref[...].astype(BF16))

    @pl.when(j == 0)
    def _():
        for rows in _row_parts(x_ref.shape[0]):
            x = x_ref[rows, :]
            h_ref[rows, :] = _rms_norm_rows(x, g_ref[...], NORM_EPS).astype(BF16)
            o_ref[rows, :] = x
            hidden_tile(rows)

    @pl.when(j > 0)
    def _():
        hidden_tile(slice(None))

    if final:
        @pl.when(j == pl.num_programs(1) - 1)
        def _():
            o_ref[...] = _rms_norm_rows(o_ref[...], fg_ref[...], NORM_EPS)


def _ffn(x, g, w_gate_all, w_up_all, w_down_all, final_g, *, layer, tm, th, final):
    m, d = x.shape
    hid = w_gate_all.shape[2]
    return pl.pallas_call(
        functools.partial(_ffn_kernel, final=final),
        grid=(m // tm, hid // th),
        in_specs=[
            pl.BlockSpec((tm, d), lambda i, j: (i, 0)),
            pl.BlockSpec((1, d), lambda i, j: (0, 0)),
            pl.BlockSpec((None, d, th), lambda i, j: (layer, 0, j)),
            pl.BlockSpec((None, d, th), lambda i, j: (layer, 0, j)),
            pl.BlockSpec((None, th, d), lambda i, j: (layer, j, 0)),
            pl.BlockSpec((1, d), lambda i, j: (0, 0)),
        ],
        out_specs=pl.BlockSpec((tm, d), lambda i, j: (i, 0)),
        out_shape=jax.ShapeDtypeStruct((m, d), F32),
        scratch_shapes=[pltpu.VMEM((tm, d), BF16)],
        compiler_params=_cparams(("arbitrary", "arbitrary")),
        name="swiglu_ffn",
    )(x, g, w_gate_all, w_up_all, w_down_all, final_g)


def _pad_cols(a, width):
    return jnp.pad(a, ((0, 0), (0, width - a.shape[1])))


def _pad_rows(a, height):
    return jnp.pad(a, ((0, height - a.shape[0]), (0, 0)))


def _rwkv_col_layout(a):
    c = RWKV_WIDTH
    xw = a[:, 3 * c:3 * c + DECAY_LORA]
    xa = a[:, 3 * c + DECAY_LORA:3 * c + DECAY_LORA + AAA_LORA]
    xg = a[:, 3 * c + DECAY_LORA + AAA_LORA:]
    return jnp.concatenate(
        [a[:, :3 * c], _pad_cols(xw, LANES), _pad_cols(xa, LANES), _pad_cols(xg, 2 * LANES)], axis=1)


def _rwkv_weight_layout(w_in):
    c = RWKV_WIDTH
    out = jnp.zeros((w_in.shape[0], RW_PAD_COLS), BF16)
    out = lax.dynamic_update_slice(out, w_in[:, :3 * c].astype(BF16), (0, 0))
    starts = (3 * c, 3 * c + DECAY_LORA, 3 * c + DECAY_LORA + AAA_LORA, RWKV_COLS)
    for src0, src1, dst in zip(starts[:-1], starts[1:], (RW_XW, RW_XA, RW_XG)):
        out = lax.dynamic_update_slice(out, w_in[:, src0:src1].astype(BF16), (0, dst))
    return out


def _rope_tables(seq):
    inv = ROPE_THETA ** (-jnp.arange(0, DIFF_HALF, 2, dtype=F32) / DIFF_HALF)
    ang = jnp.arange(seq, dtype=F32)[:, None] * inv[None, :]
    cos, sin = jnp.cos(ang), jnp.sin(ang)
    reps = LANES // DIFF_HALF
    return (jnp.tile(jnp.concatenate([cos, cos], axis=1), (1, reps)),
            jnp.tile(jnp.concatenate([-sin, sin], axis=1), (1, reps)))


def _tile_sizes(m, seq):
    return dict(
        proj_tm=min(1024, seq), proj_tn=1024, rwkv_proj_tn=RW_PAD_COLS // 2,
        ffn_tm=min(1024, m), ffn_th=512,
        out_tm=min(512, m), sgu_tm=min(512, m),
        attn_sub=min(4, seq // ATTN_SUB),
    )


def kernel(x, mix_norm, ffn_norm, ffn_w_gate, ffn_w_up, ffn_w_down, ev_w_in, ev_mu, ev_w0, ev_w_dec_up, ev_a0,
           ev_w_a_up, ev_w_g_up, ev_k_k, ev_k_a, ev_r_k, ev_lnx_w, ev_lnx_b, ev_lam_q1, ev_lam_k1, ev_lam_q2,
           ev_lam_k2, ev_subln_g, ev_w_out, od_w_in, od_ln_g, od_ln_b, od_w_s, od_b_s, od_w_out, final_norm):
    batch, seq, d = x.shape
    m = batch * seq
    ts = _tile_sizes(m, seq)
    cos, sin = _rope_tables(seq)
    row = lambda a: a.reshape(1, -1).astype(F32)
    xf = x.reshape(m, d)
    ffn_w_gate_bf, ffn_w_up_bf, ffn_w_down_f32 = ffn_w_gate.astype(BF16), ffn_w_up.astype(BF16), ffn_w_down.astype(F32)
    ev_w_out_bf, od_w_in_bf, od_w_out_bf = ev_w_out.astype(BF16), od_w_in.astype(BF16), od_w_out.astype(BF16)
    od_w_s_f32 = od_w_s.astype(F32)
    for i in range(DEPTH):
        j = i // 2
        g_mix = row(mix_norm[i])
        if i % 2 == 0:
            lambda_init = 0.8 - 0.6 * math.exp(-0.3 * i)
            w_in = ev_w_in[j]
            w_rwkv = _rwkv_weight_layout(w_in)
            w_diff = w_in[:, RWKV_COLS:].astype(BF16)
            z = _norm_matmul(xf, g_mix, w_rwkv, tm=ts["proj_tm"], tn=ts["rwkv_proj_tn"], out_dtype=BF16)
            qkv = _norm_matmul_rope(xf, g_mix, w_diff, cos, sin, tm=ts["proj_tm"], tn=ts["proj_tn"], seq=seq)
            y_a = _rwkv_time_mix(
                z, _rwkv_col_layout(row(ev_mu[j])), row(ev_w0[j]),
                _pad_rows(ev_w_dec_up[j], LANES).astype(BF16), row(ev_a0[j]),
                _pad_rows(ev_w_a_up[j], LANES).astype(BF16), _pad_rows(ev_w_g_up[j], 2 * LANES).astype(BF16),
                row(ev_k_k[j]), row(ev_k_a[j]), row(ev_r_k[j]), row(ev_lnx_w[j]), row(ev_lnx_b[j]),
                batch=batch, seq=seq)
            lam_params = _pad_rows(_pad_cols(
                jnp.stack([ev_lam_q1[j], ev_lam_k1[j], ev_lam_q2[j], ev_lam_k2[j]]).astype(F32), LANES), 8)
            y_b = _diff_attention(qkv, lam_params, ev_subln_g[j].reshape(-1, 1).astype(F32), batch=batch, seq=seq,
                                  n_sub=ts["attn_sub"], lambda_init=lambda_init)
            xf = _outproj2(y_a, y_b, xf, ev_w_out_bf, layer=j, tm=ts["out_tm"])
        else:
            u_v = _norm_matmul(xf, g_mix, od_w_in_bf, tm=ts["proj_tm"], tn=ts["proj_tn"],
                               out_dtype=BF16, gelu=True, layer=j)
            xf = _sgu_outproj(u_v, xf, row(od_ln_g[j]), row(od_ln_b[j]), od_w_s_f32,
                              od_b_s[j].T.astype(F32), od_w_out_bf, layer=j, tm=ts["sgu_tm"])
        xf = _ffn(xf, row(ffn_norm[i]), ffn_w_gate_bf, ffn_w_up_bf, ffn_w_down_f32, row(final_norm),
                  layer=i, tm=ts["ffn_tm"], th=ts["ffn_th"], final=(i == DEPTH - 1))
    return xf.reshape(batch, seq, d)
```

```python
import functools
import math

import jax
import jax.numpy as jnp
from jax import lax
from jax.experimental import pallas as pl
from jax.experimental.pallas import tpu as pltpu

F32 = jnp.float32
BF16 = jnp.bfloat16

D_MODEL = 2048
DEPTH = 4
RWKV_WIDTH = D_MODEL // 2
RWKV_HEAD = 64
DECAY_LORA = 64
AAA_LORA = 64
GATE_LORA = 160
RWKV_COLS = 3 * RWKV_WIDTH + DECAY_LORA + AAA_LORA + GATE_LORA
RWKV_LNX_EPS = 64e-5
DIFF_WIDTH = D_MODEL - RWKV_WIDTH
DIFF_HEAD = 128
DIFF_HALF = DIFF_HEAD // 2
DIFF_HEADS = DIFF_WIDTH // DIFF_HEAD
DIFF_COLS = 3 * DIFF_WIDTH
ROPE_THETA = 10000.0
NEG_INF = -1e30
GMLP_WIDTH = D_MODEL
GMLP_CHUNK = 128
GMLP_GROUPS = 16
FFN_HIDDEN = 5632
NORM_EPS = 1e-6
SUBLN_EPS = 1e-5
LN_EPS = 1e-5

LANES = 128
MXU_COLS = 256
PROLOGUE_PARTS = 4
VMEM_LIMIT_BYTES = 60 * 1024 * 1024

RW_XW = 3 * RWKV_WIDTH
RW_XA = RW_XW + LANES
RW_XG = RW_XA + LANES
RW_PAD_COLS = RW_XG + 2 * LANES

RWKV_CHUNK = 64
RWKV_CHUNKS_PER_STEP = 8
RWKV_CHUNKS_PER_TRIP = 2
ATTN_SUB = 128
ATTN_KEYS = 256
ATTN_ONES_ROWS = 16
ATTN_TRIP = 8
LOG2_E = 1.4426950408889634
PAIR = 2 * RWKV_HEAD


def _cparams(semantics):
    return pltpu.CompilerParams(dimension_semantics=semantics, vmem_limit_bytes=VMEM_LIMIT_BYTES)


def _dot(a, b):
    return jnp.dot(a, b, preferred_element_type=F32)


def _dot_nt(a, b):
    return lax.dot_general(a, b, (((1,), (1,)), ((), ())), preferred_element_type=F32)


def _rms_norm_rows(x, g, eps):
    ms = jnp.mean(x * x, axis=-1, keepdims=True)
    return x * lax.rsqrt(ms + eps) * g


def _row_parts(tm):
    part = tm // PROLOGUE_PARTS
    return [slice(r0, r0 + part) for r0 in range(0, tm, part)]


def _norm_matmul_kernel(x_ref, g_ref, w_ref, o_ref, h_ref, *, tn, gelu):
    j = pl.program_id(1)

    def columns(rows):
        for n0 in range(0, tn, MXU_COLS):
            acc = _dot(h_ref[rows, :], w_ref[:, n0:n0 + MXU_COLS])
            if gelu:
                acc = 0.5 * acc * (1.0 + lax.erf(acc * math.sqrt(0.5)))
            o_ref[rows, n0:n0 + MXU_COLS] = acc.astype(o_ref.dtype)

    @pl.when(j == 0)
    def _():
        for rows in _row_parts(x_ref.shape[0]):
            h_ref[rows, :] = _rms_norm_rows(x_ref[rows, :], g_ref[...], NORM_EPS).astype(BF16)
            columns(rows)

    @pl.when(j > 0)
    def _():
        columns(slice(None))


def _norm_matmul(x, g, w, *, tm, tn, out_dtype, gelu=False, layer=None):
    m, k = x.shape
    n = w.shape[-1]
    if layer is None:
        w_spec = pl.BlockSpec((k, tn), lambda i, j: (0, j))
    else:
        w_spec = pl.BlockSpec((None, k, tn), lambda i, j: (layer, 0, j))
    return pl.pallas_call(
        functools.partial(_norm_matmul_kernel, tn=tn, gelu=gelu),
        grid=(m // tm, n // tn),
        in_specs=[
            pl.BlockSpec((tm, k), lambda i, j: (i, 0)),
            pl.BlockSpec((1, k), lambda i, j: (0, 0)),
            w_spec,
        ],
        out_specs=pl.BlockSpec((tm, tn), lambda i, j: (i, j)),
        out_shape=jax.ShapeDtypeStruct((m, n), out_dtype),
        scratch_shapes=[pltpu.VMEM((tm, k), BF16)],
        compiler_params=_cparams(("arbitrary", "arbitrary")),
        name="norm_matmul_gelu" if gelu else "norm_matmul",
    )(x, g, w)


def _norm_matmul_rope_kernel(x_ref, g_ref, w_ref, cos_ref, sin_ref, o_ref, h_ref, *, tn, q_tiles, rope_tiles):
    j = pl.program_id(1)

    def columns(rows, epilogue):
        for n0 in range(0, tn, MXU_COLS):
            acc = _dot(h_ref[rows, :], w_ref[:, n0:n0 + MXU_COLS])
            o_ref[rows, n0:n0 + MXU_COLS] = epilogue(rows, acc).astype(o_ref.dtype)

    def rope(rows, acc):
        reps = MXU_COLS // LANES
        c = jnp.concatenate([cos_ref[rows, :]] * reps, axis=1)
        s = jnp.concatenate([sin_ref[rows, :]] * reps, axis=1)
        lane = lax.broadcasted_iota(jnp.int32, acc.shape, 1)
        partner = jnp.where((lane & (DIFF_HALF // 2)) == 0,
                            pltpu.roll(acc, MXU_COLS - DIFF_HALF // 2, 1),
                            pltpu.roll(acc, DIFF_HALF // 2, 1))
        scale = jnp.where(j < q_tiles, DIFF_HALF ** -0.5 * LOG2_E, 1.0).astype(F32)
        return (acc * c + partner * s) * scale

    @pl.when(j == 0)
    def _():
        for rows in _row_parts(x_ref.shape[0]):
            h_ref[rows, :] = _rms_norm_rows(x_ref[rows, :], g_ref[...], NORM_EPS).astype(BF16)
            columns(rows, rope)

    @pl.when(jnp.logical_and(j > 0, j < rope_tiles))
    def _():
        columns(slice(None), rope)

    @pl.when(j >= rope_tiles)
    def _():
        columns(slice(None), lambda rows, acc: acc)


def _norm_matmul_rope(x, g, w, cos, sin, *, tm, tn, seq):
    m, k = x.shape
    n = w.shape[1]
    q_tiles = DIFF_WIDTH // tn
    t_tiles = seq // tm
    return pl.pallas_call(
        functools.partial(_norm_matmul_rope_kernel, tn=tn, q_tiles=q_tiles, rope_tiles=2 * q_tiles),
        grid=(m // tm, n // tn),
        in_specs=[
            pl.BlockSpec((tm, k), lambda i, j: (i, 0)),
            pl.BlockSpec((1, k), lambda i, j: (0, 0)),
            pl.BlockSpec((k, tn), lambda i, j: (0, j)),
            pl.BlockSpec((tm, LANES), lambda i, j: (i % t_tiles, 0)),
            pl.BlockSpec((tm, LANES), lambda i, j: (i % t_tiles, 0)),
        ],
        out_specs=pl.BlockSpec((tm, tn), lambda i, j: (i, j)),
        out_shape=jax.ShapeDtypeStruct((m, n), BF16),
        scratch_shapes=[pltpu.VMEM((tm, k), BF16)],
        compiler_params=_cparams(("arbitrary", "arbitrary")),
        name="norm_matmul_rope",
    )(x, g, w, cos, sin)


def _sigmoid(y):
    return 1.0 / (1.0 + jnp.exp(-y))


def _split_bf16(x, parts):
    out = []
    rem = x
    for _ in range(parts):
        p = rem.astype(BF16)
        out.append(p)
        rem = rem - p.astype(F32)
    return out


def _rwkv_kernel(z_ref, mu_ref, w0_ref, wdec_ref, a0_ref, wa_ref, wg_ref, kk_ref, ka_ref, rk_ref,
                 lw_ref, lb_ref, o_ref, s_ref, prev_ref):
    L = RWKV_CHUNK
    C = RWKV_WIDTH
    n_pairs = C // PAIR
    n_batch = z_ref.shape[0]

    @pl.when(pl.program_id(0) == 0)
    def _():
        s_ref[...] = jnp.zeros_like(s_ref)
        prev_ref[...] = jnp.zeros_like(prev_ref)

    ti = lax.broadcasted_iota(jnp.int32, (L, L), 0)
    tj = lax.broadcasted_iota(jnp.int32, (L, L), 1)
    tril_ones = jnp.where(ti >= tj, 1.0, 0.0).astype(BF16)

    def prepare(bi, row0):
        z = z_ref[bi, pl.ds(row0, L), :].astype(F32)
        row = lax.broadcasted_iota(jnp.int32, z.shape, 0)
        z_prev = jnp.where(row == 0, prev_ref[bi, 0:1, :], pltpu.roll(z, 1, 0))
        prev_ref[bi, 0:1, :] = z[L - 1:L, :]
        zs = z + (z_prev - z) * mu_ref[...]

        r = zs[:, 0:C]
        k = zs[:, C:2 * C]
        v = zs[:, 2 * C:3 * C]
        xw = zs[:, RW_XW:RW_XA]
        xa = zs[:, RW_XA:RW_XG]
        xg = zs[:, RW_XG:RW_PAD_COLS]

        y_dec = w0_ref[...] + _dot(jnp.tanh(xw).astype(BF16), wdec_ref[...])
        logd = -math.exp(-0.5) * _sigmoid(y_dec)
        a_lr = _sigmoid(a0_ref[...] + _dot(xa.astype(BF16), wa_ref[...]))
        gate = _dot(_sigmoid(xg).astype(BF16), wg_ref[...])
        kk_raw = k * kk_ref[...]
        k2 = k * (1.0 + (a_lr - 1.0) * ka_ref[...])
        rkk = r * k2 * rk_ref[...]
        cum = sum(_dot(tril_ones, p) for p in _split_bf16(logd, 2))
        return r, k2, v, logd, a_lr, gate, kk_raw, rkk, cum

    rr = lax.broadcasted_iota(jnp.int32, (PAIR, PAIR), 0)
    cc = lax.broadcasted_iota(jnp.int32, (PAIR, PAIR), 1)
    same_head = (rr // RWKV_HEAD) == (cc // RWKV_HEAD)
    strict = jnp.logical_and(same_head, rr > cc)
    incl = jnp.logical_and(same_head, rr >= cc)
    head_ones = jnp.where(same_head, 1.0, 0.0).astype(BF16)
    eye = jnp.where(rr == cc, 1.0, 0.0).astype(F32)
    first_head = lax.broadcasted_iota(jnp.int32, (L, PAIR), 1) < RWKV_HEAD

    def head_sum(x):
        return _dot(x.astype(BF16), head_ones)

    def stack(x):
        return jnp.concatenate([jnp.where(first_head, x, 0.0), jnp.where(first_head, 0.0, x)], axis=0)

    n_steps = int(math.log2(L)) - 1
    idx = range(n_pairs)
    sls = [slice(p * PAIR, (p + 1) * PAIR) for p in idx]

    def elementwise(prep, p):
        r, k2, v, logd, a_lr, _, kk_raw, rkk, cum = prep
        sl = sls[p]
        kk_p = kk_raw[:, sl]
        kk_n = kk_p * lax.rsqrt(jnp.maximum(head_sum(kk_p * kk_p), 1e-24))
        bonus = head_sum(rkk[:, sl])
        a_p = -kk_n
        b_p = kk_n * a_lr[:, sl]
        r_p, k_p = r[:, sl], k2[:, sl]
        cu = cum[:, sl]
        c_end = cu[L - 1:L, :]
        e_pos = jnp.exp(cu)
        e_neg = jnp.exp(-cu)
        e_prev = jnp.exp(cu - logd[:, sl])
        e_end = jnp.exp(c_end - cu)
        lhs = jnp.concatenate([stack(a_p * e_prev), stack(r_p * e_pos)], axis=0).astype(BF16)
        rhs = jnp.concatenate([stack(b_p * e_neg), stack(k_p * e_neg)], axis=0).astype(BF16)
        bk_end = jnp.concatenate([stack(b_p * e_end), stack(k_p * e_end)], axis=0).astype(BF16)
        return lhs, rhs, bk_end, stack(v[:, sl]), jnp.exp(c_end), bonus

    def stages(bi, row0, prep, elems, filler):
        v, gate = prep[2], prep[5]
        lhs, rhs, bk_end, v_st, d_end, bonus_s = zip(*elems)
        pm = [_dot_nt(lhs[i], rhs[i]) for i in idx]
        filler()
        s_old = [s_ref[bi, p] for p in idx]
        sh = [_dot_nt(lhs[i], s_old[i].astype(BF16)) for i in idx]
        filler()
        p_ab = [jnp.where(strict, pm[i][:PAIR, :PAIR], 0.0) for i in idx]
        p_ak = [jnp.where(strict, pm[i][:PAIR, PAIR:], 0.0).astype(BF16) for i in idx]
        p_rbk = [jnp.concatenate([jnp.where(incl, pm[i][PAIR:, :PAIR], 0.0),
                                  jnp.where(incl, pm[i][PAIR:, PAIR:], 0.0)], axis=1).astype(BF16) for i in idx]
        akv = [_dot(p_ak[i], v_st[i].astype(BF16)) for i in idx]
        filler()

        t_inv = [eye + p_ab[i] for i in idx]
        xb = [p_ab[i].astype(BF16) for i in idx]
        xb = [_dot(xb[i], xb[i]).astype(BF16) for i in idx]
        filler()
        for it in range(n_steps):
            if it + 1 < n_steps:
                both = [_dot(xb[i], jnp.concatenate([xb[i], t_inv[i].astype(BF16)], axis=1)) for i in idx]
                xb = [both[i][:, :PAIR].astype(BF16) for i in idx]
                t_inv = [t_inv[i] + both[i][:, PAIR:] for i in idx]
            else:
                t_inv = [t_inv[i] + _dot(xb[i], t_inv[i].astype(BF16)) for i in idx]
            filler()

        u_st = [_dot(t_inv[i].astype(BF16), (sh[i][:PAIR] + akv[i]).astype(BF16)) for i in idx]
        filler()
        uv = [jnp.concatenate([u_st[i], v_st[i]], axis=0) for i in idx]
        y_st = [sh[i][PAIR:] + _dot(p_rbk[i], uv[i].astype(BF16)) for i in idx]
        filler()
        s_upd = [_dot(uv[i].T.astype(BF16), bk_end[i]) for i in idx]
        for p in idx:
            s_ref[bi, p] = s_old[p] * d_end[p] + s_upd[p]

        y = [y_st[i][:L] + y_st[i][L:] for i in idx]
        mean = [head_sum(y[i]) * (1.0 / RWKV_HEAD) for i in idx]
        yc = [y[i] - mean[i] for i in idx]
        var = [head_sum(yc[i] * yc[i]) * (1.0 / RWKV_HEAD) for i in idx]
        for i in idx:
            sl = sls[i]
            yn = yc[i] * lax.rsqrt(var[i] + RWKV_LNX_EPS) * lw_ref[:, sl] + lb_ref[:, sl]
            o_ref[bi, pl.ds(row0, L), sl] = ((yn + bonus_s[i] * v[:, sl]) * gate[:, sl]).astype(o_ref.dtype)

    def trip(ti, carry):
        rows = [pl.multiple_of((ti * RWKV_CHUNKS_PER_TRIP + k) * L, L) for k in range(RWKV_CHUNKS_PER_TRIP)]
        units = [(k, bi) for k in range(RWKV_CHUNKS_PER_TRIP) for bi in range(n_batch)]
        prepared, elems = {}, {u: [] for u in units}

        def work_before(u):
            k = u[0]
            todo = [functools.partial(lambda v: prepared.__setitem__(v, prepare(v[1], rows[v[0]])), v)
                    for v in units if v[0] == k and v not in prepared and v not in scheduled]
            scheduled.update(v for v in units if v[0] == k)
            todo += [functools.partial(lambda v, p: elems[v].append(elementwise(prepared[v], p)), u, p) for p in idx]
            return todo

        scheduled = set()
        for thunk in work_before(units[0]):
            thunk()
        for n, u in enumerate(units):
            queue = work_before(units[n + 1]) if n + 1 < len(units) else []

            def filler():
                if queue:
                    queue.pop(0)()

            stages(u[1], rows[u[0]], prepared[u], elems[u], filler)
            while queue:
                queue.pop(0)()
        return carry

    lax.fori_loop(0, z_ref.shape[1] // (L * RWKV_CHUNKS_PER_TRIP), trip, 0)


def _rwkv_time_mix(z, mu, w0, wdec, a0, wa, wg, k_k, k_a, r_k, lnx_w, lnx_b, *, batch, seq):
    rows = min(RWKV_CHUNK * RWKV_CHUNKS_PER_STEP, seq)
    n_pairs = RWKV_WIDTH // PAIR
    full = lambda shape: pl.BlockSpec(shape, lambda c: (0,) * len(shape))
    out = pl.pallas_call(
        _rwkv_kernel,
        grid=(seq // rows,),
        in_specs=[
            pl.BlockSpec((batch, rows, RW_PAD_COLS), lambda c: (0, c, 0)),
            full((1, RW_PAD_COLS)),
            full((1, RWKV_WIDTH)), full((LANES, RWKV_WIDTH)),
            full((1, RWKV_WIDTH)), full((LANES, RWKV_WIDTH)),
            full((2 * LANES, RWKV_WIDTH)),
            full((1, RWKV_WIDTH)), full((1, RWKV_WIDTH)), full((1, RWKV_WIDTH)),
            full((1, RWKV_WIDTH)), full((1, RWKV_WIDTH)),
        ],
        out_specs=pl.BlockSpec((batch, rows, RWKV_WIDTH), lambda c: (0, c, 0)),
        out_shape=jax.ShapeDtypeStruct((batch, seq, RWKV_WIDTH), BF16),
        scratch_shapes=[pltpu.VMEM((batch, n_pairs, PAIR, PAIR), F32), pltpu.VMEM((batch, 8, RW_PAD_COLS), F32)],
        compiler_params=_cparams(("arbitrary",)),
        name="rwkv7_time_mix",
    )(z.reshape(batch, seq, RW_PAD_COLS), mu, w0, wdec, a0, wa, wg, k_k, k_a, r_k, lnx_w, lnx_b)
    return out.reshape(batch * seq, RWKV_WIDTH)


def _attn_kernel(lam_ref, g_ref, q_ref, k_ref, v_ref, o_ref, vt_ref, acc_ref, qq_ref, s_ref, smax_ref, *, seq,
                 n_sub, lambda_init):
    sb = ATTN_SUB
    two = 2 * sb
    kb = ATTN_KEYS
    qb = n_sub * sb
    steps_per_q = qb // kb
    nq = seq // qb

    def transpose_block(t, carry):
        start = pl.multiple_of(t * kb, kb)
        vt_ref[t, :DIFF_HEAD, :] = v_ref[pl.ds(start, kb), :].astype(F32).T.astype(BF16)
        vt_ref[t, DIFF_HEAD:, :] = jnp.ones((ATTN_ONES_ROWS, kb), BF16)
        return carry
    lax.fori_loop(0, seq // kb, transpose_block, 0)

    lane = lax.broadcasted_iota(jnp.int32, (sb, DIFF_HEAD), 1)
    key_idx = lax.broadcasted_iota(jnp.int32, (kb, two), 0)
    qry_idx = lax.broadcasted_iota(jnp.int32, (kb, two), 1) & (sb - 1)
    lam_p = lam_ref[...]
    lam = (jnp.exp(jnp.sum(lam_p[0:1] * lam_p[1:2], axis=-1, keepdims=True))
           - jnp.exp(jnp.sum(lam_p[2:3] * lam_p[3:4], axis=-1, keepdims=True)) + lambda_init)

    def scores(qslot, c, j):
        start = pl.multiple_of(j * kb, kb)
        return _dot_nt(k_ref[pl.ds(start, kb), :], qq_ref[qslot, c])

    def softmax_pv(c, j, st, st_max, m):
        m_new = jnp.maximum(m, st_max)
        alpha = jnp.exp2(m - m_new)
        pr = jnp.exp2(st - m_new)
        acc_ref[c] = alpha * acc_ref[c] + _dot(vt_ref[j], pr.astype(BF16))
        return m_new

    def stage_scores(qslot, slot, c, j):
        st = scores(qslot, c, j)
        s_ref[slot, c] = st
        smax_ref[slot, c] = jnp.max(st, axis=0, keepdims=True)

    def open_block(i):
        qslot = i % 2
        for c in range(n_sub):
            row = pl.multiple_of(i * qb + c * sb, sb)
            qc = q_ref[pl.ds(row, sb), :]
            zero = jnp.zeros_like(qc)
            qq_ref[qslot, c] = jnp.concatenate([jnp.where(lane < DIFF_HALF, qc, zero),
                                                jnp.where(lane < DIFF_HALF, zero, qc)], axis=0)
        for c in range(n_sub):
            stage_scores(qslot, 0, c, 0)

    def q_block(i, carry):
        qslot = i % 2
        for c in range(n_sub):
            acc_ref[c] = jnp.zeros((DIFF_HEAD + ATTN_ONES_ROWS, two), F32)

        def full_steps(j0, n_steps, ms):
            ms = list(ms)
            for h in range(n_steps):
                for c in range(n_sub):
                    stage_scores(qslot, 1 - h % 2, c, j0 + h + 1)
                for c in range(n_sub):
                    ms[c] = softmax_pv(c, j0 + h, s_ref[h % 2, c], smax_ref[h % 2, c], ms[c])
            return tuple(ms)

        init = tuple(jnp.full((1, two), NEG_INF, F32) for _ in range(n_sub))
        n_full = i * steps_per_q
        n_short = (n_full % ATTN_TRIP) // 2
        ms = lax.fori_loop(0, n_short, lambda t, cr: full_steps(2 * t, 2, cr), init)
        ms = list(lax.fori_loop(0, n_full // ATTN_TRIP,
                                lambda t, cr: full_steps(2 * n_short + ATTN_TRIP * t, ATTN_TRIP, cr), ms))

        first_diag = n_full
        needed = lambda s, c: s * kb < (c + 1) * sb
        tail = {(s, c): scores(qslot, c, first_diag + s)
                for s in range(1, steps_per_q) for c in range(n_sub) if needed(s, c)}
        for s in range(steps_per_q):
            for c in range(n_sub):
                if not needed(s, c):
                    continue
                st = s_ref[0, c] if s == 0 else tail[(s, c)]
                if (s + 1) * kb - 1 > c * sb:
                    st = jnp.where(key_idx + s * kb <= qry_idx + c * sb, st, NEG_INF)
                    st_max = jnp.max(st, axis=0, keepdims=True)
                else:
                    st_max = smax_ref[0, c] if s == 0 else jnp.max(st, axis=0, keepdims=True)
                ms[c] = softmax_pv(c, first_diag + s, st, st_max, ms[c])

        open_block(jnp.minimum(i + 1, nq - 1))

        for c in range(n_sub):
            acc = acc_ref[c]
            o = acc[:DIFF_HEAD] / acc[DIFF_HEAD:DIFF_HEAD + 1]
            d = o[:, :sb] - lam * o[:, sb:]
            ms_d = jnp.mean(d * d, axis=0, keepdims=True)
            d = d * lax.rsqrt(ms_d + SUBLN_EPS) * g_ref[...] * (1.0 - lambda_init)
            row = pl.multiple_of(i * qb + c * sb, sb)
            o_ref[pl.ds(row, sb), :] = d.T.astype(o_ref.dtype)
        return carry

    open_block(0)
    lax.fori_loop(0, nq, q_block, 0)


def _diff_attention(qkv, lam_params, subln_g_col, *, batch, seq, n_sub, lambda_init):
    qb = n_sub * ATTN_SUB
    assert qb % (2 * ATTN_KEYS) == 0 and seq % qb == 0
    h = DIFF_HEADS
    head_cols = lambda off: pl.BlockSpec((seq, DIFF_HEAD), lambda b, hh: (b, off + hh))
    return pl.pallas_call(
        functools.partial(_attn_kernel, seq=seq, n_sub=n_sub, lambda_init=lambda_init),
        grid=(batch, h),
        in_specs=[
            pl.BlockSpec((8, LANES), lambda b, hh: (0, 0)),
            pl.BlockSpec((DIFF_HEAD, 1), lambda b, hh: (0, 0)),
            head_cols(0), head_cols(h), head_cols(2 * h),
        ],
        out_specs=head_cols(0),
        out_shape=jax.ShapeDtypeStruct((batch * seq, DIFF_WIDTH), BF16),
        scratch_shapes=[pltpu.VMEM((seq // ATTN_KEYS, DIFF_HEAD + ATTN_ONES_ROWS, ATTN_KEYS), BF16),
                        pltpu.VMEM((n_sub, DIFF_HEAD + ATTN_ONES_ROWS, 2 * ATTN_SUB), F32),
                        pltpu.VMEM((2, n_sub, 2 * ATTN_SUB, DIFF_HEAD), BF16),
                        pltpu.VMEM((2, n_sub, ATTN_KEYS, 2 * ATTN_SUB), F32),
                        pltpu.VMEM((2, n_sub, 1, 2 * ATTN_SUB), F32)],
        compiler_params=_cparams(("arbitrary", "arbitrary")),
        name="diff_attention",
    )(lam_params, subln_g_col, qkv, qkv, qkv)


def _outproj2_kernel(ya_ref, yb_ref, x_ref, wa_ref, wb_ref, o_ref):
    o_ref[...] = x_ref[...] + _dot(ya_ref[...], wa_ref[...]) + _dot(yb_ref[...], wb_ref[...])


def _outproj2(ya, yb, x, w_all, *, layer, tm):
    m, d = x.shape
    half = w_all.shape[1] // 2
    return pl.pallas_call(
        _outproj2_kernel,
        grid=(m // tm,),
        in_specs=[
            pl.BlockSpec((tm, ya.shape[1]), lambda i: (i, 0)),
            pl.BlockSpec((tm, yb.shape[1]), lambda i: (i, 0)),
            pl.BlockSpec((tm, d), lambda i: (i, 0)),
            pl.BlockSpec((None, half, d), lambda i: (layer, 0, 0)),
            pl.BlockSpec((None, half, d), lambda i: (layer, 1, 0)),
        ],
        out_specs=pl.BlockSpec((tm, d), lambda i: (i, 0)),
        out_shape=jax.ShapeDtypeStruct((m, d), F32),
        compiler_params=_cparams(("arbitrary",)),
        name="even_out_proj",
    )(ya, yb, x, w_all, w_all)


def _sgu_kernel(u_ref, v_ref, x_ref, lng_ref, lnb_ref, ws_ref, bs_ref, wo_ref, o_ref, vn_ref, gated_ref, *, tm):
    ch = GMLP_CHUNK
    ti = lax.broadcasted_iota(jnp.int32, (ch, ch), 0)
    tj = lax.broadcasted_iota(jnp.int32, (ch, ch), 1)
    causal = ti >= tj
    half = tm // 2
    for r0 in range(0, tm, half):
        part = slice(r0, r0 + half)
        v = v_ref[part, :].astype(F32)
        mu = jnp.mean(v, axis=-1, keepdims=True)
        vc = v - mu
        var = jnp.mean(vc * vc, axis=-1, keepdims=True)
        vn_ref[part, :] = (vc * lax.rsqrt(var + LN_EPS) * lng_ref[...] + lnb_ref[...]).astype(BF16)
        for g in range(GMLP_GROUPS):
            cols = slice(g * LANES, (g + 1) * LANES)
            wg = jnp.where(causal, ws_ref[g], 0.0).astype(BF16)
            bias = bs_ref[:, g:g + 1]
            for c in range(half // ch):
                rows = slice(r0 + c * ch, r0 + (c + 1) * ch)
                sv = _dot(wg, vn_ref[rows, cols]) + bias
                gated_ref[rows, cols] = (u_ref[rows, cols].astype(F32) * sv).astype(BF16)
        o_ref[part, :] = x_ref[part, :] + _dot(gated_ref[part, :], wo_ref[...])


def _sgu_outproj(u_v, x, ln_g, ln_b, w_s_all, b_s_t, w_out_all, *, layer, tm):
    m, d = x.shape
    n_half = GMLP_WIDTH // d
    return pl.pallas_call(
        functools.partial(_sgu_kernel, tm=tm),
        grid=(m // tm,),
        in_specs=[
            pl.BlockSpec((tm, GMLP_WIDTH), lambda i: (i, 0)),
            pl.BlockSpec((tm, GMLP_WIDTH), lambda i: (i, n_half)),
            pl.BlockSpec((tm, d), lambda i: (i, 0)),
            pl.BlockSpec((1, GMLP_WIDTH), lambda i: (0, 0)),
            pl.BlockSpec((1, GMLP_WIDTH), lambda i: (0, 0)),
            pl.BlockSpec((None,) + w_s_all.shape[1:], lambda i: (layer, 0, 0, 0)),
            pl.BlockSpec(b_s_t.shape, lambda i: (0, 0)),
            pl.BlockSpec((None,) + w_out_all.shape[1:], lambda i: (layer, 0, 0)),
        ],
        out_specs=pl.BlockSpec((tm, d), lambda i: (i, 0)),
        out_shape=jax.ShapeDtypeStruct((m, d), F32),
        scratch_shapes=[pltpu.VMEM((tm, GMLP_WIDTH), BF16), pltpu.VMEM((tm, GMLP_WIDTH), BF16)],
        compiler_params=_cparams(("arbitrary",)),
        name="sgu_out_proj",
    )(u_v, u_v, x, ln_g, ln_b, w_s_all, b_s_t, w_out_all)


def _ffn_kernel(x_ref, g_ref, wg_ref, wu_ref, wd_ref, fg_ref, o_ref, h_ref, *, final):
    j = pl.program_id(1)

    def hidden_tile(rows):
        h = h_ref[rows, :]
        gate = _dot(h, wg_ref[...])
        up = _dot(h, wu_ref[...])
        act = gate * _sigmoid(gate) * up
        o_ref[rows, :] += _dot(act.astype(BF16), wd_ref[...].astype(BF16))

    @pl.when(j == 0)
    def _():
        for rows in _row_parts(x_ref.shape[0]):
            x = x_ref[rows, :]
            h_ref[rows, :] = _rms_norm_rows(x, g_ref[...], NORM_EPS).astype(BF16)
            o_ref[rows, :] = x
            hidden_tile(rows)

    last = pl.num_programs(1) - 1
    if final:
        @pl.when(jnp.logical_and(j > 0, j < last))
        def _():
            hidden_tile(slice(None))

        @pl.when(j == last)
        def _():
            for rows in _row_parts(x_ref.shape[0]):
                hidden_tile(rows)
                o_ref[rows, :] = _rms_norm_rows(o_ref[rows, :], fg_ref[...], NORM_EPS)
    else:
        @pl.when(j > 0)
        def _():
            hidden_tile(slice(None))


def _ffn(x, g, w_gate_all, w_up_all, w_down_all, final_g, *, layer, tm, th, final):
    m, d = x.shape
    hid = w_gate_all.shape[2]
    return pl.pallas_call(
        functools.partial(_ffn_kernel, final=final),
        grid=(m // tm, hid // th),
        in_specs=[
            pl.BlockSpec((tm, d), lambda i, j: (i, 0)),
            pl.BlockSpec((1, d), lambda i, j: (0, 0)),
            pl.BlockSpec((None, d, th), lambda i, j: (layer, 0, j)),
            pl.BlockSpec((None, d, th), lambda i, j: (layer, 0, j)),
            pl.BlockSpec((None, th, d), lambda i, j: (layer, j, 0)),
            pl.BlockSpec((1, d), lambda i, j: (0, 0)),
        ],
        out_specs=pl.BlockSpec((tm, d), lambda i, j: (i, 0)),
        out_shape=jax.ShapeDtypeStruct((m, d), F32),
        scratch_shapes=[pltpu.VMEM((tm, d), BF16)],
        compiler_params=_cparams(("arbitrary", "arbitrary")),
        name="swiglu_ffn",
    )(x, g, w_gate_all, w_up_all, w_down_all, final_g)


def _pad_cols(a, width):
    return jnp.pad(a, ((0, 0), (0, width - a.shape[1])))


def _pad_rows(a, height):
    return jnp.pad(a, ((0, height - a.shape[0]), (0, 0)))


def _rwkv_col_layout(a):
    c = RWKV_WIDTH
    xw = a[:, 3 * c:3 * c + DECAY_LORA]
    xa = a[:, 3 * c + DECAY_LORA:3 * c + DECAY_LORA + AAA_LORA]
    xg = a[:, 3 * c + DECAY_LORA + AAA_LORA:]
    return jnp.concatenate(
        [a[:, :3 * c], _pad_cols(xw, LANES), _pad_cols(xa, LANES), _pad_cols(xg, 2 * LANES)], axis=1)


def _rwkv_weight_layout(w_in):
    c = RWKV_WIDTH
    out = jnp.zeros((w_in.shape[0], RW_PAD_COLS), BF16)
    out = lax.dynamic_update_slice(out, w_in[:, :3 * c].astype(BF16), (0, 0))
    starts = (3 * c, 3 * c + DECAY_LORA, 3 * c + DECAY_LORA + AAA_LORA, RWKV_COLS)
    for src0, src1, dst in zip(starts[:-1], starts[1:], (RW_XW, RW_XA, RW_XG)):
        out = lax.dynamic_update_slice(out, w_in[:, src0:src1].astype(BF16), (0, dst))
    return out


def _rope_tables(seq):
    inv = ROPE_THETA ** (-jnp.arange(0, DIFF_HALF, 2, dtype=F32) / DIFF_HALF)
    ang = jnp.arange(seq, dtype=F32)[:, None] * inv[None, :]
    cos, sin = jnp.cos(ang), jnp.sin(ang)
    reps = LANES // DIFF_HALF
    return (jnp.tile(jnp.concatenate([cos, cos], axis=1), (1, reps)),
            jnp.tile(jnp.concatenate([-sin, sin], axis=1), (1, reps)))


def _tile_sizes(m, seq):
    return dict(
        proj_tm=min(1024, seq), proj_tn=1024, rwkv_proj_tn=RW_PAD_COLS // 2,
        ffn_tm=min(1024, m), ffn_th=512,
        out_tm=min(512, m), sgu_tm=min(512, m),
        attn_sub=min(4, seq // ATTN_SUB),
    )


def kernel(x, mix_norm, ffn_norm, ffn_w_gate, ffn_w_up, ffn_w_down, ev_w_in, ev_mu, ev_w0, ev_w_dec_up, ev_a0,
           ev_w_a_up, ev_w_g_up, ev_k_k, ev_k_a, ev_r_k, ev_lnx_w, ev_lnx_b, ev_lam_q1, ev_lam_k1, ev_lam_q2,
           ev_lam_k2, ev_subln_g, ev_w_out, od_w_in, od_ln_g, od_ln_b, od_w_s, od_b_s, od_w_out, final_norm):
    batch, seq, d = x.shape
    m = batch * seq
    ts = _tile_sizes(m, seq)
    cos, sin = _rope_tables(seq)
    row = lambda a: a.reshape(1, -1).astype(F32)
    xf = x.reshape(m, d)
    ffn_w_gate_bf, ffn_w_up_bf, ffn_w_down_f32 = ffn_w_gate.astype(BF16), ffn_w_up.astype(BF16), ffn_w_down.astype(F32)
    ev_w_out_bf, od_w_in_bf, od_w_out_bf = ev_w_out.astype(BF16), od_w_in.astype(BF16), od_w_out.astype(BF16)
    od_w_s_f32 = od_w_s.astype(F32)
    for i in range(DEPTH):
        j = i // 2
        g_mix = row(mix_norm[i])
        if i % 2 == 0:
            lambda_init = 0.8 - 0.6 * math.exp(-0.3 * i)
            w_in = ev_w_in[j]
            w_rwkv = _rwkv_weight_layout(w_in)
            w_diff = w_in[:, RWKV_COLS:].astype(BF16)
            z = _norm_matmul(xf, g_mix, w_rwkv, tm=ts["proj_tm"], tn=ts["rwkv_proj_tn"], out_dtype=BF16)
            qkv = _norm_matmul_rope(xf, g_mix, w_diff, cos, sin, tm=ts["proj_tm"], tn=ts["proj_tn"], seq=seq)
            y_a = _rwkv_time_mix(
                z, _rwkv_col_layout(row(ev_mu[j])), row(ev_w0[j]),
                _pad_rows(ev_w_dec_up[j], LANES).astype(BF16), row(ev_a0[j]),
                _pad_rows(ev_w_a_up[j], LANES).astype(BF16), _pad_rows(ev_w_g_up[j], 2 * LANES).astype(BF16),
                row(ev_k_k[j]), row(ev_k_a[j]), row(ev_r_k[j]), row(ev_lnx_w[j]), row(ev_lnx_b[j]),
                batch=batch, seq=seq)
            lam_params = _pad_rows(_pad_cols(
                jnp.stack([ev_lam_q1[j], ev_lam_k1[j], ev_lam_q2[j], ev_lam_k2[j]]).astype(F32), LANES), 8)
            y_b = _diff_attention(qkv, lam_params, ev_subln_g[j].reshape(-1, 1).astype(F32), batch=batch, seq=seq,
                                  n_sub=ts["attn_sub"], lambda_init=lambda_init)
            xf = _outproj2(y_a, y_b, xf, ev_w_out_bf, layer=j, tm=ts["out_tm"])
        else:
            u_v = _norm_matmul(xf, g_mix, od_w_in_bf, tm=ts["proj_tm"], tn=ts["proj_tn"],
                               out_dtype=BF16, gelu=True, layer=j)
            xf = _sgu_outproj(u_v, xf, row(od_ln_g[j]), row(od_ln_b[j]), od_w_s_f32,
                              od_b_s[j].T.astype(F32), od_w_out_bf, layer=j, tm=ts["sgu_tm"])
        xf = _ffn(xf, row(ffn_norm[i]), ffn_w_gate_bf, ffn_w_up_bf, ffn_w_down_f32, row(final_norm),
                  layer=i, tm=ts["ffn_tm"], th=ts["ffn_th"], final=(i == DEPTH - 1))
    return xf.reshape(batch, seq, d)
```

```python
import functools
import math

import jax
import jax.numpy as jnp
from jax import lax
from jax.experimental import pallas as pl
from jax.experimental.pallas import tpu as pltpu

F32 = jnp.float32
BF16 = jnp.bfloat16

D_MODEL = 2048
DEPTH = 4
RWKV_WIDTH = D_MODEL // 2
RWKV_HEAD = 64
DECAY_LORA = 64
AAA_LORA = 64
GATE_LORA = 160
RWKV_COLS = 3 * RWKV_WIDTH + DECAY_LORA + AAA_LORA + GATE_LORA
RWKV_LNX_EPS = 64e-5
DIFF_WIDTH = D_MODEL - RWKV_WIDTH
DIFF_HEAD = 128
DIFF_HALF = DIFF_HEAD // 2
DIFF_HEADS = DIFF_WIDTH // DIFF_HEAD
ROPE_THETA = 10000.0
NEG_INF = -1e30
GMLP_WIDTH = D_MODEL
GMLP_CHUNK = 128
GMLP_GROUPS = 16
NORM_EPS = 1e-6
SUBLN_EPS = 1e-5
LN_EPS = 1e-5

LANES = 128
MXU_COLS = 256
PROLOGUE_PARTS = 4
VMEM_LIMIT_BYTES = 60 * 1024 * 1024

RW_XW = 3 * RWKV_WIDTH
RW_XA = RW_XW + LANES
RW_XG = RW_XA + LANES
RW_PAD_COLS = RW_XG + 2 * LANES

RWKV_CHUNK = 64
RWKV_CHUNKS_PER_STEP = 8
RWKV_CHUNKS_PER_TRIP = 2
ATTN_SUB = 128
ATTN_KEYS = 256
ATTN_ONES_ROWS = 16
ATTN_TRIP = 8
LOG2_E = 1.4426950408889634
PAIR = 2 * RWKV_HEAD


def _cparams(semantics):
    return pltpu.CompilerParams(dimension_semantics=semantics, vmem_limit_bytes=VMEM_LIMIT_BYTES)


def _dot(a, b):
    return jnp.dot(a, b, preferred_element_type=F32)


def _dot_nt(a, b):
    return lax.dot_general(a, b, (((1,), (1,)), ((), ())), preferred_element_type=F32)


def _rms_norm_rows(x, g, eps):
    ms = jnp.mean(x * x, axis=-1, keepdims=True)
    return x * lax.rsqrt(ms + eps) * g


def _row_parts(tm):
    part = tm // PROLOGUE_PARTS
    return [slice(r0, r0 + part) for r0 in range(0, tm, part)]


def _norm_matmul_kernel(x_ref, g_ref, w_ref, o_ref, h_ref, *, tn, gelu):
    j = pl.program_id(1)

    def columns(rows):
        for n0 in range(0, tn, MXU_COLS):
            acc = _dot(h_ref[rows, :], w_ref[:, n0:n0 + MXU_COLS])
            if gelu:
                acc = 0.5 * acc * (1.0 + lax.erf(acc * math.sqrt(0.5)))
            o_ref[rows, n0:n0 + MXU_COLS] = acc.astype(o_ref.dtype)

    @pl.when(j == 0)
    def _():
        for rows in _row_parts(x_ref.shape[0]):
            h_ref[rows, :] = _rms_norm_rows(x_ref[rows, :], g_ref[...], NORM_EPS).astype(BF16)
            columns(rows)

    @pl.when(j > 0)
    def _():
        columns(slice(None))


def _norm_matmul(x, g, w, *, tm, tn, out_dtype, gelu=False, layer=None):
    m, k = x.shape
    n = w.shape[-1]
    if layer is None:
        w_spec = pl.BlockSpec((k, tn), lambda i, j: (0, j))
    else:
        w_spec = pl.BlockSpec((None, k, tn), lambda i, j: (layer, 0, j))
    return pl.pallas_call(
        functools.partial(_norm_matmul_kernel, tn=tn, gelu=gelu),
        grid=(m // tm, n // tn),
        in_specs=[
            pl.BlockSpec((tm, k), lambda i, j: (i, 0)),
            pl.BlockSpec((1, k), lambda i, j: (0, 0)),
            w_spec,
        ],
        out_specs=pl.BlockSpec((tm, tn), lambda i, j: (i, j)),
        out_shape=jax.ShapeDtypeStruct((m, n), out_dtype),
        scratch_shapes=[pltpu.VMEM((tm, k), BF16)],
        compiler_params=_cparams(("arbitrary", "arbitrary")),
        name="norm_matmul_gelu" if gelu else "norm_matmul",
    )(x, g, w)


def _norm_matmul_rope_kernel(x_ref, g_ref, w_ref, cos_ref, sin_ref, o_ref, h_ref, *, tn, q_tiles, rope_tiles):
    j = pl.program_id(1)

    def columns(rows, epilogue):
        for n0 in range(0, tn, MXU_COLS):
            acc = _dot(h_ref[rows, :], w_ref[:, n0:n0 + MXU_COLS])
            o_ref[rows, n0:n0 + MXU_COLS] = epilogue(rows, acc).astype(o_ref.dtype)

    def rope(rows, acc):
        reps = MXU_COLS // LANES
        c = jnp.concatenate([cos_ref[rows, :]] * reps, axis=1)
        s = jnp.concatenate([sin_ref[rows, :]] * reps, axis=1)
        lane = lax.broadcasted_iota(jnp.int32, acc.shape, 1)
        partner = jnp.where((lane & (DIFF_HALF // 2)) == 0,
                            pltpu.roll(acc, MXU_COLS - DIFF_HALF // 2, 1),
                            pltpu.roll(acc, DIFF_HALF // 2, 1))
        scale = jnp.where(j < q_tiles, DIFF_HALF ** -0.5 * LOG2_E, 1.0).astype(F32)
        return (acc * c + partner * s) * scale

    @pl.when(j == 0)
    def _():
        for rows in _row_parts(x_ref.shape[0]):
            h_ref[rows, :] = _rms_norm_rows(x_ref[rows, :], g_ref[...], NORM_EPS).astype(BF16)
            columns(rows, rope)

    @pl.when(jnp.logical_and(j > 0, j < rope_tiles))
    def _():
        columns(slice(None), rope)

    @pl.when(j >= rope_tiles)
    def _():
        columns(slice(None), lambda rows, acc: acc)


def _norm_matmul_rope(x, g, w, cos, sin, *, tm, tn, seq):
    m, k = x.shape
    n = w.shape[1]
    q_tiles = DIFF_WIDTH // tn
    t_tiles = seq // tm
    return pl.pallas_call(
        functools.partial(_norm_matmul_rope_kernel, tn=tn, q_tiles=q_tiles, rope_tiles=2 * q_tiles),
        grid=(m // tm, n // tn),
        in_specs=[
            pl.BlockSpec((tm, k), lambda i, j: (i, 0)),
            pl.BlockSpec((1, k), lambda i, j: (0, 0)),
            pl.BlockSpec((k, tn), lambda i, j: (0, j)),
            pl.BlockSpec((tm, LANES), lambda i, j: (i % t_tiles, 0)),
            pl.BlockSpec((tm, LANES), lambda i, j: (i % t_tiles, 0)),
        ],
        out_specs=pl.BlockSpec((tm, tn), lambda i, j: (i, j)),
        out_shape=jax.ShapeDtypeStruct((m, n), BF16),
        scratch_shapes=[pltpu.VMEM((tm, k), BF16)],
        compiler_params=_cparams(("arbitrary", "arbitrary")),
        name="norm_matmul_rope",
    )(x, g, w, cos, sin)


def _sigmoid(y):
    return 1.0 / (1.0 + jnp.exp(-y))


def _split_bf16(x, parts):
    out = []
    rem = x
    for _ in range(parts):
        p = rem.astype(BF16)
        out.append(p)
        rem = rem - p.astype(F32)
    return out


def _rwkv_kernel(z_ref, mu_ref, w0_ref, wdec_ref, a0_ref, wa_ref, wg_ref, kk_ref, ka_ref, rk_ref,
                 lw_ref, lb_ref, o_ref, s_ref, prev_ref):
    L = RWKV_CHUNK
    C = RWKV_WIDTH
    n_pairs = C // PAIR
    n_batch = z_ref.shape[0]

    @pl.when(pl.program_id(0) == 0)
    def _():
        s_ref[...] = jnp.zeros_like(s_ref)
        prev_ref[...] = jnp.zeros_like(prev_ref)

    ti = lax.broadcasted_iota(jnp.int32, (L, L), 0)
    tj = lax.broadcasted_iota(jnp.int32, (L, L), 1)
    tril_ones = jnp.where(ti >= tj, 1.0, 0.0).astype(BF16)

    def prepare(bi, row0):
        z = z_ref[bi, pl.ds(row0, L), :].astype(F32)
        row = lax.broadcasted_iota(jnp.int32, z.shape, 0)
        z_prev = jnp.where(row == 0, prev_ref[bi, 0:1, :], pltpu.roll(z, 1, 0))
        prev_ref[bi, 0:1, :] = z[L - 1:L, :]
        zs = z + (z_prev - z) * mu_ref[...]

        r = zs[:, 0:C]
        k = zs[:, C:2 * C]
        v = zs[:, 2 * C:3 * C]
        xw = zs[:, RW_XW:RW_XA]
        xa = zs[:, RW_XA:RW_XG]
        xg = zs[:, RW_XG:RW_PAD_COLS]

        y_dec = w0_ref[...] + _dot(jnp.tanh(xw).astype(BF16), wdec_ref[...])
        logd = -math.exp(-0.5) * _sigmoid(y_dec)
        a_lr = _sigmoid(a0_ref[...] + _dot(xa.astype(BF16), wa_ref[...]))
        gate = _dot(_sigmoid(xg).astype(BF16), wg_ref[...])
        kk_raw = k * kk_ref[...]
        k2 = k * (1.0 + (a_lr - 1.0) * ka_ref[...])
        rkk = r * k2 * rk_ref[...]
        cum = sum(_dot(tril_ones, p) for p in _split_bf16(logd, 2))
        return r, k2, v, logd, a_lr, gate, kk_raw, rkk, cum

    rr = lax.broadcasted_iota(jnp.int32, (PAIR, PAIR), 0)
    cc = lax.broadcasted_iota(jnp.int32, (PAIR, PAIR), 1)
    same_head = (rr // RWKV_HEAD) == (cc // RWKV_HEAD)
    strict = jnp.logical_and(same_head, rr > cc)
    incl = jnp.logical_and(same_head, rr >= cc)
    head_ones = jnp.where(same_head, 1.0, 0.0).astype(BF16)
    eye = jnp.where(rr == cc, 1.0, 0.0).astype(F32)
    first_head = lax.broadcasted_iota(jnp.int32, (L, PAIR), 1) < RWKV_HEAD

    def head_sum(x):
        return _dot(x.astype(BF16), head_ones)

    def stack(x):
        return jnp.concatenate([jnp.where(first_head, x, 0.0), jnp.where(first_head, 0.0, x)], axis=0)

    n_steps = int(math.log2(L)) - 1
    idx = range(n_pairs)
    sls = [slice(p * PAIR, (p + 1) * PAIR) for p in idx]

    def elementwise(prep, p):
        r, k2, v, logd, a_lr, _, kk_raw, rkk, cum = prep
        sl = sls[p]
        kk_p = kk_raw[:, sl]
        kk_n = kk_p * lax.rsqrt(jnp.maximum(head_sum(kk_p * kk_p), 1e-24))
        bonus = head_sum(rkk[:, sl])
        a_p = -kk_n
        b_p = kk_n * a_lr[:, sl]
        r_p, k_p = r[:, sl], k2[:, sl]
        cu = cum[:, sl]
        c_end = cu[L - 1:L, :]
        e_pos = jnp.exp(cu)
        e_neg = jnp.exp(-cu)
        e_prev = jnp.exp(cu - logd[:, sl])
        e_end = jnp.exp(c_end - cu)
        lhs = jnp.concatenate([stack(a_p * e_prev), stack(r_p * e_pos)], axis=0).astype(BF16)
        rhs = jnp.concatenate([stack(b_p * e_neg), stack(k_p * e_neg)], axis=0).astype(BF16)
        bk_end = jnp.concatenate([stack(b_p * e_end), stack(k_p * e_end)], axis=0).astype(BF16)
        return lhs, rhs, bk_end, stack(v[:, sl]), jnp.exp(c_end), bonus

    def stages(bi, row0, prep, elems, filler):
        v, gate = prep[2], prep[5]
        lhs, rhs, bk_end, v_st, d_end, bonus_s = zip(*elems)
        pm = [_dot_nt(lhs[i], rhs[i]) for i in idx]
        filler()
        s_old = [s_ref[bi, p] for p in idx]
        sh = [_dot_nt(lhs[i], s_old[i].astype(BF16)) for i in idx]
        filler()
        p_ab = [jnp.where(strict, pm[i][:PAIR, :PAIR], 0.0) for i in idx]
        p_ak = [jnp.where(strict, pm[i][:PAIR, PAIR:], 0.0).astype(BF16) for i in idx]
        p_rbk = [jnp.concatenate([jnp.where(incl, pm[i][PAIR:, :PAIR], 0.0),
                                  jnp.where(incl, pm[i][PAIR:, PAIR:], 0.0)], axis=1).astype(BF16) for i in idx]
        akv = [_dot(p_ak[i], v_st[i].astype(BF16)) for i in idx]
        filler()

        t_inv = [eye + p_ab[i] for i in idx]
        xb = [p_ab[i].astype(BF16) for i in idx]
        xb = [_dot(xb[i], xb[i]).astype(BF16) for i in idx]
        filler()
        for it in range(n_steps):
            if it + 1 < n_steps:
                both = [_dot(xb[i], jnp.concatenate([xb[i], t_inv[i].astype(BF16)], axis=1)) for i in idx]
                xb = [both[i][:, :PAIR].astype(BF16) for i in idx]
                t_inv = [t_inv[i] + both[i][:, PAIR:] for i in idx]
            else:
                t_inv = [t_inv[i] + _dot(xb[i], t_inv[i].astype(BF16)) for i in idx]
            filler()

        u_st = [_dot(t_inv[i].astype(BF16), (sh[i][:PAIR] + akv[i]).astype(BF16)) for i in idx]
        filler()
        uv = [jnp.concatenate([u_st[i], v_st[i]], axis=0) for i in idx]
        y_st = [sh[i][PAIR:] + _dot(p_rbk[i], uv[i].astype(BF16)) for i in idx]
        filler()
        s_upd = [_dot(uv[i].T.astype(BF16), bk_end[i]) for i in idx]
        for p in idx:
            s_ref[bi, p] = s_old[p] * d_end[p] + s_upd[p]

        y = [y_st[i][:L] + y_st[i][L:] for i in idx]
        mean = [head_sum(y[i]) * (1.0 / RWKV_HEAD) for i in idx]
        yc = [y[i] - mean[i] for i in idx]
        var = [head_sum(yc[i] * yc[i]) * (1.0 / RWKV_HEAD) for i in idx]
        for i in idx:
            sl = sls[i]
            yn = yc[i] * lax.rsqrt(var[i] + RWKV_LNX_EPS) * lw_ref[:, sl] + lb_ref[:, sl]
            o_ref[bi, pl.ds(row0, L), sl] = ((yn + bonus_s[i] * v[:, sl]) * gate[:, sl]).astype(o_ref.dtype)

    def trip(ti, carry):
        rows = [pl.multiple_of((ti * RWKV_CHUNKS_PER_TRIP + k) * L, L) for k in range(RWKV_CHUNKS_PER_TRIP)]
        units = [(k, bi) for k in range(RWKV_CHUNKS_PER_TRIP) for bi in range(n_batch)]
        prepared, elems = {}, {u: [] for u in units}

        def work_before(u):
            k = u[0]
            todo = [functools.partial(lambda v: prepared.__setitem__(v, prepare(v[1], rows[v[0]])), v)
                    for v in units if v[0] == k and v not in prepared and v not in scheduled]
            scheduled.update(v for v in units if v[0] == k)
            todo += [functools.partial(lambda v, p: elems[v].append(elementwise(prepared[v], p)), u, p) for p in idx]
            return todo

        scheduled = set()
        for thunk in work_before(units[0]):
            thunk()
        for n, u in enumerate(units):
            queue = work_before(units[n + 1]) if n + 1 < len(units) else []

            def filler():
                if queue:
                    queue.pop(0)()

            stages(u[1], rows[u[0]], prepared[u], elems[u], filler)
            while queue:
                queue.pop(0)()
        return carry

    lax.fori_loop(0, z_ref.shape[1] // (L * RWKV_CHUNKS_PER_TRIP), trip, 0)


def _rwkv_time_mix(z, mu, w0, wdec, a0, wa, wg, k_k, k_a, r_k, lnx_w, lnx_b, *, batch, seq):
    rows = min(RWKV_CHUNK * RWKV_CHUNKS_PER_STEP, seq)
    n_pairs = RWKV_WIDTH // PAIR
    full = lambda shape: pl.BlockSpec(shape, lambda c: (0,) * len(shape))
    out = pl.pallas_call(
        _rwkv_kernel,
        grid=(seq // rows,),
        in_specs=[
            pl.BlockSpec((batch, rows, RW_PAD_COLS), lambda c: (0, c, 0)),
            full((1, RW_PAD_COLS)),
            full((1, RWKV_WIDTH)), full((LANES, RWKV_WIDTH)),
            full((1, RWKV_WIDTH)), full((LANES, RWKV_WIDTH)),
            full((2 * LANES, RWKV_WIDTH)),
            full((1, RWKV_WIDTH)), full((1, RWKV_WIDTH)), full((1, RWKV_WIDTH)),
            full((1, RWKV_WIDTH)), full((1, RWKV_WIDTH)),
        ],
        out_specs=pl.BlockSpec((batch, rows, RWKV_WIDTH), lambda c: (0, c, 0)),
        out_shape=jax.ShapeDtypeStruct((batch, seq, RWKV_WIDTH), BF16),
        scratch_shapes=[pltpu.VMEM((batch, n_pairs, PAIR, PAIR), F32), pltpu.VMEM((batch, 8, RW_PAD_COLS), F32)],
        compiler_params=_cparams(("arbitrary",)),
        name="rwkv7_time_mix",
    )(z.reshape(batch, seq, RW_PAD_COLS), mu, w0, wdec, a0, wa, wg, k_k, k_a, r_k, lnx_w, lnx_b)
    return out.reshape(batch * seq, RWKV_WIDTH)


def _attn_kernel(lam_ref, g_ref, q_ref, k_ref, v_ref, o_ref, vt_ref, acc_ref, qq_ref, s_ref, smax_ref, *, seq,
                 n_sub, lambda_init):
    sb = ATTN_SUB
    two = 2 * sb
    kb = ATTN_KEYS
    qb = n_sub * sb
    steps_per_q = qb // kb
    nq = seq // qb

    def transpose_block(t, carry):
        start = pl.multiple_of(t * kb, kb)
        vt_ref[t, :DIFF_HEAD, :] = v_ref[pl.ds(start, kb), :].astype(F32).T.astype(BF16)
        vt_ref[t, DIFF_HEAD:, :] = jnp.ones((ATTN_ONES_ROWS, kb), BF16)
        return carry
    lax.fori_loop(0, seq // kb, transpose_block, 0)

    lane = lax.broadcasted_iota(jnp.int32, (sb, DIFF_HEAD), 1)
    key_idx = lax.broadcasted_iota(jnp.int32, (kb, two), 0)
    qry_idx = lax.broadcasted_iota(jnp.int32, (kb, two), 1) & (sb - 1)
    lam_p = lam_ref[...]
    lam = (jnp.exp(jnp.sum(lam_p[0:1] * lam_p[1:2], axis=-1, keepdims=True))
           - jnp.exp(jnp.sum(lam_p[2:3] * lam_p[3:4], axis=-1, keepdims=True)) + lambda_init)

    def scores(qslot, c, j):
        start = pl.multiple_of(j * kb, kb)
        return _dot_nt(k_ref[pl.ds(start, kb), :], qq_ref[qslot, c])

    def softmax_pv(c, j, st, st_max, m):
        m_new = jnp.maximum(m, st_max)
        alpha = jnp.exp2(m - m_new)
        pr = jnp.exp2(st - m_new)
        acc_ref[c] = alpha * acc_ref[c] + _dot(vt_ref[j], pr.astype(BF16))
        return m_new

    def stage_scores(qslot, slot, c, j):
        st = scores(qslot, c, j)
        s_ref[slot, c] = st
        smax_ref[slot, c] = jnp.max(st, axis=0, keepdims=True)

    def open_block(i):
        qslot = i % 2
        for c in range(n_sub):
            row = pl.multiple_of(i * qb + c * sb, sb)
            qc = q_ref[pl.ds(row, sb), :]
            zero = jnp.zeros_like(qc)
            qq_ref[qslot, c] = jnp.concatenate([jnp.where(lane < DIFF_HALF, qc, zero),
                                                jnp.where(lane < DIFF_HALF, zero, qc)], axis=0)
        for c in range(n_sub):
            stage_scores(qslot, 0, c, 0)

    def q_block(i, carry):
        qslot = i % 2
        for c in range(n_sub):
            acc_ref[c] = jnp.zeros((DIFF_HEAD + ATTN_ONES_ROWS, two), F32)

        def full_steps(j0, n_steps, ms):
            ms = list(ms)
            for h in range(n_steps):
                for c in range(n_sub):
                    stage_scores(qslot, 1 - h % 2, c, j0 + h + 1)
                for c in range(n_sub):
                    ms[c] = softmax_pv(c, j0 + h, s_ref[h % 2, c], smax_ref[h % 2, c], ms[c])
            return tuple(ms)

        init = tuple(jnp.full((1, two), NEG_INF, F32) for _ in range(n_sub))
        n_full = i * steps_per_q
        n_short = (n_full % ATTN_TRIP) // 2
        ms = lax.fori_loop(0, n_short, lambda t, cr: full_steps(2 * t, 2, cr), init)
        ms = list(lax.fori_loop(0, n_full // ATTN_TRIP,
                                lambda t, cr: full_steps(2 * n_short + ATTN_TRIP * t, ATTN_TRIP, cr), ms))

        first_diag = n_full
        needed = lambda s, c: s * kb < (c + 1) * sb
        tail = {(s, c): scores(qslot, c, first_diag + s)
                for s in range(1, steps_per_q) for c in range(n_sub) if needed(s, c)}
        for s in range(steps_per_q):
            for c in range(n_sub):
                if not needed(s, c):
                    continue
                st = s_ref[0, c] if s == 0 else tail[(s, c)]
                if (s + 1) * kb - 1 > c * sb:
                    st = jnp.where(key_idx + s * kb <= qry_idx + c * sb, st, NEG_INF)
                    st_max = jnp.max(st, axis=0, keepdims=True)
                else:
                    st_max = smax_ref[0, c] if s == 0 else jnp.max(st, axis=0, keepdims=True)
                ms[c] = softmax_pv(c, first_diag + s, st, st_max, ms[c])

        open_block(jnp.minimum(i + 1, nq - 1))

        for c in range(n_sub):
            acc = acc_ref[c]
            o = acc[:DIFF_HEAD] / acc[DIFF_HEAD:DIFF_HEAD + 1]
            d = o[:, :sb] - lam * o[:, sb:]
            ms_d = jnp.mean(d * d, axis=0, keepdims=True)
            d = d * lax.rsqrt(ms_d + SUBLN_EPS) * g_ref[...] * (1.0 - lambda_init)
            row = pl.multiple_of(i * qb + c * sb, sb)
            o_ref[pl.ds(row, sb), :] = d.T.astype(o_ref.dtype)
        return carry

    open_block(0)
    lax.fori_loop(0, nq, q_block, 0)


def _diff_attention(qkv, lam_params, subln_g_col, *, batch, seq, n_sub, lambda_init):
    qb = n_sub * ATTN_SUB
    assert qb % (2 * ATTN_KEYS) == 0 and seq % qb == 0
    h = DIFF_HEADS
    head_cols = lambda off: pl.BlockSpec((seq, DIFF_HEAD), lambda b, hh: (b, off + hh))
    return pl.pallas_call(
        functools.partial(_attn_kernel, seq=seq, n_sub=n_sub, lambda_init=lambda_init),
        grid=(batch, h),
        in_specs=[
            pl.BlockSpec((8, LANES), lambda b, hh: (0, 0)),
            pl.BlockSpec((DIFF_HEAD, 1), lambda b, hh: (0, 0)),
            head_cols(0), head_cols(h), head_cols(2 * h),
        ],
        out_specs=head_cols(0),
        out_shape=jax.ShapeDtypeStruct((batch * seq, DIFF_WIDTH), BF16),
        scratch_shapes=[pltpu.VMEM((seq // ATTN_KEYS, DIFF_HEAD + ATTN_ONES_ROWS, ATTN_KEYS), BF16),
                        pltpu.VMEM((n_sub, DIFF_HEAD + ATTN_ONES_ROWS, 2 * ATTN_SUB), F32),
                        pltpu.VMEM((2, n_sub, 2 * ATTN_SUB, DIFF_HEAD), BF16),
                        pltpu.VMEM((2, n_sub, ATTN_KEYS, 2 * ATTN_SUB), F32),
                        pltpu.VMEM((2, n_sub, 1, 2 * ATTN_SUB), F32)],
        compiler_params=_cparams(("arbitrary", "arbitrary")),
        name="diff_attention",
    )(lam_params, subln_g_col, qkv, qkv, qkv)


def _outproj2_kernel(ya_ref, yb_ref, x_ref, wa_ref, wb_ref, o_ref):
    o_ref[...] = x_ref[...] + _dot(ya_ref[...], wa_ref[...]) + _dot(yb_ref[...], wb_ref[...])


def _outproj2(ya, yb, x, w_all, *, layer, tm):
    m, d = x.shape
    half = w_all.shape[1] // 2
    return pl.pallas_call(
        _outproj2_kernel,
        grid=(m // tm,),
        in_specs=[
            pl.BlockSpec((tm, ya.shape[1]), lambda i: (i, 0)),
            pl.BlockSpec((tm, yb.shape[1]), lambda i: (i, 0)),
            pl.BlockSpec((tm, d), lambda i: (i, 0)),
            pl.BlockSpec((None, half, d), lambda i: (layer, 0, 0)),
            pl.BlockSpec((None, half, d), lambda i: (layer, 1, 0)),
        ],
        out_specs=pl.BlockSpec((tm, d), lambda i: (i, 0)),
        out_shape=jax.ShapeDtypeStruct((m, d), F32),
        compiler_params=_cparams(("arbitrary",)),
        name="even_out_proj",
    )(ya, yb, x, w_all, w_all)


def _sgu_kernel(u_ref, v_ref, x_ref, lng_ref, lnb_ref, ws_ref, bs_ref, wo_ref, o_ref, vn_ref, gated_ref, *, tm):
    ch = GMLP_CHUNK
    ti = lax.broadcasted_iota(jnp.int32, (ch, ch), 0)
    tj = lax.broadcasted_iota(jnp.int32, (ch, ch), 1)
    causal = ti >= tj
    half = tm // 2
    for r0 in range(0, tm, half):
        part = slice(r0, r0 + half)
        v = v_ref[part, :].astype(F32)
        mu = jnp.mean(v, axis=-1, keepdims=True)
        vc = v - mu
        var = jnp.mean(vc * vc, axis=-1, keepdims=True)
        vn_ref[part, :] = (vc * lax.rsqrt(var + LN_EPS) * lng_ref[...] + lnb_ref[...]).astype(BF16)
        for g in range(GMLP_GROUPS):
            cols = slice(g * LANES, (g + 1) * LANES)
            wg = jnp.where(causal, ws_ref[g], 0.0).astype(BF16)
            bias = bs_ref[:, g:g + 1]
            for c in range(half // ch):
                rows = slice(r0 + c * ch, r0 + (c + 1) * ch)
                sv = _dot(wg, vn_ref[rows, cols]) + bias
                gated_ref[rows, cols] = (u_ref[rows, cols].astype(F32) * sv).astype(BF16)
        o_ref[part, :] = x_ref[part, :] + _dot(gated_ref[part, :], wo_ref[...])


def _sgu_outproj(u_v, x, ln_g, ln_b, w_s_all, b_s_t, w_out_all, *, layer, tm):
    m, d = x.shape
    n_half = GMLP_WIDTH // d
    return pl.pallas_call(
        functools.partial(_sgu_kernel, tm=tm),
        grid=(m // tm,),
        in_specs=[
            pl.BlockSpec((tm, GMLP_WIDTH), lambda i: (i, 0)),
            pl.BlockSpec((tm, GMLP_WIDTH), lambda i: (i, n_half)),
            pl.BlockSpec((tm, d), lambda i: (i, 0)),
            pl.BlockSpec((1, GMLP_WIDTH), lambda i: (0, 0)),
            pl.BlockSpec((1, GMLP_WIDTH), lambda i: (0, 0)),
            pl.BlockSpec((None,) + w_s_all.shape[1:], lambda i: (layer, 0, 0, 0)),
            pl.BlockSpec(b_s_t.shape, lambda i: (0, 0)),
            pl.BlockSpec((None,) + w_out_all.shape[1:], lambda i: (layer, 0, 0)),
        ],
        out_specs=pl.BlockSpec((tm, d), lambda i: (i, 0)),
        out_shape=jax.ShapeDtypeStruct((m, d), F32),
        scratch_shapes=[pltpu.VMEM((tm, GMLP_WIDTH), BF16), pltpu.VMEM((tm, GMLP_WIDTH), BF16)],
        compiler_params=_cparams(("arbitrary",)),
        name="sgu_out_proj",
    )(u_v, u_v, x, ln_g, ln_b, w_s_all, b_s_t, w_out_all)


def _ffn_kernel(x_ref, g_ref, wg_ref, wu_ref, wd_ref, fg_ref, o_ref, h_ref, *, final):
    j = pl.program_id(1)

    def hidden_tile(rows):
        h = h_ref[rows, :]
        gate = _dot(h, wg_ref[...])
        up = _dot(h, wu_ref[...])
        act = gate * _sigmoid(gate) * up
        o_ref[rows, :] += _dot(act.astype(BF16), wd_ref[...].astype(BF16))

    @pl.when(j == 0)
    def _():
        for rows in _row_parts(x_ref.shape[0]):
            x = x_ref[rows, :]
            h_ref[rows, :] = _rms_norm_rows(x, g_ref[...], NORM_EPS).astype(BF16)
            o_ref[rows, :] = x
            hidden_tile(rows)

    last = pl.num_programs(1) - 1
    if final:
        @pl.when(jnp.logical_and(j > 0, j < last))
        def _():
            hidden_tile(slice(None))

        @pl.when(j == last)
        def _():
            for rows in _row_parts(x_ref.shape[0]):
                hidden_tile(rows)
                o_ref[rows, :] = _rms_norm_rows(o_ref[rows, :], fg_ref[...], NORM_EPS)
    else:
        @pl.when(j > 0)
        def _():
            hidden_tile(slice(None))


def _ffn(x, g, w_gate_all, w_up_all, w_down_all, final_g, *, layer, tm, th, final):
    m, d = x.shape
    hid = w_gate_all.shape[2]
    return pl.pallas_call(
        functools.partial(_ffn_kernel, final=final),
        grid=(m // tm, hid // th),
        in_specs=[
            pl.BlockSpec((tm, d), lambda i, j: (i, 0)),
            pl.BlockSpec((1, d), lambda i, j: (0, 0)),
            pl.BlockSpec((None, d, th), lambda i, j: (layer, 0, j)),
            pl.BlockSpec((None, d, th), lambda i, j: (layer, 0, j)),
            pl.BlockSpec((None, th, d), lambda i, j: (layer, j, 0)),
            pl.BlockSpec((1, d), lambda i, j: (0, 0)),
        ],
        out_specs=pl.BlockSpec((tm, d), lambda i, j: (i, 0)),
        out_shape=jax.ShapeDtypeStruct((m, d), F32),
        scratch_shapes=[pltpu.VMEM((tm, d), BF16)],
        compiler_params=_cparams(("arbitrary", "arbitrary")),
        name="swiglu_ffn",
    )(x, g, w_gate_all, w_up_all, w_down_all, final_g)


def _pad_cols(a, width):
    return jnp.pad(a, ((0, 0), (0, width - a.shape[1])))


def _pad_rows(a, height):
    return jnp.pad(a, ((0, height - a.shape[0]), (0, 0)))


def _rwkv_col_layout(a):
    c = RWKV_WIDTH
    xw = a[:, 3 * c:3 * c + DECAY_LORA]
    xa = a[:, 3 * c + DECAY_LORA:3 * c + DECAY_LORA + AAA_LORA]
    xg = a[:, 3 * c + DECAY_LORA + AAA_LORA:]
    return jnp.concatenate(
        [a[:, :3 * c], _pad_cols(xw, LANES), _pad_cols(xa, LANES), _pad_cols(xg, 2 * LANES)], axis=1)


def _rwkv_weight_layout(w_in):
    c = RWKV_WIDTH
    out = jnp.zeros((w_in.shape[0], RW_PAD_COLS), BF16)
    out = lax.dynamic_update_slice(out, w_in[:, :3 * c].astype(BF16), (0, 0))
    starts = (3 * c, 3 * c + DECAY_LORA, 3 * c + DECAY_LORA + AAA_LORA, RWKV_COLS)
    for src0, src1, dst in zip(starts[:-1], starts[1:], (RW_XW, RW_XA, RW_XG)):
        out = lax.dynamic_update_slice(out, w_in[:, src0:src1].astype(BF16), (0, dst))
    return out


def _rope_tables(seq):
    inv = ROPE_THETA ** (-jnp.arange(0, DIFF_HALF, 2, dtype=F32) / DIFF_HALF)
    ang = jnp.arange(seq, dtype=F32)[:, None] * inv[None, :]
    cos, sin = jnp.cos(ang), jnp.sin(ang)
    reps = LANES // DIFF_HALF
    return (jnp.tile(jnp.concatenate([cos, cos], axis=1), (1, reps)),
            jnp.tile(jnp.concatenate([-sin, sin], axis=1), (1, reps)))


def _tile_sizes(m, seq):
    return dict(
        proj_tm=min(1024, seq), proj_tn=1024, rwkv_proj_tn=RW_PAD_COLS // 2, gelu_proj_tn=2048,
        ffn_tm=min(1024, m), ffn_th=512,
        out_tm=min(512, m), sgu_tm=min(512, m),
        attn_sub=min(4, seq // ATTN_SUB),
    )


def kernel(x, mix_norm, ffn_norm, ffn_w_gate, ffn_w_up, ffn_w_down, ev_w_in, ev_mu, ev_w0, ev_w_dec_up, ev_a0,
           ev_w_a_up, ev_w_g_up, ev_k_k, ev_k_a, ev_r_k, ev_lnx_w, ev_lnx_b, ev_lam_q1, ev_lam_k1, ev_lam_q2,
           ev_lam_k2, ev_subln_g, ev_w_out, od_w_in, od_ln_g, od_ln_b, od_w_s, od_b_s, od_w_out, final_norm):
    batch, seq, d = x.shape
    m = batch * seq
    ts = _tile_sizes(m, seq)
    cos, sin = _rope_tables(seq)
    row = lambda a: a.reshape(1, -1).astype(F32)
    xf = x.reshape(m, d)
    ffn_w_gate_bf, ffn_w_up_bf, ffn_w_down_f32 = ffn_w_gate.astype(BF16), ffn_w_up.astype(BF16), ffn_w_down.astype(F32)
    ev_w_out_bf, od_w_in_bf, od_w_out_bf = ev_w_out.astype(BF16), od_w_in.astype(BF16), od_w_out.astype(BF16)
    od_w_s_f32 = od_w_s.astype(F32)
    for i in range(DEPTH):
        j = i // 2
        g_mix = row(mix_norm[i])
        if i % 2 == 0:
            lambda_init = 0.8 - 0.6 * math.exp(-0.3 * i)
            w_in = ev_w_in[j]
            w_rwkv = _rwkv_weight_layout(w_in)
            w_diff = w_in[:, RWKV_COLS:].astype(BF16)
            z = _norm_matmul(xf, g_mix, w_rwkv, tm=ts["proj_tm"], tn=ts["rwkv_proj_tn"], out_dtype=BF16)
            qkv = _norm_matmul_rope(xf, g_mix, w_diff, cos, sin, tm=ts["proj_tm"], tn=ts["proj_tn"], seq=seq)
            y_a = _rwkv_time_mix(
                z, _rwkv_col_layout(row(ev_mu[j])), row(ev_w0[j]),
                _pad_rows(ev_w_dec_up[j], LANES).astype(BF16), row(ev_a0[j]),
                _pad_rows(ev_w_a_up[j], LANES).astype(BF16), _pad_rows(ev_w_g_up[j], 2 * LANES).astype(BF16),
                row(ev_k_k[j]), row(ev_k_a[j]), row(ev_r_k[j]), row(ev_lnx_w[j]), row(ev_lnx_b[j]),
                batch=batch, seq=seq)
            lam_params = _pad_rows(_pad_cols(
                jnp.stack([ev_lam_q1[j], ev_lam_k1[j], ev_lam_q2[j], ev_lam_k2[j]]).astype(F32), LANES), 8)
            y_b = _diff_attention(qkv, lam_params, ev_subln_g[j].reshape(-1, 1).astype(F32), batch=batch, seq=seq,
                                  n_sub=ts["attn_sub"], lambda_init=lambda_init)
            xf = _outproj2(y_a, y_b, xf, ev_w_out_bf, layer=j, tm=ts["out_tm"])
        else:
            u_v = _norm_matmul(xf, g_mix, od_w_in_bf, tm=ts["proj_tm"], tn=ts["gelu_proj_tn"],
                               out_dtype=BF16, gelu=True, layer=j)
            xf = _sgu_outproj(u_v, xf, row(od_ln_g[j]), row(od_ln_b[j]), od_w_s_f32,
                              od_b_s[j].T.astype(F32), od_w_out_bf, layer=j, tm=ts["sgu_tm"])
        xf = _ffn(xf, row(ffn_norm[i]), ffn_w_gate_bf, ffn_w_up_bf, ffn_w_down_f32, row(final_norm),
                  layer=i, tm=ts["ffn_tm"], th=ts["ffn_th"], final=(i == DEPTH - 1))
    return xf.reshape(batch, seq, d)
```

```python
import functools
import math

import jax
import jax.numpy as jnp
from jax import lax
from jax.experimental import pallas as pl
from jax.experimental.pallas import tpu as pltpu

F32 = jnp.float32
BF16 = jnp.bfloat16

D_MODEL = 2048
DEPTH = 4
RWKV_WIDTH = D_MODEL // 2
RWKV_HEAD = 64
DECAY_LORA = 64
AAA_LORA = 64
GATE_LORA = 160
RWKV_COLS = 3 * RWKV_WIDTH + DECAY_LORA + AAA_LORA + GATE_LORA
RWKV_LNX_EPS = 64e-5
DIFF_WIDTH = D_MODEL - RWKV_WIDTH
DIFF_HEAD = 128
DIFF_HALF = DIFF_HEAD // 2
DIFF_HEADS = DIFF_WIDTH // DIFF_HEAD
ROPE_THETA = 10000.0
NEG_INF = -1e30
GMLP_WIDTH = D_MODEL
GMLP_CHUNK = 128
GMLP_GROUPS = 16
NORM_EPS = 1e-6
SUBLN_EPS = 1e-5
LN_EPS = 1e-5

LANES = 128
MXU_COLS = 256
PROLOGUE_PARTS = 4
VMEM_LIMIT_BYTES = 60 * 1024 * 1024

RW_XW = 3 * RWKV_WIDTH
RW_XA = RW_XW + LANES
RW_XG = RW_XA + LANES
RW_PAD_COLS = RW_XG + 2 * LANES

RWKV_CHUNK = 64
RWKV_CHUNKS_PER_STEP = 8
RWKV_CHUNKS_PER_TRIP = 2
ATTN_SUB = 128
ATTN_KEYS = 256
ATTN_ONES_ROWS = 16
ATTN_TRIP = 8
LOG2_E = 1.4426950408889634
PAIR = 2 * RWKV_HEAD


def _cparams(semantics):
    return pltpu.CompilerParams(dimension_semantics=semantics, vmem_limit_bytes=VMEM_LIMIT_BYTES)


def _dot(a, b):
    return jnp.dot(a, b, preferred_element_type=F32)


def _dot_nt(a, b):
    return lax.dot_general(a, b, (((1,), (1,)), ((), ())), preferred_element_type=F32)


def _rms_norm_rows(x, g, eps):
    ms = jnp.mean(x * x, axis=-1, keepdims=True)
    return x * lax.rsqrt(ms + eps) * g


def _row_parts(tm):
    part = tm // PROLOGUE_PARTS
    return [slice(r0, r0 + part) for r0 in range(0, tm, part)]


def _norm_matmul_kernel(x_ref, g_ref, w_ref, o_ref, h_ref, *, tn, gelu):
    j = pl.program_id(1)

    def columns(rows):
        for n0 in range(0, tn, MXU_COLS):
            acc = _dot(h_ref[rows, :], w_ref[:, n0:n0 + MXU_COLS])
            if gelu:
                acc = 0.5 * acc * (1.0 + lax.erf(acc * math.sqrt(0.5)))
            o_ref[rows, n0:n0 + MXU_COLS] = acc.astype(o_ref.dtype)

    @pl.when(j == 0)
    def _():
        for rows in _row_parts(x_ref.shape[0]):
            h_ref[rows, :] = _rms_norm_rows(x_ref[rows, :], g_ref[...], NORM_EPS).astype(BF16)
            columns(rows)

    @pl.when(j > 0)
    def _():
        columns(slice(None))


def _norm_matmul(x, g, w, *, tm, tn, out_dtype, gelu=False, layer=None):
    m, k = x.shape
    n = w.shape[-1]
    if layer is None:
        w_spec = pl.BlockSpec((k, tn), lambda i, j: (0, j))
    else:
        w_spec = pl.BlockSpec((None, k, tn), lambda i, j: (layer, 0, j))
    return pl.pallas_call(
        functools.partial(_norm_matmul_kernel, tn=tn, gelu=gelu),
        grid=(m // tm, n // tn),
        in_specs=[
            pl.BlockSpec((tm, k), lambda i, j: (i, 0)),
            pl.BlockSpec((1, k), lambda i, j: (0, 0)),
            w_spec,
        ],
        out_specs=pl.BlockSpec((tm, tn), lambda i, j: (i, j)),
        out_shape=jax.ShapeDtypeStruct((m, n), out_dtype),
        scratch_shapes=[pltpu.VMEM((tm, k), BF16)],
        compiler_params=_cparams(("arbitrary", "arbitrary")),
        name="norm_matmul_gelu" if gelu else "norm_matmul",
    )(x, g, w)


def _norm_matmul_rope_kernel(x_ref, g_ref, w_ref, cos_ref, sin_ref, o_ref, h_ref, *, tn, q_tiles, rope_tiles):
    j = pl.program_id(1)

    def columns(rows, epilogue):
        for n0 in range(0, tn, MXU_COLS):
            acc = _dot(h_ref[rows, :], w_ref[:, n0:n0 + MXU_COLS])
            o_ref[rows, n0:n0 + MXU_COLS] = epilogue(rows, acc).astype(o_ref.dtype)

    def rope(rows, acc):
        reps = MXU_COLS // LANES
        c = jnp.concatenate([cos_ref[rows, :]] * reps, axis=1)
        s = jnp.concatenate([sin_ref[rows, :]] * reps, axis=1)
        lane = lax.broadcasted_iota(jnp.int32, acc.shape, 1)
        partner = jnp.where((lane & (DIFF_HALF // 2)) == 0,
                            pltpu.roll(acc, MXU_COLS - DIFF_HALF // 2, 1),
                            pltpu.roll(acc, DIFF_HALF // 2, 1))
        scale = jnp.where(j < q_tiles, DIFF_HALF ** -0.5 * LOG2_E, 1.0).astype(F32)
        return (acc * c + partner * s) * scale

    @pl.when(j == 0)
    def _():
        for rows in _row_parts(x_ref.shape[0]):
            h_ref[rows, :] = _rms_norm_rows(x_ref[rows, :], g_ref[...], NORM_EPS).astype(BF16)
            columns(rows, rope)

    @pl.when(jnp.logical_and(j > 0, j < rope_tiles))
    def _():
        columns(slice(None), rope)

    @pl.when(j >= rope_tiles)
    def _():
        columns(slice(None), lambda rows, acc: acc)


def _norm_matmul_rope(x, g, w, cos, sin, *, tm, tn, seq):
    m, k = x.shape
    n = w.shape[1]
    q_tiles = DIFF_WIDTH // tn
    t_tiles = seq // tm
    return pl.pallas_call(
        functools.partial(_norm_matmul_rope_kernel, tn=tn, q_tiles=q_tiles, rope_tiles=2 * q_tiles),
        grid=(m // tm, n // tn),
        in_specs=[
            pl.BlockSpec((tm, k), lambda i, j: (i, 0)),
            pl.BlockSpec((1, k), lambda i, j: (0, 0)),
            pl.BlockSpec((k, tn), lambda i, j: (0, j)),
            pl.BlockSpec((tm, LANES), lambda i, j: (i % t_tiles, 0)),
            pl.BlockSpec((tm, LANES), lambda i, j: (i % t_tiles, 0)),
        ],
        out_specs=pl.BlockSpec((tm, tn), lambda i, j: (i, j)),
        out_shape=jax.ShapeDtypeStruct((m, n), BF16),
        scratch_shapes=[pltpu.VMEM((tm, k), BF16)],
        compiler_params=_cparams(("arbitrary", "arbitrary")),
        name="norm_matmul_rope",
    )(x, g, w, cos, sin)


def _sigmoid(y):
    return 1.0 / (1.0 + jnp.exp(-y))


def _split_bf16(x, parts):
    out = []
    rem = x
    for _ in range(parts):
        p = rem.astype(BF16)
        out.append(p)
        rem = rem - p.astype(F32)
    return out


def _rwkv_kernel(z_ref, mu_ref, w0_ref, wdec_ref, a0_ref, wa_ref, wg_ref, kk_ref, ka_ref, rk_ref,
                 lw_ref, lb_ref, o_ref, s_ref, prev_ref):
    L = RWKV_CHUNK
    C = RWKV_WIDTH
    n_pairs = C // PAIR
    n_batch = z_ref.shape[0]

    @pl.when(pl.program_id(0) == 0)
    def _():
        s_ref[...] = jnp.zeros_like(s_ref)
        prev_ref[...] = jnp.zeros_like(prev_ref)

    ti = lax.broadcasted_iota(jnp.int32, (L, L), 0)
    tj = lax.broadcasted_iota(jnp.int32, (L, L), 1)
    tril_ones = jnp.where(ti >= tj, 1.0, 0.0).astype(BF16)

    def prepare(bi, row0):
        z = z_ref[bi, pl.ds(row0, L), :].astype(F32)
        row = lax.broadcasted_iota(jnp.int32, z.shape, 0)
        z_prev = jnp.where(row == 0, prev_ref[bi, 0:1, :], pltpu.roll(z, 1, 0))
        prev_ref[bi, 0:1, :] = z[L - 1:L, :]
        zs = z + (z_prev - z) * mu_ref[...]

        r = zs[:, 0:C]
        k = zs[:, C:2 * C]
        v = zs[:, 2 * C:3 * C]
        xw = zs[:, RW_XW:RW_XA]
        xa = zs[:, RW_XA:RW_XG]
        xg = zs[:, RW_XG:RW_PAD_COLS]

        y_dec = w0_ref[...] + _dot(jnp.tanh(xw).astype(BF16), wdec_ref[...])
        logd = -math.exp(-0.5) * _sigmoid(y_dec)
        a_lr = _sigmoid(a0_ref[...] + _dot(xa.astype(BF16), wa_ref[...]))
        gate = _dot(_sigmoid(xg).astype(BF16), wg_ref[...])
        kk_raw = k * kk_ref[...]
        k2 = k * (1.0 + (a_lr - 1.0) * ka_ref[...])
        rkk = r * k2 * rk_ref[...]
        cum = sum(_dot(tril_ones, p) for p in _split_bf16(logd, 2))
        return r, k2, v, logd, a_lr, gate, kk_raw, rkk, cum

    rr = lax.broadcasted_iota(jnp.int32, (PAIR, PAIR), 0)
    cc = lax.broadcasted_iota(jnp.int32, (PAIR, PAIR), 1)
    same_head = (rr // RWKV_HEAD) == (cc // RWKV_HEAD)
    strict = jnp.logical_and(same_head, rr > cc)
    incl = jnp.logical_and(same_head, rr >= cc)
    head_ones = jnp.where(same_head, 1.0, 0.0).astype(BF16)
    eye = jnp.where(rr == cc, 1.0, 0.0).astype(F32)
    first_head = lax.broadcasted_iota(jnp.int32, (L, PAIR), 1) < RWKV_HEAD

    def head_sum(x):
        return _dot(x.astype(BF16), head_ones)

    def stack(x):
        return jnp.concatenate([jnp.where(first_head, x, 0.0), jnp.where(first_head, 0.0, x)], axis=0)

    n_steps = int(math.log2(L)) - 1
    idx = range(n_pairs)
    sls = [slice(p * PAIR, (p + 1) * PAIR) for p in idx]

    def elementwise(prep, p):
        r, k2, v, logd, a_lr, _, kk_raw, rkk, cum = prep
        sl = sls[p]
        kk_p = kk_raw[:, sl]
        kk_n = kk_p * lax.rsqrt(jnp.maximum(head_sum(kk_p * kk_p), 1e-24))
        bonus = head_sum(rkk[:, sl])
        a_p = -kk_n
        b_p = kk_n * a_lr[:, sl]
        r_p, k_p = r[:, sl], k2[:, sl]
        cu = cum[:, sl]
        c_end = cu[L - 1:L, :]
        e_pos = jnp.exp(cu)
        e_neg = jnp.exp(-cu)
        e_prev = jnp.exp(cu - logd[:, sl])
        e_end = jnp.exp(c_end - cu)
        lhs = jnp.concatenate([stack(a_p * e_prev), stack(r_p * e_pos)], axis=0).astype(BF16)
        rhs = jnp.concatenate([stack(b_p * e_neg), stack(k_p * e_neg)], axis=0).astype(BF16)
        bk_end = jnp.concatenate([stack(b_p * e_end), stack(k_p * e_end)], axis=0).astype(BF16)
        return lhs, rhs, bk_end, stack(v[:, sl]), jnp.exp(c_end), bonus

    def stages(bi, row0, prep, elems, filler):
        v, gate = prep[2], prep[5]
        lhs, rhs, bk_end, v_st, d_end, bonus_s = zip(*elems)
        pm = [_dot_nt(lhs[i], rhs[i]) for i in idx]
        filler()
        s_old = [s_ref[bi, p] for p in idx]
        sh = [_dot_nt(lhs[i], s_old[i].astype(BF16)) for i in idx]
        filler()
        p_ab = [jnp.where(strict, pm[i][:PAIR, :PAIR], 0.0) for i in idx]
        p_ak = [jnp.where(strict, pm[i][:PAIR, PAIR:], 0.0).astype(BF16) for i in idx]
        p_rbk = [jnp.concatenate([jnp.where(incl, pm[i][PAIR:, :PAIR], 0.0),
                                  jnp.where(incl, pm[i][PAIR:, PAIR:], 0.0)], axis=1).astype(BF16) for i in idx]
        akv = [_dot(p_ak[i], v_st[i].astype(BF16)) for i in idx]
        filler()

        t_inv = [eye + p_ab[i] for i in idx]
        xb = [p_ab[i].astype(BF16) for i in idx]
        xb = [_dot(xb[i], xb[i]).astype(BF16) for i in idx]
        filler()
        for it in range(n_steps):
            if it + 1 < n_steps:
                both = [_dot(xb[i], jnp.concatenate([xb[i], t_inv[i].astype(BF16)], axis=1)) for i in idx]
                xb = [both[i][:, :PAIR].astype(BF16) for i in idx]
                t_inv = [t_inv[i] + both[i][:, PAIR:] for i in idx]
            else:
                t_inv = [t_inv[i] + _dot(xb[i], t_inv[i].astype(BF16)) for i in idx]
            filler()

        u_st = [_dot(t_inv[i].astype(BF16), (sh[i][:PAIR] + akv[i]).astype(BF16)) for i in idx]
        filler()
        uv = [jnp.concatenate([u_st[i], v_st[i]], axis=0) for i in idx]
        y_st = [sh[i][PAIR:] + _dot(p_rbk[i], uv[i].astype(BF16)) for i in idx]
        filler()
        s_upd = [_dot(uv[i].T.astype(BF16), bk_end[i]) for i in idx]
        for p in idx:
            s_ref[bi, p] = s_old[p] * d_end[p] + s_upd[p]

        y = [y_st[i][:L] + y_st[i][L:] for i in idx]
        mean = [head_sum(y[i]) * (1.0 / RWKV_HEAD) for i in idx]
        yc = [y[i] - mean[i] for i in idx]
        var = [head_sum(yc[i] * yc[i]) * (1.0 / RWKV_HEAD) for i in idx]
        for i in idx:
            sl = sls[i]
            yn = yc[i] * lax.rsqrt(var[i] + RWKV_LNX_EPS) * lw_ref[:, sl] + lb_ref[:, sl]
            o_ref[bi, pl.ds(row0, L), sl] = ((yn + bonus_s[i] * v[:, sl]) * gate[:, sl]).astype(o_ref.dtype)

    def trip(ti, carry):
        rows = [pl.multiple_of((ti * RWKV_CHUNKS_PER_TRIP + k) * L, L) for k in range(RWKV_CHUNKS_PER_TRIP)]
        units = [(k, bi) for k in range(RWKV_CHUNKS_PER_TRIP) for bi in range(n_batch)]
        prepared, elems = {}, {u: [] for u in units}

        def work_before(u):
            k = u[0]
            todo = [functools.partial(lambda v: prepared.__setitem__(v, prepare(v[1], rows[v[0]])), v)
                    for v in units if v[0] == k and v not in prepared and v not in scheduled]
            scheduled.update(v for v in units if v[0] == k)
            todo += [functools.partial(lambda v, p: elems[v].append(elementwise(prepared[v], p)), u, p) for p in idx]
            return todo

        scheduled = set()
        for thunk in work_before(units[0]):
            thunk()
        for n, u in enumerate(units):
            queue = work_before(units[n + 1]) if n + 1 < len(units) else []

            def filler():
                if queue:
                    queue.pop(0)()

            stages(u[1], rows[u[0]], prepared[u], elems[u], filler)
            while queue:
                queue.pop(0)()
        return carry

    lax.fori_loop(0, z_ref.shape[1] // (L * RWKV_CHUNKS_PER_TRIP), trip, 0)


def _rwkv_time_mix(z, mu, w0, wdec, a0, wa, wg, k_k, k_a, r_k, lnx_w, lnx_b, *, batch, seq):
    rows = min(RWKV_CHUNK * RWKV_CHUNKS_PER_STEP, seq)
    n_pairs = RWKV_WIDTH // PAIR
    full = lambda shape: pl.BlockSpec(shape, lambda c: (0,) * len(shape))
    out = pl.pallas_call(
        _rwkv_kernel,
        grid=(seq // rows,),
        in_specs=[
            pl.BlockSpec((batch, rows, RW_PAD_COLS), lambda c: (0, c, 0)),
            full((1, RW_PAD_COLS)),
            full((1, RWKV_WIDTH)), full((LANES, RWKV_WIDTH)),
            full((1, RWKV_WIDTH)), full((LANES, RWKV_WIDTH)),
            full((2 * LANES, RWKV_WIDTH)),
            full((1, RWKV_WIDTH)), full((1, RWKV_WIDTH)), full((1, RWKV_WIDTH)),
            full((1, RWKV_WIDTH)), full((1, RWKV_WIDTH)),
        ],
        out_specs=pl.BlockSpec((batch, rows, RWKV_WIDTH), lambda c: (0, c, 0)),
        out_shape=jax.ShapeDtypeStruct((batch, seq, RWKV_WIDTH), BF16),
        scratch_shapes=[pltpu.VMEM((batch, n_pairs, PAIR, PAIR), F32), pltpu.VMEM((batch, 8, RW_PAD_COLS), F32)],
        compiler_params=_cparams(("arbitrary",)),
        name="rwkv7_time_mix",
    )(z.reshape(batch, seq, RW_PAD_COLS), mu, w0, wdec, a0, wa, wg, k_k, k_a, r_k, lnx_w, lnx_b)
    return out.reshape(batch * seq, RWKV_WIDTH)


def _attn_kernel(lam_ref, g_ref, q_ref, k_ref, v_ref, o_ref, vt_ref, acc_ref, qq_ref, s_ref, smax_ref, *, seq,
                 n_sub, lambda_init):
    sb = ATTN_SUB
    two = 2 * sb
    kb = ATTN_KEYS
    qb = n_sub * sb
    steps_per_q = qb // kb
    nq = seq // qb

    def transpose_block(t):
        start = pl.multiple_of(t * kb, kb)
        vt_ref[t, :DIFF_HEAD, :] = v_ref[pl.ds(start, kb), :].astype(F32).T.astype(BF16)
        vt_ref[t, DIFF_HEAD:, :] = jnp.ones((ATTN_ONES_ROWS, kb), BF16)

    lane = lax.broadcasted_iota(jnp.int32, (sb, DIFF_HEAD), 1)
    key_idx = lax.broadcasted_iota(jnp.int32, (kb, two), 0)
    qry_idx = lax.broadcasted_iota(jnp.int32, (kb, two), 1) & (sb - 1)
    lam_p = lam_ref[...]
    lam = (jnp.exp(jnp.sum(lam_p[0:1] * lam_p[1:2], axis=-1, keepdims=True))
           - jnp.exp(jnp.sum(lam_p[2:3] * lam_p[3:4], axis=-1, keepdims=True)) + lambda_init)

    def scores(qslot, c, j):
        start = pl.multiple_of(j * kb, kb)
        return _dot_nt(k_ref[pl.ds(start, kb), :], qq_ref[qslot, c])

    def softmax_pv(c, j, st, st_max, m):
        m_new = jnp.maximum(m, st_max)
        alpha = jnp.exp2(m - m_new)
        pr = jnp.exp2(st - m_new)
        acc_ref[c] = alpha * acc_ref[c] + _dot(vt_ref[j], pr.astype(BF16))
        return m_new

    def stage_scores(qslot, slot, c, j):
        st = scores(qslot, c, j)
        s_ref[slot, c] = st
        smax_ref[slot, c] = jnp.max(st, axis=0, keepdims=True)

    def open_block(i):
        qslot = i % 2
        for c in range(n_sub):
            row = pl.multiple_of(i * qb + c * sb, sb)
            qc = q_ref[pl.ds(row, sb), :]
            zero = jnp.zeros_like(qc)
            qq_ref[qslot, c] = jnp.concatenate([jnp.where(lane < DIFF_HALF, qc, zero),
                                                jnp.where(lane < DIFF_HALF, zero, qc)], axis=0)
        for c in range(n_sub):
            stage_scores(qslot, 0, c, 0)

    def q_block(i, carry):
        qslot = i % 2
        for c in range(n_sub):
            acc_ref[c] = jnp.zeros((DIFF_HEAD + ATTN_ONES_ROWS, two), F32)

        def full_steps(j0, n_steps, ms):
            ms = list(ms)
            for h in range(n_steps):
                for c in range(n_sub):
                    stage_scores(qslot, 1 - h % 2, c, j0 + h + 1)
                for c in range(n_sub):
                    ms[c] = softmax_pv(c, j0 + h, s_ref[h % 2, c], smax_ref[h % 2, c], ms[c])
            return tuple(ms)

        init = tuple(jnp.full((1, two), NEG_INF, F32) for _ in range(n_sub))
        n_full = i * steps_per_q
        n_short = (n_full % ATTN_TRIP) // 2
        ms = lax.fori_loop(0, n_short, lambda t, cr: full_steps(2 * t, 2, cr), init)
        ms = list(lax.fori_loop(0, n_full // ATTN_TRIP,
                                lambda t, cr: full_steps(2 * n_short + ATTN_TRIP * t, ATTN_TRIP, cr), ms))

        first_diag = n_full
        for s in range(steps_per_q):
            transpose_block(first_diag + s)
        needed = lambda s, c: s * kb < (c + 1) * sb
        tail = {(s, c): scores(qslot, c, first_diag + s)
                for s in range(1, steps_per_q) for c in range(n_sub) if needed(s, c)}
        for s in range(steps_per_q):
            for c in range(n_sub):
                if not needed(s, c):
                    continue
                st = s_ref[0, c] if s == 0 else tail[(s, c)]
                if (s + 1) * kb - 1 > c * sb:
                    st = jnp.where(key_idx + s * kb <= qry_idx + c * sb, st, NEG_INF)
                    st_max = jnp.max(st, axis=0, keepdims=True)
                else:
                    st_max = smax_ref[0, c] if s == 0 else jnp.max(st, axis=0, keepdims=True)
                ms[c] = softmax_pv(c, first_diag + s, st, st_max, ms[c])

        open_block(jnp.minimum(i + 1, nq - 1))

        for c in range(n_sub):
            acc = acc_ref[c]
            o = acc[:DIFF_HEAD] / acc[DIFF_HEAD:DIFF_HEAD + 1]
            d = o[:, :sb] - lam * o[:, sb:]
            ms_d = jnp.mean(d * d, axis=0, keepdims=True)
            d = d * lax.rsqrt(ms_d + SUBLN_EPS) * g_ref[...] * (1.0 - lambda_init)
            row = pl.multiple_of(i * qb + c * sb, sb)
            o_ref[pl.ds(row, sb), :] = d.T.astype(o_ref.dtype)
        return carry

    open_block(0)
    lax.fori_loop(0, nq, q_block, 0)


def _diff_attention(qkv, lam_params, subln_g_col, *, batch, seq, n_sub, lambda_init):
    qb = n_sub * ATTN_SUB
    assert qb % (2 * ATTN_KEYS) == 0 and seq % qb == 0
    h = DIFF_HEADS
    head_cols = lambda off: pl.BlockSpec((seq, DIFF_HEAD), lambda b, hh: (b, off + hh))
    return pl.pallas_call(
        functools.partial(_attn_kernel, seq=seq, n_sub=n_sub, lambda_init=lambda_init),
        grid=(batch, h),
        in_specs=[
            pl.BlockSpec((8, LANES), lambda b, hh: (0, 0)),
            pl.BlockSpec((DIFF_HEAD, 1), lambda b, hh: (0, 0)),
            head_cols(0), head_cols(h), head_cols(2 * h),
        ],
        out_specs=head_cols(0),
        out_shape=jax.ShapeDtypeStruct((batch * seq, DIFF_WIDTH), BF16),
        scratch_shapes=[pltpu.VMEM((seq // ATTN_KEYS, DIFF_HEAD + ATTN_ONES_ROWS, ATTN_KEYS), BF16),
                        pltpu.VMEM((n_sub, DIFF_HEAD + ATTN_ONES_ROWS, 2 * ATTN_SUB), F32),
                        pltpu.VMEM((2, n_sub, 2 * ATTN_SUB, DIFF_HEAD), BF16),
                        pltpu.VMEM((2, n_sub, ATTN_KEYS, 2 * ATTN_SUB), F32),
                        pltpu.VMEM((2, n_sub, 1, 2 * ATTN_SUB), F32)],
        compiler_params=_cparams(("arbitrary", "arbitrary")),
        name="diff_attention",
    )(lam_params, subln_g_col, qkv, qkv, qkv)


def _outproj2_kernel(ya_ref, yb_ref, x_ref, wa_ref, wb_ref, o_ref):
    o_ref[...] = x_ref[...] + _dot(ya_ref[...], wa_ref[...]) + _dot(yb_ref[...], wb_ref[...])


def _outproj2(ya, yb, x, w_all, *, layer, tm):
    m, d = x.shape
    half = w_all.shape[1] // 2
    return pl.pallas_call(
        _outproj2_kernel,
        grid=(m // tm,),
        in_specs=[
            pl.BlockSpec((tm, ya.shape[1]), lambda i: (i, 0)),
            pl.BlockSpec((tm, yb.shape[1]), lambda i: (i, 0)),
            pl.BlockSpec((tm, d), lambda i: (i, 0)),
            pl.BlockSpec((None, half, d), lambda i: (layer, 0, 0)),
            pl.BlockSpec((None, half, d), lambda i: (layer, 1, 0)),
        ],
        out_specs=pl.BlockSpec((tm, d), lambda i: (i, 0)),
        out_shape=jax.ShapeDtypeStruct((m, d), F32),
        compiler_params=_cparams(("arbitrary",)),
        name="even_out_proj",
    )(ya, yb, x, w_all, w_all)


def _sgu_kernel(u_ref, v_ref, x_ref, lng_ref, lnb_ref, ws_ref, bs_ref, wo_ref, o_ref, vn_ref, gated_ref, *, tm):
    ch = GMLP_CHUNK
    ti = lax.broadcasted_iota(jnp.int32, (ch, ch), 0)
    tj = lax.broadcasted_iota(jnp.int32, (ch, ch), 1)
    causal = ti >= tj
    half = tm // 2
    for r0 in range(0, tm, half):
        part = slice(r0, r0 + half)
        v = v_ref[part, :].astype(F32)
        mu = jnp.mean(v, axis=-1, keepdims=True)
        vc = v - mu
        var = jnp.mean(vc * vc, axis=-1, keepdims=True)
        vn_ref[part, :] = (vc * lax.rsqrt(var + LN_EPS) * lng_ref[...] + lnb_ref[...]).astype(BF16)
        for g in range(GMLP_GROUPS):
            cols = slice(g * LANES, (g + 1) * LANES)
            wg = jnp.where(causal, ws_ref[g], 0.0).astype(BF16)
            bias = bs_ref[:, g:g + 1]
            for c in range(half // ch):
                rows = slice(r0 + c * ch, r0 + (c + 1) * ch)
                sv = _dot(wg, vn_ref[rows, cols]) + bias
                gated_ref[rows, cols] = (u_ref[rows, cols].astype(F32) * sv).astype(BF16)
        o_ref[part, :] = x_ref[part, :] + _dot(gated_ref[part, :], wo_ref[...])


def _sgu_outproj(u_v, x, ln_g, ln_b, w_s_all, b_s_t, w_out_all, *, layer, tm):
    m, d = x.shape
    n_half = GMLP_WIDTH // d
    return pl.pallas_call(
        functools.partial(_sgu_kernel, tm=tm),
        grid=(m // tm,),
        in_specs=[
            pl.BlockSpec((tm, GMLP_WIDTH), lambda i: (i, 0)),
            pl.BlockSpec((tm, GMLP_WIDTH), lambda i: (i, n_half)),
            pl.BlockSpec((tm, d), lambda i: (i, 0)),
            pl.BlockSpec((1, GMLP_WIDTH), lambda i: (0, 0)),
            pl.BlockSpec((1, GMLP_WIDTH), lambda i: (0, 0)),
            pl.BlockSpec((None,) + w_s_all.shape[1:], lambda i: (layer, 0, 0, 0)),
            pl.BlockSpec(b_s_t.shape, lambda i: (0, 0)),
            pl.BlockSpec((None,) + w_out_all.shape[1:], lambda i: (layer, 0, 0)),
        ],
        out_specs=pl.BlockSpec((tm, d), lambda i: (i, 0)),
        out_shape=jax.ShapeDtypeStruct((m, d), F32),
        scratch_shapes=[pltpu.VMEM((tm, GMLP_WIDTH), BF16), pltpu.VMEM((tm, GMLP_WIDTH), BF16)],
        compiler_params=_cparams(("arbitrary",)),
        name="sgu_out_proj",
    )(u_v, u_v, x, ln_g, ln_b, w_s_all, b_s_t, w_out_all)


def _ffn_kernel(x_ref, g_ref, wg_ref, wu_ref, wd_ref, fg_ref, o_ref, h_ref, *, final):
    j = pl.program_id(1)

    def hidden_tile(rows):
        h = h_ref[rows, :]
        gate = _dot(h, wg_ref[...])
        up = _dot(h, wu_ref[...])
        act = gate * _sigmoid(gate) * up
        o_ref[rows, :] += _dot(act.astype(BF16), wd_ref[...].astype(BF16))

    @pl.when(j == 0)
    def _():
        for rows in _row_parts(x_ref.shape[0]):
            x = x_ref[rows, :]
            h_ref[rows, :] = _rms_norm_rows(x, g_ref[...], NORM_EPS).astype(BF16)
            o_ref[rows, :] = x
            hidden_tile(rows)

    last = pl.num_programs(1) - 1
    if final:
        @pl.when(jnp.logical_and(j > 0, j < last))
        def _():
            hidden_tile(slice(None))

        @pl.when(j == last)
        def _():
            for rows in _row_parts(x_ref.shape[0]):
                hidden_tile(rows)
                o_ref[rows, :] = _rms_norm_rows(o_ref[rows, :], fg_ref[...], NORM_EPS)
    else:
        @pl.when(j > 0)
        def _():
            hidden_tile(slice(None))


def _ffn(x, g, w_gate_all, w_up_all, w_down_all, final_g, *, layer, tm, th, final):
    m, d = x.shape
    hid = w_gate_all.shape[2]
    return pl.pallas_call(
        functools.partial(_ffn_kernel, final=final),
        grid=(m // tm, hid // th),
        in_specs=[
            pl.BlockSpec((tm, d), lambda i, j: (i, 0)),
            pl.BlockSpec((1, d), lambda i, j: (0, 0)),
            pl.BlockSpec((None, d, th), lambda i, j: (layer, 0, j)),
            pl.BlockSpec((None, d, th), lambda i, j: (layer, 0, j)),
            pl.BlockSpec((None, th, d), lambda i, j: (layer, j, 0)),
            pl.BlockSpec((1, d), lambda i, j: (0, 0)),
        ],
        out_specs=pl.BlockSpec((tm, d), lambda i, j: (i, 0)),
        out_shape=jax.ShapeDtypeStruct((m, d), F32),
        scratch_shapes=[pltpu.VMEM((tm, d), BF16)],
        compiler_params=_cparams(("arbitrary", "arbitrary")),
        name="swiglu_ffn",
    )(x, g, w_gate_all, w_up_all, w_down_all, final_g)


def _pad_cols(a, width):
    return jnp.pad(a, ((0, 0), (0, width - a.shape[1])))


def _pad_rows(a, height):
    return jnp.pad(a, ((0, height - a.shape[0]), (0, 0)))


def _rwkv_col_layout(a):
    c = RWKV_WIDTH
    xw = a[:, 3 * c:3 * c + DECAY_LORA]
    xa = a[:, 3 * c + DECAY_LORA:3 * c + DECAY_LORA + AAA_LORA]
    xg = a[:, 3 * c + DECAY_LORA + AAA_LORA:]
    return jnp.concatenate(
        [a[:, :3 * c], _pad_cols(xw, LANES), _pad_cols(xa, LANES), _pad_cols(xg, 2 * LANES)], axis=1)


def _rwkv_weight_layout(w_in):
    c = RWKV_WIDTH
    out = jnp.zeros((w_in.shape[0], RW_PAD_COLS), BF16)
    out = lax.dynamic_update_slice(out, w_in[:, :3 * c].astype(BF16), (0, 0))
    starts = (3 * c, 3 * c + DECAY_LORA, 3 * c + DECAY_LORA + AAA_LORA, RWKV_COLS)
    for src0, src1, dst in zip(starts[:-1], starts[1:], (RW_XW, RW_XA, RW_XG)):
        out = lax.dynamic_update_slice(out, w_in[:, src0:src1].astype(BF16), (0, dst))
    return out


def _rope_tables(seq):
    inv = ROPE_THETA ** (-jnp.arange(0, DIFF_HALF, 2, dtype=F32) / DIFF_HALF)
    ang = jnp.arange(seq, dtype=F32)[:, None] * inv[None, :]
    cos, sin = jnp.cos(ang), jnp.sin(ang)
    reps = LANES // DIFF_HALF
    return (jnp.tile(jnp.concatenate([cos, cos], axis=1), (1, reps)),
            jnp.tile(jnp.concatenate([-sin, sin], axis=1), (1, reps)))


def _tile_sizes(m, seq):
    return dict(
        proj_tm=min(1024, seq), proj_tn=1024, rwkv_proj_tn=RW_PAD_COLS // 2, gelu_proj_tn=2048,
        ffn_tm=min(1024, m), ffn_th=512,
        out_tm=min(512, m), sgu_tm=min(512, m),
        attn_sub=min(4, seq // ATTN_SUB),
    )


def kernel(x, mix_norm, ffn_norm, ffn_w_gate, ffn_w_up, ffn_w_down, ev_w_in, ev_mu, ev_w0, ev_w_dec_up, ev_a0,
           ev_w_a_up, ev_w_g_up, ev_k_k, ev_k_a, ev_r_k, ev_lnx_w, ev_lnx_b, ev_lam_q1, ev_lam_k1, ev_lam_q2,
           ev_lam_k2, ev_subln_g, ev_w_out, od_w_in, od_ln_g, od_ln_b, od_w_s, od_b_s, od_w_out, final_norm):
    batch, seq, d = x.shape
    m = batch * seq
    ts = _tile_sizes(m, seq)
    cos, sin = _rope_tables(seq)
    row = lambda a: a.reshape(1, -1).astype(F32)
    xf = x.reshape(m, d)
    ffn_w_gate_bf, ffn_w_up_bf, ffn_w_down_f32 = ffn_w_gate.astype(BF16), ffn_w_up.astype(BF16), ffn_w_down.astype(F32)
    ev_w_out_bf, od_w_in_bf, od_w_out_bf = ev_w_out.astype(BF16), od_w_in.astype(BF16), od_w_out.astype(BF16)
    od_w_s_f32 = od_w_s.astype(F32)
    for i in range(DEPTH):
        j = i // 2
        g_mix = row(mix_norm[i])
        if i % 2 == 0:
            lambda_init = 0.8 - 0.6 * math.exp(-0.3 * i)
            w_in = ev_w_in[j]
            w_rwkv = _rwkv_weight_layout(w_in)
            w_diff = w_in[:, RWKV_COLS:].astype(BF16)
            z = _norm_matmul(xf, g_mix, w_rwkv, tm=ts["proj_tm"], tn=ts["rwkv_proj_tn"], out_dtype=BF16)
            qkv = _norm_matmul_rope(xf, g_mix, w_diff, cos, sin, tm=ts["proj_tm"], tn=ts["proj_tn"], seq=seq)
            y_a = _rwkv_time_mix(
                z, _rwkv_col_layout(row(ev_mu[j])), row(ev_w0[j]),
                _pad_rows(ev_w_dec_up[j], LANES).astype(BF16), row(ev_a0[j]),
                _pad_rows(ev_w_a_up[j], LANES).astype(BF16), _pad_rows(ev_w_g_up[j], 2 * LANES).astype(BF16),
                row(ev_k_k[j]), row(ev_k_a[j]), row(ev_r_k[j]), row(ev_lnx_w[j]), row(ev_lnx_b[j]),
                batch=batch, seq=seq)
            lam_params = _pad_rows(_pad_cols(
                jnp.stack([ev_lam_q1[j], ev_lam_k1[j], ev_lam_q2[j], ev_lam_k2[j]]).astype(F32), LANES), 8)
            y_b = _diff_attention(qkv, lam_params, ev_subln_g[j].reshape(-1, 1).astype(F32), batch=batch, seq=seq,
                                  n_sub=ts["attn_sub"], lambda_init=lambda_init)
            xf = _outproj2(y_a, y_b, xf, ev_w_out_bf, layer=j, tm=ts["out_tm"])
        else:
            u_v = _norm_matmul(xf, g_mix, od_w_in_bf, tm=ts["proj_tm"], tn=ts["gelu_proj_tn"],
                               out_dtype=BF16, gelu=True, layer=j)
            xf = _sgu_outproj(u_v, xf, row(od_ln_g[j]), row(od_ln_b[j]), od_w_s_f32,
                              od_b_s[j].T.astype(F32), od_w_out_bf, layer=j, tm=ts["sgu_tm"])
        xf = _ffn(xf, row(ffn_norm[i]), ffn_w_gate_bf, ffn_w_up_bf, ffn_w_down_f32, row(final_norm),
                  layer=i, tm=ts["ffn_tm"], th=ts["ffn_th"], final=(i == DEPTH - 1))
    return xf.reshape(batch, seq, d)
```

```python
import functools
import math

import jax
import jax.numpy as jnp
from jax import lax
from jax.experimental import pallas as pl
from jax.experimental.pallas import tpu as pltpu

F32 = jnp.float32
BF16 = jnp.bfloat16

D_MODEL = 2048
DEPTH = 4
RWKV_WIDTH = D_MODEL // 2
RWKV_HEAD = 64
DECAY_LORA = 64
AAA_LORA = 64
GATE_LORA = 160
RWKV_COLS = 3 * RWKV_WIDTH + DECAY_LORA + AAA_LORA + GATE_LORA
RWKV_LNX_EPS = 64e-5
DIFF_WIDTH = D_MODEL - RWKV_WIDTH
DIFF_HEAD = 128
DIFF_HALF = DIFF_HEAD // 2
DIFF_HEADS = DIFF_WIDTH // DIFF_HEAD
ROPE_THETA = 10000.0
NEG_INF = -1e30
GMLP_WIDTH = D_MODEL
GMLP_CHUNK = 128
GMLP_GROUPS = 16
NORM_EPS = 1e-6
SUBLN_EPS = 1e-5
LN_EPS = 1e-5

LANES = 128
MXU_COLS = 256
PROLOGUE_PARTS = 4
VMEM_LIMIT_BYTES = 60 * 1024 * 1024

RW_XW = 3 * RWKV_WIDTH
RW_XA = RW_XW + LANES
RW_XG = RW_XA + LANES
RW_PAD_COLS = RW_XG + 2 * LANES

RWKV_CHUNK = 64
RWKV_CHUNKS_PER_STEP = 8
RWKV_CHUNKS_PER_TRIP = 2
ATTN_SUB = 128
ATTN_KEYS = 256
ATTN_ONES_ROWS = 16
ATTN_TRIP = 8
LOG2_E = 1.4426950408889634
PAIR = 2 * RWKV_HEAD


def _cparams(semantics):
    return pltpu.CompilerParams(dimension_semantics=semantics, vmem_limit_bytes=VMEM_LIMIT_BYTES)


def _dot(a, b):
    return jnp.dot(a, b, preferred_element_type=F32)


def _dot_nt(a, b):
    return lax.dot_general(a, b, (((1,), (1,)), ((), ())), preferred_element_type=F32)


def _rms_norm_rows(x, g, eps):
    ms = jnp.mean(x * x, axis=-1, keepdims=True)
    return x * lax.rsqrt(ms + eps) * g


def _row_parts(tm):
    part = tm // PROLOGUE_PARTS
    return [slice(r0, r0 + part) for r0 in range(0, tm, part)]


def _norm_matmul_kernel(x_ref, g_ref, w_ref, o_ref, h_ref, *, tn, gelu):
    j = pl.program_id(1)

    def columns(rows):
        for n0 in range(0, tn, MXU_COLS):
            acc = _dot(h_ref[rows, :], w_ref[:, n0:n0 + MXU_COLS])
            if gelu:
                acc = 0.5 * acc * (1.0 + lax.erf(acc * math.sqrt(0.5)))
            o_ref[rows, n0:n0 + MXU_COLS] = acc.astype(o_ref.dtype)

    @pl.when(j == 0)
    def _():
        for rows in _row_parts(x_ref.shape[0]):
            h_ref[rows, :] = _rms_norm_rows(x_ref[rows, :], g_ref[...], NORM_EPS).astype(BF16)
            columns(rows)

    @pl.when(j > 0)
    def _():
        columns(slice(None))


def _norm_matmul(x, g, w, *, tm, tn, out_dtype, gelu=False, layer=None):
    m, k = x.shape
    n = w.shape[-1]
    if layer is None:
        w_spec = pl.BlockSpec((k, tn), lambda i, j: (0, j))
    else:
        w_spec = pl.BlockSpec((None, k, tn), lambda i, j: (layer, 0, j))
    return pl.pallas_call(
        functools.partial(_norm_matmul_kernel, tn=tn, gelu=gelu),
        grid=(m // tm, n // tn),
        in_specs=[
            pl.BlockSpec((tm, k), lambda i, j: (i, 0)),
            pl.BlockSpec((1, k), lambda i, j: (0, 0)),
            w_spec,
        ],
        out_specs=pl.BlockSpec((tm, tn), lambda i, j: (i, j)),
        out_shape=jax.ShapeDtypeStruct((m, n), out_dtype),
        scratch_shapes=[pltpu.VMEM((tm, k), BF16)],
        compiler_params=_cparams(("arbitrary", "arbitrary")),
        name="norm_matmul_gelu" if gelu else "norm_matmul",
    )(x, g, w)


def _norm_matmul_rope_kernel(x_ref, g_ref, w_ref, cos_ref, sin_ref, o_ref, h_ref, *, tn, q_tiles, rope_tiles):
    j = pl.program_id(1)

    def columns(rows, epilogue):
        for n0 in range(0, tn, MXU_COLS):
            acc = _dot(h_ref[rows, :], w_ref[:, n0:n0 + MXU_COLS])
            o_ref[rows, n0:n0 + MXU_COLS] = epilogue(rows, acc).astype(o_ref.dtype)

    def rope(rows, acc):
        reps = MXU_COLS // LANES
        c = jnp.concatenate([cos_ref[rows, :]] * reps, axis=1)
        s = jnp.concatenate([sin_ref[rows, :]] * reps, axis=1)
        lane = lax.broadcasted_iota(jnp.int32, acc.shape, 1)
        partner = jnp.where((lane & (DIFF_HALF // 2)) == 0,
                            pltpu.roll(acc, MXU_COLS - DIFF_HALF // 2, 1),
                            pltpu.roll(acc, DIFF_HALF // 2, 1))
        scale = jnp.where(j < q_tiles, DIFF_HALF ** -0.5 * LOG2_E, 1.0).astype(F32)
        return (acc * c + partner * s) * scale

    @pl.when(j == 0)
    def _():
        for rows in _row_parts(x_ref.shape[0]):
            h_ref[rows, :] = _rms_norm_rows(x_ref[rows, :], g_ref[...], NORM_EPS).astype(BF16)
            columns(rows, rope)

    @pl.when(jnp.logical_and(j > 0, j < rope_tiles))
    def _():
        columns(slice(None), rope)

    @pl.when(j >= rope_tiles)
    def _():
        columns(slice(None), lambda rows, acc: acc)


def _norm_matmul_rope(x, g, w, cos, sin, *, tm, tn, seq):
    m, k = x.shape
    n = w.shape[1]
    q_tiles = DIFF_WIDTH // tn
    t_tiles = seq // tm
    return pl.pallas_call(
        functools.partial(_norm_matmul_rope_kernel, tn=tn, q_tiles=q_tiles, rope_tiles=2 * q_tiles),
        grid=(m // tm, n // tn),
        in_specs=[
            pl.BlockSpec((tm, k), lambda i, j: (i, 0)),
            pl.BlockSpec((1, k), lambda i, j: (0, 0)),
            pl.BlockSpec((k, tn), lambda i, j: (0, j)),
            pl.BlockSpec((tm, LANES), lambda i, j: (i % t_tiles, 0)),
            pl.BlockSpec((tm, LANES), lambda i, j: (i % t_tiles, 0)),
        ],
        out_specs=pl.BlockSpec((tm, tn), lambda i, j: (i, j)),
        out_shape=jax.ShapeDtypeStruct((m, n), BF16),
        scratch_shapes=[pltpu.VMEM((tm, k), BF16)],
        compiler_params=_cparams(("arbitrary", "arbitrary")),
        name="norm_matmul_rope",
    )(x, g, w, cos, sin)


def _sigmoid(y):
    return 1.0 / (1.0 + jnp.exp(-y))


def _split_bf16(x, parts):
    out = []
    rem = x
    for _ in range(parts):
        p = rem.astype(BF16)
        out.append(p)
        rem = rem - p.astype(F32)
    return out


def _rwkv_kernel(z_ref, mu_ref, w0_ref, wdec_ref, a0_ref, wa_ref, wg_ref, kk_ref, ka_ref, rk_ref,
                 lw_ref, lb_ref, o_ref, s_ref, prev_ref):
    L = RWKV_CHUNK
    C = RWKV_WIDTH
    n_pairs = C // PAIR
    n_batch = z_ref.shape[0]

    @pl.when(pl.program_id(0) == 0)
    def _():
        s_ref[...] = jnp.zeros_like(s_ref)
        prev_ref[...] = jnp.zeros_like(prev_ref)

    ti = lax.broadcasted_iota(jnp.int32, (L, L), 0)
    tj = lax.broadcasted_iota(jnp.int32, (L, L), 1)
    tril_ones = jnp.where(ti >= tj, 1.0, 0.0).astype(BF16)

    def prepare(bi, row0):
        z = z_ref[bi, pl.ds(row0, L), :].astype(F32)
        row = lax.broadcasted_iota(jnp.int32, z.shape, 0)
        z_prev = jnp.where(row == 0, prev_ref[bi, 0:1, :], pltpu.roll(z, 1, 0))
        prev_ref[bi, 0:1, :] = z[L - 1:L, :]
        zs = z + (z_prev - z) * mu_ref[...]

        r = zs[:, 0:C]
        k = zs[:, C:2 * C]
        v = zs[:, 2 * C:3 * C]
        xw = zs[:, RW_XW:RW_XA]
        xa = zs[:, RW_XA:RW_XG]
        xg = zs[:, RW_XG:RW_PAD_COLS]

        y_dec = w0_ref[...] + _dot(jnp.tanh(xw).astype(BF16), wdec_ref[...])
        logd = -math.exp(-0.5) * _sigmoid(y_dec)
        a_lr = _sigmoid(a0_ref[...] + _dot(xa.astype(BF16), wa_ref[...]))
        gate = _dot(_sigmoid(xg).astype(BF16), wg_ref[...])
        kk_raw = k * kk_ref[...]
        k2 = k * (1.0 + (a_lr - 1.0) * ka_ref[...])
        rkk = r * k2 * rk_ref[...]
        cum = sum(_dot(tril_ones, p) for p in _split_bf16(logd, 2))
        return r, k2, v, logd, a_lr, gate, kk_raw, rkk, cum

    rr = lax.broadcasted_iota(jnp.int32, (PAIR, PAIR), 0)
    cc = lax.broadcasted_iota(jnp.int32, (PAIR, PAIR), 1)
    same_head = (rr // RWKV_HEAD) == (cc // RWKV_HEAD)
    strict = jnp.logical_and(same_head, rr > cc)
    incl = jnp.logical_and(same_head, rr >= cc)
    head_ones = jnp.where(same_head, 1.0, 0.0).astype(BF16)
    eye = jnp.where(rr == cc, 1.0, 0.0).astype(F32)
    first_head = lax.broadcasted_iota(jnp.int32, (L, PAIR), 1) < RWKV_HEAD

    def head_sum(x):
        return _dot(x.astype(BF16), head_ones)

    def stack(x):
        return jnp.concatenate([jnp.where(first_head, x, 0.0), jnp.where(first_head, 0.0, x)], axis=0)

    n_steps = int(math.log2(L)) - 1
    idx = range(n_pairs)
    sls = [slice(p * PAIR, (p + 1) * PAIR) for p in idx]

    def elementwise(prep, p):
        r, k2, v, logd, a_lr, _, kk_raw, rkk, cum = prep
        sl = sls[p]
        kk_p = kk_raw[:, sl]
        kk_n = kk_p * lax.rsqrt(jnp.maximum(head_sum(kk_p * kk_p), 1e-24))
        bonus = head_sum(rkk[:, sl])
        a_p = -kk_n
        b_p = kk_n * a_lr[:, sl]
        r_p, k_p = r[:, sl], k2[:, sl]
        cu = cum[:, sl]
        c_end = cu[L - 1:L, :]
        e_pos = jnp.exp(cu)
        e_neg = jnp.exp(-cu)
        e_prev = jnp.exp(cu - logd[:, sl])
        e_end = jnp.exp(c_end - cu)
        lhs = jnp.concatenate([stack(a_p * e_prev), stack(r_p * e_pos)], axis=0).astype(BF16)
        rhs = jnp.concatenate([stack(b_p * e_neg), stack(k_p * e_neg)], axis=0).astype(BF16)
        bk_end = jnp.concatenate([stack(b_p * e_end), stack(k_p * e_end)], axis=0).astype(BF16)
        return lhs, rhs, bk_end, stack(v[:, sl]), jnp.exp(c_end), bonus

    def stages(bi, row0, prep, elems, filler):
        v, gate = prep[2], prep[5]
        lhs, rhs, bk_end, v_st, d_end, bonus_s = zip(*elems)
        pm = [_dot_nt(lhs[i], rhs[i]) for i in idx]
        filler()
        s_old = [s_ref[bi, p] for p in idx]
        sh = [_dot_nt(lhs[i], s_old[i].astype(BF16)) for i in idx]
        filler()
        p_ab = [jnp.where(strict, pm[i][:PAIR, :PAIR], 0.0) for i in idx]
        p_ak = [jnp.where(strict, pm[i][:PAIR, PAIR:], 0.0).astype(BF16) for i in idx]
        p_rbk = [jnp.concatenate([jnp.where(incl, pm[i][PAIR:, :PAIR], 0.0),
                                  jnp.where(incl, pm[i][PAIR:, PAIR:], 0.0)], axis=1).astype(BF16) for i in idx]
        akv = [_dot(p_ak[i], v_st[i].astype(BF16)) for i in idx]
        filler()

        t_inv = [eye + p_ab[i] for i in idx]
        xb = [p_ab[i].astype(BF16) for i in idx]
        xb = [_dot(xb[i], xb[i]).astype(BF16) for i in idx]
        filler()
        for it in range(n_steps):
            if it + 1 < n_steps:
                both = [_dot(xb[i], jnp.concatenate([xb[i], t_inv[i].astype(BF16)], axis=1)) for i in idx]
                xb = [both[i][:, :PAIR].astype(BF16) for i in idx]
                t_inv = [t_inv[i] + both[i][:, PAIR:] for i in idx]
            else:
                t_inv = [t_inv[i] + _dot(xb[i], t_inv[i].astype(BF16)) for i in idx]
            filler()

        u_st = [_dot(t_inv[i].astype(BF16), (sh[i][:PAIR] + akv[i]).astype(BF16)) for i in idx]
        filler()
        uv = [jnp.concatenate([u_st[i], v_st[i]], axis=0) for i in idx]
        y_st = [sh[i][PAIR:] + _dot(p_rbk[i], uv[i].astype(BF16)) for i in idx]
        filler()
        s_upd = [_dot(uv[i].T.astype(BF16), bk_end[i]) for i in idx]
        for p in idx:
            s_ref[bi, p] = s_old[p] * d_end[p] + s_upd[p]

        y = [y_st[i][:L] + y_st[i][L:] for i in idx]
        mean = [head_sum(y[i]) * (1.0 / RWKV_HEAD) for i in idx]
        yc = [y[i] - mean[i] for i in idx]
        var = [head_sum(yc[i] * yc[i]) * (1.0 / RWKV_HEAD) for i in idx]
        for i in idx:
            sl = sls[i]
            yn = yc[i] * lax.rsqrt(var[i] + RWKV_LNX_EPS) * lw_ref[:, sl] + lb_ref[:, sl]
            o_ref[bi, pl.ds(row0, L), sl] = ((yn + bonus_s[i] * v[:, sl]) * gate[:, sl]).astype(o_ref.dtype)

    def trip(ti, carry):
        rows = [pl.multiple_of((ti * RWKV_CHUNKS_PER_TRIP + k) * L, L) for k in range(RWKV_CHUNKS_PER_TRIP)]
        units = [(k, bi) for k in range(RWKV_CHUNKS_PER_TRIP) for bi in range(n_batch)]
        prepared, elems = {}, {u: [] for u in units}

        def work_before(u):
            k = u[0]
            todo = [functools.partial(lambda v: prepared.__setitem__(v, prepare(v[1], rows[v[0]])), v)
                    for v in units if v[0] == k and v not in prepared and v not in scheduled]
            scheduled.update(v for v in units if v[0] == k)
            todo += [functools.partial(lambda v, p: elems[v].append(elementwise(prepared[v], p)), u, p) for p in idx]
            return todo

        scheduled = set()
        for thunk in work_before(units[0]):
            thunk()
        for n, u in enumerate(units):
            queue = work_before(units[n + 1]) if n + 1 < len(units) else []

            def filler():
                if queue:
                    queue.pop(0)()

            stages(u[1], rows[u[0]], prepared[u], elems[u], filler)
            while queue:
                queue.pop(0)()
        return carry

    lax.fori_loop(0, z_ref.shape[1] // (L * RWKV_CHUNKS_PER_TRIP), trip, 0)


def _rwkv_time_mix(z, mu, w0, wdec, a0, wa, wg, k_k, k_a, r_k, lnx_w, lnx_b, *, batch, seq):
    rows = min(RWKV_CHUNK * RWKV_CHUNKS_PER_STEP, seq)
    n_pairs = RWKV_WIDTH // PAIR
    full = lambda shape: pl.BlockSpec(shape, lambda c: (0,) * len(shape))
    out = pl.pallas_call(
        _rwkv_kernel,
        grid=(seq // rows,),
        in_specs=[
            pl.BlockSpec((batch, rows, RW_PAD_COLS), lambda c: (0, c, 0)),
            full((1, RW_PAD_COLS)),
            full((1, RWKV_WIDTH)), full((LANES, RWKV_WIDTH)),
            full((1, RWKV_WIDTH)), full((LANES, RWKV_WIDTH)),
            full((2 * LANES, RWKV_WIDTH)),
            full((1, RWKV_WIDTH)), full((1, RWKV_WIDTH)), full((1, RWKV_WIDTH)),
            full((1, RWKV_WIDTH)), full((1, RWKV_WIDTH)),
        ],
        out_specs=pl.BlockSpec((batch, rows, RWKV_WIDTH), lambda c: (0, c, 0)),
        out_shape=jax.ShapeDtypeStruct((batch, seq, RWKV_WIDTH), BF16),
        scratch_shapes=[pltpu.VMEM((batch, n_pairs, PAIR, PAIR), F32), pltpu.VMEM((batch, 8, RW_PAD_COLS), F32)],
        compiler_params=_cparams(("arbitrary",)),
        name="rwkv7_time_mix",
    )(z.reshape(batch, seq, RW_PAD_COLS), mu, w0, wdec, a0, wa, wg, k_k, k_a, r_k, lnx_w, lnx_b)
    return out.reshape(batch * seq, RWKV_WIDTH)


def _attn_kernel(lam_ref, g_ref, q_ref, k_ref, v_ref, o_ref, vt_ref, acc_ref, qq_ref, s_ref, smax_ref, *, seq,
                 n_sub, lambda_init):
    sb = ATTN_SUB
    two = 2 * sb
    kb = ATTN_KEYS
    qb = n_sub * sb
    steps_per_q = qb // kb
    nq = seq // qb

    def transpose_block(t):
        start = pl.multiple_of(t * kb, kb)
        vt_ref[t, :DIFF_HEAD, :] = v_ref[pl.ds(start, kb), :].astype(F32).T.astype(BF16)
        vt_ref[t, DIFF_HEAD:, :] = jnp.ones((ATTN_ONES_ROWS, kb), BF16)

    lane = lax.broadcasted_iota(jnp.int32, (sb, DIFF_HEAD), 1)
    key_idx = lax.broadcasted_iota(jnp.int32, (kb, two), 0)
    qry_idx = lax.broadcasted_iota(jnp.int32, (kb, two), 1) & (sb - 1)
    lam_p = lam_ref[...]
    lam = (jnp.exp(jnp.sum(lam_p[0:1] * lam_p[1:2], axis=-1, keepdims=True))
           - jnp.exp(jnp.sum(lam_p[2:3] * lam_p[3:4], axis=-1, keepdims=True)) + lambda_init)

    def scores(qslot, c, j):
        start = pl.multiple_of(j * kb, kb)
        return _dot_nt(k_ref[pl.ds(start, kb), :], qq_ref[qslot, c])

    def softmax_pv(c, j, st, st_max, m):
        m_new = jnp.maximum(m, st_max)
        alpha = jnp.exp2(m - m_new)
        pr = jnp.exp2(st - m_new)
        acc_ref[c] = alpha * acc_ref[c] + _dot(vt_ref[j], pr.astype(BF16))
        return m_new

    def stage_scores(qslot, slot, c, j):
        st = scores(qslot, c, j)
        s_ref[slot, c] = st
        smax_ref[slot, c] = jnp.max(st, axis=0, keepdims=True)

    def open_block(i):
        qslot = i % 2
        for c in range(n_sub):
            row = pl.multiple_of(i * qb + c * sb, sb)
            qc = q_ref[pl.ds(row, sb), :]
            zero = jnp.zeros_like(qc)
            qq_ref[qslot, c] = jnp.concatenate([jnp.where(lane < DIFF_HALF, qc, zero),
                                                jnp.where(lane < DIFF_HALF, zero, qc)], axis=0)
        for c in range(n_sub):
            stage_scores(qslot, 0, c, 0)

    def q_block(i, carry):
        qslot = i % 2
        for c in range(n_sub):
            acc_ref[c] = jnp.zeros((DIFF_HEAD + ATTN_ONES_ROWS, two), F32)

        def full_steps(j0, n_steps, ms):
            ms = list(ms)
            for h in range(n_steps):
                for c in range(n_sub):
                    stage_scores(qslot, 1 - h % 2, c, j0 + h + 1)
                for c in range(n_sub):
                    ms[c] = softmax_pv(c, j0 + h, s_ref[h % 2, c], smax_ref[h % 2, c], ms[c])
            return tuple(ms)

        init = tuple(jnp.full((1, two), NEG_INF, F32) for _ in range(n_sub))
        n_full = i * steps_per_q
        rem = n_full % ATTN_TRIP
        n2, n4 = (rem % 4) // 2, rem // 4
        ms = lax.fori_loop(0, n2, lambda t, cr: full_steps(0, 2, cr), init)
        ms = lax.fori_loop(0, n4, lambda t, cr: full_steps(2 * n2, 4, cr), ms)
        ms = list(lax.fori_loop(0, n_full // ATTN_TRIP,
                                lambda t, cr: full_steps(rem + ATTN_TRIP * t, ATTN_TRIP, cr), ms))

        first_diag = n_full
        for s in range(steps_per_q):
            transpose_block(first_diag + s)
        needed = lambda s, c: s * kb < (c + 1) * sb
        tail = {(s, c): scores(qslot, c, first_diag + s)
                for s in range(1, steps_per_q) for c in range(n_sub) if needed(s, c)}
        for s in range(steps_per_q):
            for c in range(n_sub):
                if not needed(s, c):
                    continue
                st = s_ref[0, c] if s == 0 else tail[(s, c)]
                if (s + 1) * kb - 1 > c * sb:
                    st = jnp.where(key_idx + s * kb <= qry_idx + c * sb, st, NEG_INF)
                    st_max = jnp.max(st, axis=0, keepdims=True)
                else:
                    st_max = smax_ref[0, c] if s == 0 else jnp.max(st, axis=0, keepdims=True)
                ms[c] = softmax_pv(c, first_diag + s, st, st_max, ms[c])

        open_block(jnp.minimum(i + 1, nq - 1))

        for c in range(n_sub):
            acc = acc_ref[c]
            o = acc[:DIFF_HEAD] / acc[DIFF_HEAD:DIFF_HEAD + 1]
            d = o[:, :sb] - lam * o[:, sb:]
            ms_d = jnp.mean(d * d, axis=0, keepdims=True)
            d = d * lax.rsqrt(ms_d + SUBLN_EPS) * g_ref[...] * (1.0 - lambda_init)
            row = pl.multiple_of(i * qb + c * sb, sb)
            o_ref[pl.ds(row, sb), :] = d.T.astype(o_ref.dtype)
        return carry

    open_block(0)
    lax.fori_loop(0, nq, q_block, 0)


def _diff_attention(qkv, lam_params, subln_g_col, *, batch, seq, n_sub, lambda_init):
    qb = n_sub * ATTN_SUB
    assert qb % (2 * ATTN_KEYS) == 0 and seq % qb == 0
    assert ATTN_TRIP == 8
    h = DIFF_HEADS
    head_cols = lambda off: pl.BlockSpec((seq, DIFF_HEAD), lambda b, hh: (b, off + hh))
    return pl.pallas_call(
        functools.partial(_attn_kernel, seq=seq, n_sub=n_sub, lambda_init=lambda_init),
        grid=(batch, h),
        in_specs=[
            pl.BlockSpec((8, LANES), lambda b, hh: (0, 0)),
            pl.BlockSpec((DIFF_HEAD, 1), lambda b, hh: (0, 0)),
            head_cols(0), head_cols(h), head_cols(2 * h),
        ],
        out_specs=head_cols(0),
        out_shape=jax.ShapeDtypeStruct((batch * seq, DIFF_WIDTH), BF16),
        scratch_shapes=[pltpu.VMEM((seq // ATTN_KEYS, DIFF_HEAD + ATTN_ONES_ROWS, ATTN_KEYS), BF16),
                        pltpu.VMEM((n_sub, DIFF_HEAD + ATTN_ONES_ROWS, 2 * ATTN_SUB), F32),
                        pltpu.VMEM((2, n_sub, 2 * ATTN_SUB, DIFF_HEAD), BF16),
                        pltpu.VMEM((2, n_sub, ATTN_KEYS, 2 * ATTN_SUB), F32),
                        pltpu.VMEM((2, n_sub, 1, 2 * ATTN_SUB), F32)],
        compiler_params=_cparams(("arbitrary", "arbitrary")),
        name="diff_attention",
    )(lam_params, subln_g_col, qkv, qkv, qkv)


def _outproj2_kernel(ya_ref, yb_ref, x_ref, wa_ref, wb_ref, o_ref):
    o_ref[...] = x_ref[...] + _dot(ya_ref[...], wa_ref[...]) + _dot(yb_ref[...], wb_ref[...])


def _outproj2(ya, yb, x, w_all, *, layer, tm):
    m, d = x.shape
    half = w_all.shape[1] // 2
    return pl.pallas_call(
        _outproj2_kernel,
        grid=(m // tm,),
        in_specs=[
            pl.BlockSpec((tm, ya.shape[1]), lambda i: (i, 0)),
            pl.BlockSpec((tm, yb.shape[1]), lambda i: (i, 0)),
            pl.BlockSpec((tm, d), lambda i: (i, 0)),
            pl.BlockSpec((None, half, d), lambda i: (layer, 0, 0)),
            pl.BlockSpec((None, half, d), lambda i: (layer, 1, 0)),
        ],
        out_specs=pl.BlockSpec((tm, d), lambda i: (i, 0)),
        out_shape=jax.ShapeDtypeStruct((m, d), F32),
        compiler_params=_cparams(("arbitrary",)),
        name="even_out_proj",
    )(ya, yb, x, w_all, w_all)


def _sgu_kernel(u_ref, v_ref, x_ref, lng_ref, lnb_ref, ws_ref, bs_ref, wo_ref, o_ref, vn_ref, gated_ref, *, tm):
    ch = GMLP_CHUNK
    ti = lax.broadcasted_iota(jnp.int32, (ch, ch), 0)
    tj = lax.broadcasted_iota(jnp.int32, (ch, ch), 1)
    causal = ti >= tj
    half = tm // 2
    for r0 in range(0, tm, half):
        part = slice(r0, r0 + half)
        v = v_ref[part, :].astype(F32)
        mu = jnp.mean(v, axis=-1, keepdims=True)
        vc = v - mu
        var = jnp.mean(vc * vc, axis=-1, keepdims=True)
        vn_ref[part, :] = (vc * lax.rsqrt(var + LN_EPS) * lng_ref[...] + lnb_ref[...]).astype(BF16)
        for g in range(GMLP_GROUPS):
            cols = slice(g * LANES, (g + 1) * LANES)
            wg = jnp.where(causal, ws_ref[g], 0.0).astype(BF16)
            bias = bs_ref[:, g:g + 1]
            for c in range(half // ch):
                rows = slice(r0 + c * ch, r0 + (c + 1) * ch)
                sv = _dot(wg, vn_ref[rows, cols]) + bias
                gated_ref[rows, cols] = (u_ref[rows, cols].astype(F32) * sv).astype(BF16)
        o_ref[part, :] = x_ref[part, :] + _dot(gated_ref[part, :], wo_ref[...])


def _sgu_outproj(u_v, x, ln_g, ln_b, w_s_all, b_s_t, w_out_all, *, layer, tm):
    m, d = x.shape
    n_half = GMLP_WIDTH // d
    return pl.pallas_call(
        functools.partial(_sgu_kernel, tm=tm),
        grid=(m // tm,),
        in_specs=[
            pl.BlockSpec((tm, GMLP_WIDTH), lambda i: (i, 0)),
            pl.BlockSpec((tm, GMLP_WIDTH), lambda i: (i, n_half)),
            pl.BlockSpec((tm, d), lambda i: (i, 0)),
            pl.BlockSpec((1, GMLP_WIDTH), lambda i: (0, 0)),
            pl.BlockSpec((1, GMLP_WIDTH), lambda i: (0, 0)),
            pl.BlockSpec((None,) + w_s_all.shape[1:], lambda i: (layer, 0, 0, 0)),
            pl.BlockSpec(b_s_t.shape, lambda i: (0, 0)),
            pl.BlockSpec((None,) + w_out_all.shape[1:], lambda i: (layer, 0, 0)),
        ],
        out_specs=pl.BlockSpec((tm, d), lambda i: (i, 0)),
        out_shape=jax.ShapeDtypeStruct((m, d), F32),
        scratch_shapes=[pltpu.VMEM((tm, GMLP_WIDTH), BF16), pltpu.VMEM((tm, GMLP_WIDTH), BF16)],
        compiler_params=_cparams(("arbitrary",)),
        name="sgu_out_proj",
    )(u_v, u_v, x, ln_g, ln_b, w_s_all, b_s_t, w_out_all)


def _ffn_kernel(x_ref, g_ref, wg_ref, wu_ref, wd_ref, fg_ref, o_ref, h_ref, *, final):
    j = pl.program_id(1)

    def hidden_tile(rows):
        h = h_ref[rows, :]
        gate = _dot(h, wg_ref[...])
        up = _dot(h, wu_ref[...])
        act = gate * _sigmoid(gate) * up
        o_ref[rows, :] += _dot(act.astype(BF16), wd_ref[...].astype(BF16))

    @pl.when(j == 0)
    def _():
        for rows in _row_parts(x_ref.shape[0]):
            x = x_ref[rows, :]
            h_ref[rows, :] = _rms_norm_rows(x, g_ref[...], NORM_EPS).astype(BF16)
            o_ref[rows, :] = x
            hidden_tile(rows)

    last = pl.num_programs(1) - 1
    if final:
        @pl.when(jnp.logical_and(j > 0, j < last))
        def _():
            hidden_tile(slice(None))

        @pl.when(j == last)
        def _():
            for rows in _row_parts(x_ref.shape[0]):
                hidden_tile(rows)
                o_ref[rows, :] = _rms_norm_rows(o_ref[rows, :], fg_ref[...], NORM_EPS)
    else:
        @pl.when(j > 0)
        def _():
            hidden_tile(slice(None))


def _ffn(x, g, w_gate_all, w_up_all, w_down_all, final_g, *, layer, tm, th, final):
    m, d = x.shape
    hid = w_gate_all.shape[2]
    return pl.pallas_call(
        functools.partial(_ffn_kernel, final=final),
        grid=(m // tm, hid // th),
        in_specs=[
            pl.BlockSpec((tm, d), lambda i, j: (i, 0)),
            pl.BlockSpec((1, d), lambda i, j: (0, 0)),
            pl.BlockSpec((None, d, th), lambda i, j: (layer, 0, j)),
            pl.BlockSpec((None, d, th), lambda i, j: (layer, 0, j)),
            pl.BlockSpec((None, th, d), lambda i, j: (layer, j, 0)),
            pl.BlockSpec((1, d), lambda i, j: (0, 0)),
        ],
        out_specs=pl.BlockSpec((tm, d), lambda i, j: (i, 0)),
        out_shape=jax.ShapeDtypeStruct((m, d), F32),
        scratch_shapes=[pltpu.VMEM((tm, d), BF16)],
        compiler_params=_cparams(("arbitrary", "arbitrary")),
        name="swiglu_ffn",
    )(x, g, w_gate_all, w_up_all, w_down_all, final_g)


def _pad_cols(a, width):
    return jnp.pad(a, ((0, 0), (0, width - a.shape[1])))


def _pad_rows(a, height):
    return jnp.pad(a, ((0, height - a.shape[0]), (0, 0)))


def _rwkv_col_layout(a):
    c = RWKV_WIDTH
    xw = a[:, 3 * c:3 * c + DECAY_LORA]
    xa = a[:, 3 * c + DECAY_LORA:3 * c + DECAY_LORA + AAA_LORA]
    xg = a[:, 3 * c + DECAY_LORA + AAA_LORA:]
    return jnp.concatenate(
        [a[:, :3 * c], _pad_cols(xw, LANES), _pad_cols(xa, LANES), _pad_cols(xg, 2 * LANES)], axis=1)


def _rwkv_weight_layout(w_in):
    c = RWKV_WIDTH
    out = jnp.zeros((w_in.shape[0], RW_PAD_COLS), BF16)
    out = lax.dynamic_update_slice(out, w_in[:, :3 * c].astype(BF16), (0, 0))
    starts = (3 * c, 3 * c + DECAY_LORA, 3 * c + DECAY_LORA + AAA_LORA, RWKV_COLS)
    for src0, src1, dst in zip(starts[:-1], starts[1:], (RW_XW, RW_XA, RW_XG)):
        out = lax.dynamic_update_slice(out, w_in[:, src0:src1].astype(BF16), (0, dst))
    return out


def _rope_tables(seq):
    inv = ROPE_THETA ** (-jnp.arange(0, DIFF_HALF, 2, dtype=F32) / DIFF_HALF)
    ang = jnp.arange(seq, dtype=F32)[:, None] * inv[None, :]
    cos, sin = jnp.cos(ang), jnp.sin(ang)
    reps = LANES // DIFF_HALF
    return (jnp.tile(jnp.concatenate([cos, cos], axis=1), (1, reps)),
            jnp.tile(jnp.concatenate([-sin, sin], axis=1), (1, reps)))


def _tile_sizes(m, seq):
    return dict(
        proj_tm=min(1024, seq), proj_tn=1024, rwkv_proj_tn=RW_PAD_COLS // 2, gelu_proj_tn=2048,
        ffn_tm=min(1024, m), ffn_th=512,
        out_tm=min(512, m), sgu_tm=min(512, m),
        attn_sub=min(4, seq // ATTN_SUB),
    )


def kernel(x, mix_norm, ffn_norm, ffn_w_gate, ffn_w_up, ffn_w_down, ev_w_in, ev_mu, ev_w0, ev_w_dec_up, ev_a0,
           ev_w_a_up, ev_w_g_up, ev_k_k, ev_k_a, ev_r_k, ev_lnx_w, ev_lnx_b, ev_lam_q1, ev_lam_k1, ev_lam_q2,
           ev_lam_k2, ev_subln_g, ev_w_out, od_w_in, od_ln_g, od_ln_b, od_w_s, od_b_s, od_w_out, final_norm):
    batch, seq, d = x.shape
    m = batch * seq
    ts = _tile_sizes(m, seq)
    cos, sin = _rope_tables(seq)
    row = lambda a: a.reshape(1, -1).astype(F32)
    xf = x.reshape(m, d)
    ffn_w_gate_bf, ffn_w_up_bf, ffn_w_down_f32 = ffn_w_gate.astype(BF16), ffn_w_up.astype(BF16), ffn_w_down.astype(F32)
    ev_w_out_bf, od_w_in_bf, od_w_out_bf = ev_w_out.astype(BF16), od_w_in.astype(BF16), od_w_out.astype(BF16)
    od_w_s_f32 = od_w_s.astype(F32)
    for i in range(DEPTH):
        j = i // 2
        g_mix = row(mix_norm[i])
        if i % 2 == 0:
            lambda_init = 0.8 - 0.6 * math.exp(-0.3 * i)
            w_in = ev_w_in[j]
            w_rwkv = _rwkv_weight_layout(w_in)
            w_diff = w_in[:, RWKV_COLS:].astype(BF16)
            z = _norm_matmul(xf, g_mix, w_rwkv, tm=ts["proj_tm"], tn=ts["rwkv_proj_tn"], out_dtype=BF16)
            qkv = _norm_matmul_rope(xf, g_mix, w_diff, cos, sin, tm=ts["proj_tm"], tn=ts["proj_tn"], seq=seq)
            y_a = _rwkv_time_mix(
                z, _rwkv_col_layout(row(ev_mu[j])), row(ev_w0[j]),
                _pad_rows(ev_w_dec_up[j], LANES).astype(BF16), row(ev_a0[j]),
                _pad_rows(ev_w_a_up[j], LANES).astype(BF16), _pad_rows(ev_w_g_up[j], 2 * LANES).astype(BF16),
                row(ev_k_k[j]), row(ev_k_a[j]), row(ev_r_k[j]), row(ev_lnx_w[j]), row(ev_lnx_b[j]),
                batch=batch, seq=seq)
            lam_params = _pad_rows(_pad_cols(
                jnp.stack([ev_lam_q1[j], ev_lam_k1[j], ev_lam_q2[j], ev_lam_k2[j]]).astype(F32), LANES), 8)
            y_b = _diff_attention(qkv, lam_params, ev_subln_g[j].reshape(-1, 1).astype(F32), batch=batch, seq=seq,
                                  n_sub=ts["attn_sub"], lambda_init=lambda_init)
            xf = _outproj2(y_a, y_b, xf, ev_w_out_bf, layer=j, tm=ts["out_tm"])
        else:
            u_v = _norm_matmul(xf, g_mix, od_w_in_bf, tm=ts["proj_tm"], tn=ts["gelu_proj_tn"],
                               out_dtype=BF16, gelu=True, layer=j)
            xf = _sgu_outproj(u_v, xf, row(od_ln_g[j]), row(od_ln_b[j]), od_w_s_f32,
                              od_b_s[j].T.astype(F32), od_w_out_bf, layer=j, tm=ts["sgu_tm"])
        xf = _ffn(xf, row(ffn_norm[i]), ffn_w_gate_bf, ffn_w_up_bf, ffn_w_down_f32, row(final_norm),
                  layer=i, tm=ts["ffn_tm"], th=ts["ffn_th"], final=(i == DEPTH - 1))
    return xf.reshape(batch, seq, d)
```

```python
import functools
import math

import jax
import jax.numpy as jnp
from jax import lax
from jax.experimental import pallas as pl
from jax.experimental.pallas import tpu as pltpu

F32 = jnp.float32
BF16 = jnp.bfloat16

D_MODEL = 2048
DEPTH = 4
RWKV_WIDTH = D_MODEL // 2
RWKV_HEAD = 64
DECAY_LORA = 64
AAA_LORA = 64
GATE_LORA = 160
RWKV_COLS = 3 * RWKV_WIDTH + DECAY_LORA + AAA_LORA + GATE_LORA
RWKV_LNX_EPS = 64e-5
DIFF_WIDTH = D_MODEL - RWKV_WIDTH
DIFF_HEAD = 128
DIFF_HALF = DIFF_HEAD // 2
DIFF_HEADS = DIFF_WIDTH // DIFF_HEAD
ROPE_THETA = 10000.0
NEG_INF = -1e30
GMLP_WIDTH = D_MODEL
GMLP_CHUNK = 128
GMLP_GROUPS = 16
NORM_EPS = 1e-6
SUBLN_EPS = 1e-5
LN_EPS = 1e-5

LANES = 128
MXU_COLS = 256
PROLOGUE_PARTS = 4
RESIDENT = pl.Buffered(1)
VMEM_LIMIT_BYTES = 60 * 1024 * 1024

RW_XW = 3 * RWKV_WIDTH
RW_XA = RW_XW + LANES
RW_XG = RW_XA + LANES
RW_PAD_COLS = RW_XG + 2 * LANES

RWKV_CHUNK = 64
RWKV_CHUNKS_PER_STEP = 8
RWKV_CHUNKS_PER_TRIP = 2
ATTN_SUB = 128
ATTN_KEYS = 256
ATTN_ONES_ROWS = 16
ATTN_TRIP = 8
LOG2_E = 1.4426950408889634
PAIR = 2 * RWKV_HEAD


def _cparams(semantics):
    return pltpu.CompilerParams(dimension_semantics=semantics, vmem_limit_bytes=VMEM_LIMIT_BYTES)


def _dot(a, b):
    return jnp.dot(a, b, preferred_element_type=F32)


def _dot_nt(a, b):
    return lax.dot_general(a, b, (((1,), (1,)), ((), ())), preferred_element_type=F32)


def _rms_norm_rows(x, g, eps):
    ms = jnp.mean(x * x, axis=-1, keepdims=True)
    return x * lax.rsqrt(ms + eps) * g


def _row_parts(tm):
    part = tm // PROLOGUE_PARTS
    return [slice(r0, r0 + part) for r0 in range(0, tm, part)]


def _norm_matmul_kernel(x_ref, g_ref, w_ref, o_ref, h_ref, *, tn, gelu):
    j = pl.program_id(1)

    def columns(rows):
        for n0 in range(0, tn, MXU_COLS):
            acc = _dot(h_ref[rows, :], w_ref[:, n0:n0 + MXU_COLS])
            if gelu:
                acc = 0.5 * acc * (1.0 + lax.erf(acc * math.sqrt(0.5)))
            o_ref[rows, n0:n0 + MXU_COLS] = acc.astype(o_ref.dtype)

    @pl.when(j == 0)
    def _():
        for rows in _row_parts(x_ref.shape[0]):
            h_ref[rows, :] = _rms_norm_rows(x_ref[rows, :], g_ref[...], NORM_EPS).astype(BF16)
            columns(rows)

    @pl.when(j > 0)
    def _():
        columns(slice(None))


def _norm_matmul(x, g, w, *, tm, tn, out_dtype, gelu=False, layer=None):
    m, k = x.shape
    n = w.shape[-1]
    if layer is None:
        w_spec = pl.BlockSpec((k, tn), lambda i, j: (0, j))
    else:
        w_spec = pl.BlockSpec((None, k, tn), lambda i, j: (layer, 0, j))
    return pl.pallas_call(
        functools.partial(_norm_matmul_kernel, tn=tn, gelu=gelu),
        grid=(m // tm, n // tn),
        in_specs=[
            pl.BlockSpec((tm, k), lambda i, j: (i, 0)),
            pl.BlockSpec((1, k), lambda i, j: (0, 0)),
            w_spec,
        ],
        out_specs=pl.BlockSpec((tm, tn), lambda i, j: (i, j)),
        out_shape=jax.ShapeDtypeStruct((m, n), out_dtype),
        scratch_shapes=[pltpu.VMEM((tm, k), BF16)],
        compiler_params=_cparams(("arbitrary", "arbitrary")),
        name="norm_matmul_gelu" if gelu else "norm_matmul",
    )(x, g, w)


def _norm_matmul_rope_kernel(x_ref, g_ref, w_ref, cos_ref, sin_ref, o_ref, h_ref, *, tn, q_tiles, rope_tiles):
    j = pl.program_id(1)

    def columns(rows, epilogue):
        for n0 in range(0, tn, MXU_COLS):
            acc = _dot(h_ref[rows, :], w_ref[:, n0:n0 + MXU_COLS])
            o_ref[rows, n0:n0 + MXU_COLS] = epilogue(rows, acc).astype(o_ref.dtype)

    def rope(rows, acc):
        reps = MXU_COLS // LANES
        c = jnp.concatenate([cos_ref[rows, :]] * reps, axis=1)
        s = jnp.concatenate([sin_ref[rows, :]] * reps, axis=1)
        lane = lax.broadcasted_iota(jnp.int32, acc.shape, 1)
        partner = jnp.where((lane & (DIFF_HALF // 2)) == 0,
                            pltpu.roll(acc, MXU_COLS - DIFF_HALF // 2, 1),
                            pltpu.roll(acc, DIFF_HALF // 2, 1))
        scale = jnp.where(j < q_tiles, DIFF_HALF ** -0.5 * LOG2_E, 1.0).astype(F32)
        return (acc * c + partner * s) * scale

    @pl.when(j == 0)
    def _():
        for rows in _row_parts(x_ref.shape[0]):
            h_ref[rows, :] = _rms_norm_rows(x_ref[rows, :], g_ref[...], NORM_EPS).astype(BF16)
            columns(rows, rope)

    @pl.when(jnp.logical_and(j > 0, j < rope_tiles))
    def _():
        columns(slice(None), rope)

    @pl.when(j >= rope_tiles)
    def _():
        columns(slice(None), lambda rows, acc: acc)


def _norm_matmul_rope(x, g, w, cos, sin, *, tm, tn, seq):
    m, k = x.shape
    n = w.shape[1]
    q_tiles = DIFF_WIDTH // tn
    t_tiles = seq // tm
    return pl.pallas_call(
        functools.partial(_norm_matmul_rope_kernel, tn=tn, q_tiles=q_tiles, rope_tiles=2 * q_tiles),
        grid=(m // tm, n // tn),
        in_specs=[
            pl.BlockSpec((tm, k), lambda i, j: (i, 0)),
            pl.BlockSpec((1, k), lambda i, j: (0, 0)),
            pl.BlockSpec((k, tn), lambda i, j: (0, j)),
            pl.BlockSpec((tm, LANES), lambda i, j: (i % t_tiles, 0)),
            pl.BlockSpec((tm, LANES), lambda i, j: (i % t_tiles, 0)),
        ],
        out_specs=pl.BlockSpec((tm, tn), lambda i, j: (i, j)),
        out_shape=jax.ShapeDtypeStruct((m, n), BF16),
        scratch_shapes=[pltpu.VMEM((tm, k), BF16)],
        compiler_params=_cparams(("arbitrary", "arbitrary")),
        name="norm_matmul_rope",
    )(x, g, w, cos, sin)


def _sigmoid(y):
    return 1.0 / (1.0 + jnp.exp(-y))


def _split_bf16(x, parts):
    out = []
    rem = x
    for _ in range(parts):
        p = rem.astype(BF16)
        out.append(p)
        rem = rem - p.astype(F32)
    return out


def _rwkv_kernel(z_ref, mu_ref, w0_ref, wdec_ref, a0_ref, wa_ref, wg_ref, kk_ref, ka_ref, rk_ref,
                 lw_ref, lb_ref, o_ref, s_ref, prev_ref):
    L = RWKV_CHUNK
    C = RWKV_WIDTH
    n_pairs = C // PAIR
    n_batch = z_ref.shape[0]

    @pl.when(pl.program_id(0) == 0)
    def _():
        s_ref[...] = jnp.zeros_like(s_ref)
        prev_ref[...] = jnp.zeros_like(prev_ref)

    ti = lax.broadcasted_iota(jnp.int32, (L, L), 0)
    tj = lax.broadcasted_iota(jnp.int32, (L, L), 1)
    tril_ones = jnp.where(ti >= tj, 1.0, 0.0).astype(BF16)

    def prepare(bi, row0):
        z = z_ref[bi, pl.ds(row0, L), :].astype(F32)
        row = lax.broadcasted_iota(jnp.int32, z.shape, 0)
        z_prev = jnp.where(row == 0, prev_ref[bi, 0:1, :], pltpu.roll(z, 1, 0))
        prev_ref[bi, 0:1, :] = z[L - 1:L, :]
        zs = z + (z_prev - z) * mu_ref[...]

        r = zs[:, 0:C]
        k = zs[:, C:2 * C]
        v = zs[:, 2 * C:3 * C]
        xw = zs[:, RW_XW:RW_XA]
        xa = zs[:, RW_XA:RW_XG]
        xg = zs[:, RW_XG:RW_PAD_COLS]

        y_dec = w0_ref[...] + _dot(jnp.tanh(xw).astype(BF16), wdec_ref[...])
        logd = -math.exp(-0.5) * _sigmoid(y_dec)
        a_lr = _sigmoid(a0_ref[...] + _dot(xa.astype(BF16), wa_ref[...]))
        gate = _dot(_sigmoid(xg).astype(BF16), wg_ref[...])
        kk_raw = k * kk_ref[...]
        k2 = k * (1.0 + (a_lr - 1.0) * ka_ref[...])
        rkk = r * k2 * rk_ref[...]
        cum = sum(_dot(tril_ones, p) for p in _split_bf16(logd, 2))
        return r, k2, v, logd, a_lr, gate, kk_raw, rkk, cum

    rr = lax.broadcasted_iota(jnp.int32, (PAIR, PAIR), 0)
    cc = lax.broadcasted_iota(jnp.int32, (PAIR, PAIR), 1)
    same_head = (rr // RWKV_HEAD) == (cc // RWKV_HEAD)
    strict = jnp.logical_and(same_head, rr > cc)
    incl = jnp.logical_and(same_head, rr >= cc)
    head_ones = jnp.where(same_head, 1.0, 0.0).astype(BF16)
    eye = jnp.where(rr == cc, 1.0, 0.0).astype(F32)
    first_head = lax.broadcasted_iota(jnp.int32, (L, PAIR), 1) < RWKV_HEAD

    def head_sum(x):
        return _dot(x.astype(BF16), head_ones)

    def stack(x):
        return jnp.concatenate([jnp.where(first_head, x, 0.0), jnp.where(first_head, 0.0, x)], axis=0)

    n_steps = int(math.log2(L)) - 1
    idx = range(n_pairs)
    sls = [slice(p * PAIR, (p + 1) * PAIR) for p in idx]

    def elementwise(prep, p):
        r, k2, v, logd, a_lr, _, kk_raw, rkk, cum = prep
        sl = sls[p]
        kk_p = kk_raw[:, sl]
        kk_n = kk_p * lax.rsqrt(jnp.maximum(head_sum(kk_p * kk_p), 1e-24))
        bonus = head_sum(rkk[:, sl])
        a_p = -kk_n
        b_p = kk_n * a_lr[:, sl]
        r_p, k_p = r[:, sl], k2[:, sl]
        cu = cum[:, sl]
        c_end = cu[L - 1:L, :]
        e_pos = jnp.exp(cu)
        e_neg = jnp.exp(-cu)
        e_prev = jnp.exp(cu - logd[:, sl])
        e_end = jnp.exp(c_end - cu)
        lhs = jnp.concatenate([stack(a_p * e_prev), stack(r_p * e_pos)], axis=0).astype(BF16)
        rhs = jnp.concatenate([stack(b_p * e_neg), stack(k_p * e_neg)], axis=0).astype(BF16)
        bk_end = jnp.concatenate([stack(b_p * e_end), stack(k_p * e_end)], axis=0).astype(BF16)
        return lhs, rhs, bk_end, stack(v[:, sl]), jnp.exp(c_end), bonus

    def stages(bi, row0, prep, elems, filler):
        v, gate = prep[2], prep[5]
        lhs, rhs, bk_end, v_st, d_end, bonus_s = zip(*elems)
        pm = [_dot_nt(lhs[i], rhs[i]) for i in idx]
        filler()
        s_old = [s_ref[bi, p] for p in idx]
        sh = [_dot_nt(lhs[i], s_old[i].astype(BF16)) for i in idx]
        filler()
        p_ab = [jnp.where(strict, pm[i][:PAIR, :PAIR], 0.0) for i in idx]
        p_ak = [jnp.where(strict, pm[i][:PAIR, PAIR:], 0.0).astype(BF16) for i in idx]
        p_rbk = [jnp.concatenate([jnp.where(incl, pm[i][PAIR:, :PAIR], 0.0),
                                  jnp.where(incl, pm[i][PAIR:, PAIR:], 0.0)], axis=1).astype(BF16) for i in idx]
        akv = [_dot(p_ak[i], v_st[i].astype(BF16)) for i in idx]
        filler()

        t_inv = [eye + p_ab[i] for i in idx]
        xb = [p_ab[i].astype(BF16) for i in idx]
        xb = [_dot(xb[i], xb[i]).astype(BF16) for i in idx]
        filler()
        for it in range(n_steps):
            if it + 1 < n_steps:
                both = [_dot(xb[i], jnp.concatenate([xb[i], t_inv[i].astype(BF16)], axis=1)) for i in idx]
                xb = [both[i][:, :PAIR].astype(BF16) for i in idx]
                t_inv = [t_inv[i] + both[i][:, PAIR:] for i in idx]
            else:
                t_inv = [t_inv[i] + _dot(xb[i], t_inv[i].astype(BF16)) for i in idx]
            filler()

        u_st = [_dot(t_inv[i].astype(BF16), (sh[i][:PAIR] + akv[i]).astype(BF16)) for i in idx]
        filler()
        uv = [jnp.concatenate([u_st[i], v_st[i]], axis=0) for i in idx]
        y_st = [sh[i][PAIR:] + _dot(p_rbk[i], uv[i].astype(BF16)) for i in idx]
        filler()
        s_upd = [_dot(uv[i].T.astype(BF16), bk_end[i]) for i in idx]
        for p in idx:
            s_ref[bi, p] = s_old[p] * d_end[p] + s_upd[p]

        y = [y_st[i][:L] + y_st[i][L:] for i in idx]
        mean = [head_sum(y[i]) * (1.0 / RWKV_HEAD) for i in idx]
        yc = [y[i] - mean[i] for i in idx]
        var = [head_sum(yc[i] * yc[i]) * (1.0 / RWKV_HEAD) for i in idx]
        for i in idx:
            sl = sls[i]
            yn = yc[i] * lax.rsqrt(var[i] + RWKV_LNX_EPS) * lw_ref[:, sl] + lb_ref[:, sl]
            o_ref[bi, pl.ds(row0, L), sl] = ((yn + bonus_s[i] * v[:, sl]) * gate[:, sl]).astype(o_ref.dtype)

    def trip(ti, carry):
        rows = [pl.multiple_of((ti * RWKV_CHUNKS_PER_TRIP + k) * L, L) for k in range(RWKV_CHUNKS_PER_TRIP)]
        units = [(k, bi) for k in range(RWKV_CHUNKS_PER_TRIP) for bi in range(n_batch)]
        prepared, elems = {}, {u: [] for u in units}

        def work_before(u):
            k = u[0]
            todo = [functools.partial(lambda v: prepared.__setitem__(v, prepare(v[1], rows[v[0]])), v)
                    for v in units if v[0] == k and v not in prepared and v not in scheduled]
            scheduled.update(v for v in units if v[0] == k)
            todo += [functools.partial(lambda v, p: elems[v].append(elementwise(prepared[v], p)), u, p) for p in idx]
            return todo

        scheduled = set()
        for thunk in work_before(units[0]):
            thunk()
        for n, u in enumerate(units):
            queue = work_before(units[n + 1]) if n + 1 < len(units) else []

            def filler():
                if queue:
                    queue.pop(0)()

            stages(u[1], rows[u[0]], prepared[u], elems[u], filler)
            while queue:
                queue.pop(0)()
        return carry

    lax.fori_loop(0, z_ref.shape[1] // (L * RWKV_CHUNKS_PER_TRIP), trip, 0)


def _rwkv_time_mix(z, mu, w0, wdec, a0, wa, wg, k_k, k_a, r_k, lnx_w, lnx_b, *, batch, seq):
    rows = min(RWKV_CHUNK * RWKV_CHUNKS_PER_STEP, seq)
    n_pairs = RWKV_WIDTH // PAIR
    full = lambda shape: pl.BlockSpec(shape, lambda c: (0,) * len(shape))
    out = pl.pallas_call(
        _rwkv_kernel,
        grid=(seq // rows,),
        in_specs=[
            pl.BlockSpec((batch, rows, RW_PAD_COLS), lambda c: (0, c, 0)),
            full((1, RW_PAD_COLS)),
            full((1, RWKV_WIDTH)), full((LANES, RWKV_WIDTH)),
            full((1, RWKV_WIDTH)), full((LANES, RWKV_WIDTH)),
            full((2 * LANES, RWKV_WIDTH)),
            full((1, RWKV_WIDTH)), full((1, RWKV_WIDTH)), full((1, RWKV_WIDTH)),
            full((1, RWKV_WIDTH)), full((1, RWKV_WIDTH)),
        ],
        out_specs=pl.BlockSpec((batch, rows, RWKV_WIDTH), lambda c: (0, c, 0)),
        out_shape=jax.ShapeDtypeStruct((batch, seq, RWKV_WIDTH), BF16),
        scratch_shapes=[pltpu.VMEM((batch, n_pairs, PAIR, PAIR), F32), pltpu.VMEM((batch, 8, RW_PAD_COLS), F32)],
        compiler_params=_cparams(("arbitrary",)),
        name="rwkv7_time_mix",
    )(z.reshape(batch, seq, RW_PAD_COLS), mu, w0, wdec, a0, wa, wg, k_k, k_a, r_k, lnx_w, lnx_b)
    return out.reshape(batch * seq, RWKV_WIDTH)


def _attn_kernel(lam_ref, g_ref, q_ref, k_ref, v_ref, o_ref, vt_ref, acc_ref, qq_ref, s_ref, smax_ref, *, seq,
                 n_sub, lambda_init):
    sb = ATTN_SUB
    two = 2 * sb
    kb = ATTN_KEYS
    qb = n_sub * sb
    steps_per_q = qb // kb
    nq = seq // qb

    def transpose_block(t):
        start = pl.multiple_of(t * kb, kb)
        vt_ref[t, :DIFF_HEAD, :] = v_ref[pl.ds(start, kb), :].astype(F32).T.astype(BF16)
        vt_ref[t, DIFF_HEAD:, :] = jnp.ones((ATTN_ONES_ROWS, kb), BF16)

    lane = lax.broadcasted_iota(jnp.int32, (sb, DIFF_HEAD), 1)
    key_idx = lax.broadcasted_iota(jnp.int32, (kb, two), 0)
    qry_idx = lax.broadcasted_iota(jnp.int32, (kb, two), 1) & (sb - 1)
    lam_p = lam_ref[...]
    lam = (jnp.exp(jnp.sum(lam_p[0:1] * lam_p[1:2], axis=-1, keepdims=True))
           - jnp.exp(jnp.sum(lam_p[2:3] * lam_p[3:4], axis=-1, keepdims=True)) + lambda_init)

    def scores(qslot, c, j):
        start = pl.multiple_of(j * kb, kb)
        return _dot_nt(k_ref[pl.ds(start, kb), :], qq_ref[qslot, c])

    def softmax_pv(c, j, st, st_max, m):
        m_new = jnp.maximum(m, st_max)
        alpha = jnp.exp2(m - m_new)
        pr = jnp.exp2(st - m_new)
        acc_ref[c] = alpha * acc_ref[c] + _dot(vt_ref[j], pr.astype(BF16))
        return m_new

    def stage_scores(qslot, slot, c, j):
        st = scores(qslot, c, j)
        s_ref[slot, c] = st
        smax_ref[slot, c] = jnp.max(st, axis=0, keepdims=True)

    def open_block(i):
        qslot = i % 2
        for c in range(n_sub):
            row = pl.multiple_of(i * qb + c * sb, sb)
            qc = q_ref[pl.ds(row, sb), :]
            zero = jnp.zeros_like(qc)
            qq_ref[qslot, c] = jnp.concatenate([jnp.where(lane < DIFF_HALF, qc, zero),
                                                jnp.where(lane < DIFF_HALF, zero, qc)], axis=0)
        for c in range(n_sub):
            stage_scores(qslot, 0, c, 0)

    def q_block(i, carry):
        qslot = i % 2
        for c in range(n_sub):
            acc_ref[c] = jnp.zeros((DIFF_HEAD + ATTN_ONES_ROWS, two), F32)

        def full_steps(j0, n_steps, ms):
            ms = list(ms)
            for h in range(n_steps):
                for c in range(n_sub):
                    stage_scores(qslot, 1 - h % 2, c, j0 + h + 1)
                for c in range(n_sub):
                    ms[c] = softmax_pv(c, j0 + h, s_ref[h % 2, c], smax_ref[h % 2, c], ms[c])
            return tuple(ms)

        init = tuple(jnp.full((1, two), NEG_INF, F32) for _ in range(n_sub))
        n_full = i * steps_per_q
        n_short = (n_full % ATTN_TRIP) // 2
        ms = lax.fori_loop(0, n_short, lambda t, cr: full_steps(2 * t, 2, cr), init)
        ms = list(lax.fori_loop(0, n_full // ATTN_TRIP,
                                lambda t, cr: full_steps(2 * n_short + ATTN_TRIP * t, ATTN_TRIP, cr), ms))

        first_diag = n_full
        for s in range(steps_per_q):
            transpose_block(first_diag + s)
        needed = lambda s, c: s * kb < (c + 1) * sb
        tail = {(s, c): scores(qslot, c, first_diag + s)
                for s in range(1, steps_per_q) for c in range(n_sub) if needed(s, c)}
        for s in range(steps_per_q):
            for c in range(n_sub):
                if not needed(s, c):
                    continue
                st = s_ref[0, c] if s == 0 else tail[(s, c)]
                if (s + 1) * kb - 1 > c * sb:
                    st = jnp.where(key_idx + s * kb <= qry_idx + c * sb, st, NEG_INF)
                    st_max = jnp.max(st, axis=0, keepdims=True)
                else:
                    st_max = smax_ref[0, c] if s == 0 else jnp.max(st, axis=0, keepdims=True)
                ms[c] = softmax_pv(c, first_diag + s, st, st_max, ms[c])

        open_block(jnp.minimum(i + 1, nq - 1))

        for c in range(n_sub):
            acc = acc_ref[c]
            o = acc[:DIFF_HEAD] / acc[DIFF_HEAD:DIFF_HEAD + 1]
            d = o[:, :sb] - lam * o[:, sb:]
            ms_d = jnp.mean(d * d, axis=0, keepdims=True)
            d = d * lax.rsqrt(ms_d + SUBLN_EPS) * g_ref[...] * (1.0 - lambda_init)
            row = pl.multiple_of(i * qb + c * sb, sb)
            o_ref[pl.ds(row, sb), :] = d.T.astype(o_ref.dtype)
        return carry

    open_block(0)
    lax.fori_loop(0, nq, q_block, 0)


def _diff_attention(qkv, lam_params, subln_g_col, *, batch, seq, n_sub, lambda_init):
    qb = n_sub * ATTN_SUB
    assert qb % (2 * ATTN_KEYS) == 0 and seq % qb == 0
    h = DIFF_HEADS
    head_cols = lambda off: pl.BlockSpec((seq, DIFF_HEAD), lambda b, hh: (b, off + hh))
    return pl.pallas_call(
        functools.partial(_attn_kernel, seq=seq, n_sub=n_sub, lambda_init=lambda_init),
        grid=(batch, h),
        in_specs=[
            pl.BlockSpec((8, LANES), lambda b, hh: (0, 0)),
            pl.BlockSpec((DIFF_HEAD, 1), lambda b, hh: (0, 0)),
            head_cols(0), head_cols(h), head_cols(2 * h),
        ],
        out_specs=head_cols(0),
        out_shape=jax.ShapeDtypeStruct((batch * seq, DIFF_WIDTH), BF16),
        scratch_shapes=[pltpu.VMEM((seq // ATTN_KEYS, DIFF_HEAD + ATTN_ONES_ROWS, ATTN_KEYS), BF16),
                        pltpu.VMEM((n_sub, DIFF_HEAD + ATTN_ONES_ROWS, 2 * ATTN_SUB), F32),
                        pltpu.VMEM((2, n_sub, 2 * ATTN_SUB, DIFF_HEAD), BF16),
                        pltpu.VMEM((2, n_sub, ATTN_KEYS, 2 * ATTN_SUB), F32),
                        pltpu.VMEM((2, n_sub, 1, 2 * ATTN_SUB), F32)],
        compiler_params=_cparams(("arbitrary", "arbitrary")),
        name="diff_attention",
    )(lam_params, subln_g_col, qkv, qkv, qkv)


def _outproj2_kernel(ya_ref, yb_ref, x_ref, wa_ref, wb_ref, o_ref):
    o_ref[...] = x_ref[...] + _dot(ya_ref[...], wa_ref[...]) + _dot(yb_ref[...], wb_ref[...])


def _outproj2(ya, yb, x, w_all, *, layer, tm):
    m, d = x.shape
    half = w_all.shape[1] // 2
    return pl.pallas_call(
        _outproj2_kernel,
        grid=(m // tm,),
        in_specs=[
            pl.BlockSpec((tm, ya.shape[1]), lambda i: (i, 0)),
            pl.BlockSpec((tm, yb.shape[1]), lambda i: (i, 0)),
            pl.BlockSpec((tm, d), lambda i: (i, 0)),
            pl.BlockSpec((None, half, d), lambda i: (layer, 0, 0), pipeline_mode=RESIDENT),
            pl.BlockSpec((None, half, d), lambda i: (layer, 1, 0), pipeline_mode=RESIDENT),
        ],
        out_specs=pl.BlockSpec((tm, d), lambda i: (i, 0)),
        out_shape=jax.ShapeDtypeStruct((m, d), F32),
        compiler_params=_cparams(("arbitrary",)),
        name="even_out_proj",
    )(ya, yb, x, w_all, w_all)


def _sgu_kernel(u_ref, v_ref, x_ref, lng_ref, lnb_ref, ws_ref, bs_ref, wo_ref, o_ref, vn_ref, gated_ref, *, tm):
    ch = GMLP_CHUNK
    ti = lax.broadcasted_iota(jnp.int32, (ch, ch), 0)
    tj = lax.broadcasted_iota(jnp.int32, (ch, ch), 1)
    causal = ti >= tj
    half = tm // 2
    for r0 in range(0, tm, half):
        part = slice(r0, r0 + half)
        v = v_ref[part, :].astype(F32)
        mu = jnp.mean(v, axis=-1, keepdims=True)
        vc = v - mu
        var = jnp.mean(vc * vc, axis=-1, keepdims=True)
        vn_ref[part, :] = (vc * lax.rsqrt(var + LN_EPS) * lng_ref[...] + lnb_ref[...]).astype(BF16)
        for g in range(GMLP_GROUPS):
            cols = slice(g * LANES, (g + 1) * LANES)
            wg = jnp.where(causal, ws_ref[g], 0.0).astype(BF16)
            bias = bs_ref[:, g:g + 1]
            for c in range(half // ch):
                rows = slice(r0 + c * ch, r0 + (c + 1) * ch)
                sv = _dot(wg, vn_ref[rows, cols]) + bias
                gated_ref[rows, cols] = (u_ref[rows, cols].astype(F32) * sv).astype(BF16)
        o_ref[part, :] = x_ref[part, :] + _dot(gated_ref[part, :], wo_ref[...])


def _sgu_outproj(u_v, x, ln_g, ln_b, w_s_all, b_s_t, w_out_all, *, layer, tm):
    m, d = x.shape
    n_half = GMLP_WIDTH // d
    return pl.pallas_call(
        functools.partial(_sgu_kernel, tm=tm),
        grid=(m // tm,),
        in_specs=[
            pl.BlockSpec((tm, GMLP_WIDTH), lambda i: (i, 0)),
            pl.BlockSpec((tm, GMLP_WIDTH), lambda i: (i, n_half)),
            pl.BlockSpec((tm, d), lambda i: (i, 0)),
            pl.BlockSpec((1, GMLP_WIDTH), lambda i: (0, 0)),
            pl.BlockSpec((1, GMLP_WIDTH), lambda i: (0, 0)),
            pl.BlockSpec((None,) + w_s_all.shape[1:], lambda i: (layer, 0, 0, 0)),
            pl.BlockSpec(b_s_t.shape, lambda i: (0, 0)),
            pl.BlockSpec((None,) + w_out_all.shape[1:], lambda i: (layer, 0, 0)),
        ],
        out_specs=pl.BlockSpec((tm, d), lambda i: (i, 0)),
        out_shape=jax.ShapeDtypeStruct((m, d), F32),
        scratch_shapes=[pltpu.VMEM((tm, GMLP_WIDTH), BF16), pltpu.VMEM((tm, GMLP_WIDTH), BF16)],
        compiler_params=_cparams(("arbitrary",)),
        name="sgu_out_proj",
    )(u_v, u_v, x, ln_g, ln_b, w_s_all, b_s_t, w_out_all)


def _ffn_kernel(x_ref, g_ref, wg_ref, wu_ref, wd_ref, fg_ref, o_ref, h_ref, *, final):
    j = pl.program_id(1)

    def hidden_tile(rows):
        h = h_ref[rows, :]
        gate = _dot(h, wg_ref[...])
        up = _dot(h, wu_ref[...])
        act = gate * _sigmoid(gate) * up
        o_ref[rows, :] += _dot(act.astype(BF16), wd_ref[...].astype(BF16))

    @pl.when(j == 0)
    def _():
        for rows in _row_parts(x_ref.shape[0]):
            x = x_ref[rows, :]
            h_ref[rows, :] = _rms_norm_rows(x, g_ref[...], NORM_EPS).astype(BF16)
            o_ref[rows, :] = x
            hidden_tile(rows)

    last = pl.num_programs(1) - 1
    if final:
        @pl.when(jnp.logical_and(j > 0, j < last))
        def _():
            hidden_tile(slice(None))

        @pl.when(j == last)
        def _():
            for rows in _row_parts(x_ref.shape[0]):
                hidden_tile(rows)
                o_ref[rows, :] = _rms_norm_rows(o_ref[rows, :], fg_ref[...], NORM_EPS)
    else:
        @pl.when(j > 0)
        def _():
            hidden_tile(slice(None))


def _ffn(x, g, w_gate_all, w_up_all, w_down_all, final_g, *, layer, tm, th, final):
    m, d = x.shape
    hid = w_gate_all.shape[2]
    return pl.pallas_call(
        functools.partial(_ffn_kernel, final=final),
        grid=(m // tm, hid // th),
        in_specs=[
            pl.BlockSpec((tm, d), lambda i, j: (i, 0)),
            pl.BlockSpec((1, d), lambda i, j: (0, 0)),
            pl.BlockSpec((None, d, th), lambda i, j: (layer, 0, j)),
            pl.BlockSpec((None, d, th), lambda i, j: (layer, 0, j)),
            pl.BlockSpec((None, th, d), lambda i, j: (layer, j, 0)),
            pl.BlockSpec((1, d), lambda i, j: (0, 0)),
        ],
        out_specs=pl.BlockSpec((tm, d), lambda i, j: (i, 0)),
        out_shape=jax.ShapeDtypeStruct((m, d), F32),
        scratch_shapes=[pltpu.VMEM((tm, d), BF16)],
        compiler_params=_cparams(("arbitrary", "arbitrary")),
        name="swiglu_ffn",
    )(x, g, w_gate_all, w_up_all, w_down_all, final_g)


def _pad_cols(a, width):
    return jnp.pad(a, ((0, 0), (0, width - a.shape[1])))


def _pad_rows(a, height):
    return jnp.pad(a, ((0, height - a.shape[0]), (0, 0)))


def _rwkv_col_layout(a):
    c = RWKV_WIDTH
    xw = a[:, 3 * c:3 * c + DECAY_LORA]
    xa = a[:, 3 * c + DECAY_LORA:3 * c + DECAY_LORA + AAA_LORA]
    xg = a[:, 3 * c + DECAY_LORA + AAA_LORA:]
    return jnp.concatenate(
        [a[:, :3 * c], _pad_cols(xw, LANES), _pad_cols(xa, LANES), _pad_cols(xg, 2 * LANES)], axis=1)


def _rwkv_weight_layout(w_in):
    c = RWKV_WIDTH
    out = jnp.zeros((w_in.shape[0], RW_PAD_COLS), BF16)
    out = lax.dynamic_update_slice(out, w_in[:, :3 * c].astype(BF16), (0, 0))
    starts = (3 * c, 3 * c + DECAY_LORA, 3 * c + DECAY_LORA + AAA_LORA, RWKV_COLS)
    for src0, src1, dst in zip(starts[:-1], starts[1:], (RW_XW, RW_XA, RW_XG)):
        out = lax.dynamic_update_slice(out, w_in[:, src0:src1].astype(BF16), (0, dst))
    return out


def _rope_tables(seq):
    inv = ROPE_THETA ** (-jnp.arange(0, DIFF_HALF, 2, dtype=F32) / DIFF_HALF)
    ang = jnp.arange(seq, dtype=F32)[:, None] * inv[None, :]
    cos, sin = jnp.cos(ang), jnp.sin(ang)
    reps = LANES // DIFF_HALF
    return (jnp.tile(jnp.concatenate([cos, cos], axis=1), (1, reps)),
            jnp.tile(jnp.concatenate([-sin, sin], axis=1), (1, reps)))


def _tile_sizes(m, seq):
    return dict(
        proj_tm=min(1024, seq), proj_tn=1024, rwkv_proj_tn=RW_PAD_COLS // 2, gelu_proj_tn=2048,
        ffn_tm=min(1024, m), ffn_th=512,
        out_tm=min(1024, m), sgu_tm=min(512, m),
        attn_sub=min(4, seq // ATTN_SUB),
    )


def kernel(x, mix_norm, ffn_norm, ffn_w_gate, ffn_w_up, ffn_w_down, ev_w_in, ev_mu, ev_w0, ev_w_dec_up, ev_a0,
           ev_w_a_up, ev_w_g_up, ev_k_k, ev_k_a, ev_r_k, ev_lnx_w, ev_lnx_b, ev_lam_q1, ev_lam_k1, ev_lam_q2,
           ev_lam_k2, ev_subln_g, ev_w_out, od_w_in, od_ln_g, od_ln_b, od_w_s, od_b_s, od_w_out, final_norm):
    batch, seq, d = x.shape
    m = batch * seq
    ts = _tile_sizes(m, seq)
    cos, sin = _rope_tables(seq)
    row = lambda a: a.reshape(1, -1).astype(F32)
    xf = x.reshape(m, d)
    ffn_w_gate_bf, ffn_w_up_bf, ffn_w_down_f32 = ffn_w_gate.astype(BF16), ffn_w_up.astype(BF16), ffn_w_down.astype(F32)
    ev_w_out_bf, od_w_in_bf, od_w_out_bf = ev_w_out.astype(BF16), od_w_in.astype(BF16), od_w_out.astype(BF16)
    od_w_s_f32 = od_w_s.astype(F32)
    for i in range(DEPTH):
        j = i // 2
        g_mix = row(mix_norm[i])
        if i % 2 == 0:
            lambda_init = 0.8 - 0.6 * math.exp(-0.3 * i)
            w_in = ev_w_in[j]
            w_rwkv = _rwkv_weight_layout(w_in)
            w_diff = w_in[:, RWKV_COLS:].astype(BF16)
            z = _norm_matmul(xf, g_mix, w_rwkv, tm=ts["proj_tm"], tn=ts["rwkv_proj_tn"], out_dtype=BF16)
            qkv = _norm_matmul_rope(xf, g_mix, w_diff, cos, sin, tm=ts["proj_tm"], tn=ts["proj_tn"], seq=seq)
            y_a = _rwkv_time_mix(
                z, _rwkv_col_layout(row(ev_mu[j])), row(ev_w0[j]),
                _pad_rows(ev_w_dec_up[j], LANES).astype(BF16), row(ev_a0[j]),
                _pad_rows(ev_w_a_up[j], LANES).astype(BF16), _pad_rows(ev_w_g_up[j], 2 * LANES).astype(BF16),
                row(ev_k_k[j]), row(ev_k_a[j]), row(ev_r_k[j]), row(ev_lnx_w[j]), row(ev_lnx_b[j]),
                batch=batch, seq=seq)
            lam_params = _pad_rows(_pad_cols(
                jnp.stack([ev_lam_q1[j], ev_lam_k1[j], ev_lam_q2[j], ev_lam_k2[j]]).astype(F32), LANES), 8)
            y_b = _diff_attention(qkv, lam_params, ev_subln_g[j].reshape(-1, 1).astype(F32), batch=batch, seq=seq,
                                  n_sub=ts["attn_sub"], lambda_init=lambda_init)
            xf = _outproj2(y_a, y_b, xf, ev_w_out_bf, layer=j, tm=ts["out_tm"])
        else:
            u_v = _norm_matmul(xf, g_mix, od_w_in_bf, tm=ts["proj_tm"], tn=ts["gelu_proj_tn"],
                               out_dtype=BF16, gelu=True, layer=j)
            xf = _sgu_outproj(u_v, xf, row(od_ln_g[j]), row(od_ln_b[j]), od_w_s_f32,
                              od_b_s[j].T.astype(F32), od_w_out_bf, layer=j, tm=ts["sgu_tm"])
        xf = _ffn(xf, row(ffn_norm[i]), ffn_w_gate_bf, ffn_w_up_bf, ffn_w_down_f32, row(final_norm),
                  layer=i, tm=ts["ffn_tm"], th=ts["ffn_th"], final=(i == DEPTH - 1))
    return xf.reshape(batch, seq, d)
```
